```python
import jax, jax.numpy as jnp
from jax import lax
import numpy as np

D_MODEL = 2048
BATCH = 8
SEQ = 4096
DEPTH = 2

EPS = 1e-6
N_BRANCH = 4
D_FF = 4 * D_MODEL
POOL_DIM = 512
POOL_WINDOWS = (2, 4, 8, 16)
POOL_GROUPS = len(POOL_WINDOWS)
POOL_GDIM = POOL_DIM // POOL_GROUPS
CONV_DIM = 512
CONV_WIDTH = 31
SGU_DIM = 512
SGU_GROUPS = 4
SGU_GDIM = SGU_DIM // SGU_GROUPS
CHUNK = 128
MLA_HEADS = 8
Q_LORA = 512
KV_LORA = 512
QK_NOPE = 128
QK_ROPE = 64
V_DIM = 128
ROPE_THETA = 10000.0
ATTN_BLOCK = 128
OFF_POOL = 0
OFF_CONV = OFF_POOL + POOL_DIM
OFF_SGU = OFF_CONV + 2 * CONV_DIM
OFF_Q = OFF_SGU + 2 * SGU_DIM
OFF_KV = OFF_Q + Q_LORA
OFF_KR = OFF_KV + KV_LORA
OFF_GATE = OFF_KR + QK_ROPE
N_IN = OFF_GATE + N_BRANCH * D_MODEL

kernel_name = 'hybrid_gated_pool_conv_sgu_mla_block'


def rmsnorm(x, g):
    xf = x.astype(jnp.float32)
    y = xf * lax.rsqrt(jnp.mean(xf * xf, axis=-1, keepdims=True) + EPS)
    return (y * g.astype(jnp.float32)).astype(x.dtype)


def layernorm(x, g, b):
    xf = x.astype(jnp.float32)
    mu = jnp.mean(xf, axis=-1, keepdims=True)
    var = jnp.mean(jnp.square(xf - mu), axis=-1, keepdims=True)
    y = (xf - mu) * lax.rsqrt(var + EPS)
    return (y * g.astype(jnp.float32) + b.astype(jnp.float32)).astype(x.dtype)


def pool_mixer(a, pool_w, pool_scale):
    B, S, _ = a.shape
    af = a.astype(jnp.float32)
    csum = jnp.cumsum(af, axis=1)
    t = jnp.arange(S)
    means = []
    for gi, w in enumerate(POOL_WINDOWS):
        cs = csum[..., gi * POOL_GDIM:(gi + 1) * POOL_GDIM]
        lagged = jnp.pad(cs, ((0, 0), (w, 0), (0, 0)))[:, :S]
        count = jnp.minimum(t + 1, w).astype(jnp.float32)
        means.append((cs - lagged) / count[None, :, None])
    pooled = (jnp.concatenate(means, axis=-1) - af).astype(a.dtype)
    pooled = pooled.reshape(B, S, POOL_GROUPS, POOL_GDIM)
    mixed = jnp.einsum('bsgc,gcd->bsgd', pooled, pool_w).reshape(B, S, POOL_DIM)
    return mixed * pool_scale


def conformer_conv(c, conv_w, conv_b, norm_g, norm_b):
    a, gate = jnp.split(c, 2, axis=-1)
    glu = a * jax.nn.sigmoid(gate)
    padded = jnp.pad(glu, ((0, 0), (CONV_WIDTH - 1, 0), (0, 0)))
    y = lax.conv_general_dilated(
        padded, conv_w[:, None, :], window_strides=(1,), padding='VALID',
        dimension_numbers=('NWC', 'WIO', 'NWC'), feature_group_count=CONV_DIM)
    y = layernorm(y + conv_b, norm_g, norm_b)
    return jax.nn.silu(y)


def spatial_gating(z, norm_g, norm_b, w_s, b_s):
    z = jax.nn.gelu(z)
    u, v = jnp.split(z, 2, axis=-1)
    v = layernorm(v, norm_g, norm_b)
    B, S, _ = v.shape
    v = v.reshape(B, S // CHUNK, CHUNK, SGU_GROUPS, SGU_GDIM)
    mask = jnp.tril(jnp.ones((CHUNK, CHUNK), dtype=bool))
    w = jnp.where(mask[None], w_s, 0)
    sp = jnp.einsum('gts,bnsgc->bntgc', w, v) + b_s.T[None, None, :, :, None]
    return u * sp.reshape(B, S, SGU_DIM)


def apply_rope(x, cos, sin):
    x1, x2 = jnp.split(x.astype(jnp.float32), 2, axis=-1)
    return jnp.concatenate([x1 * cos - x2 * sin, x2 * cos + x1 * sin], axis=-1).astype(x.dtype)


def latent_attention(cq, ckv, kr, cos, sin, q_norm_g, w_uq, kv_norm_g, w_ukv, attn_proj):
    B, S, _ = cq.shape
    q = (rmsnorm(cq, q_norm_g) @ w_uq).reshape(B, S, MLA_HEADS, QK_NOPE + QK_ROPE)
    q_nope = q[..., :QK_NOPE]
    q_rope = apply_rope(q[..., QK_NOPE:], cos[:, :, None], sin[:, :, None])
    kv = (rmsnorm(ckv, kv_norm_g) @ w_ukv).reshape(B, S, MLA_HEADS, QK_NOPE + V_DIM)
    k_nope, v = kv[..., :QK_NOPE], kv[..., QK_NOPE:]
    k_rope = apply_rope(kr, cos, sin)
    scale = (QK_NOPE + QK_ROPE) ** -0.5
    outs = []
    for i in range(S // ATTN_BLOCK):
        q0, q1 = i * ATTN_BLOCK, (i + 1) * ATTN_BLOCK
        s = (jnp.einsum('bqhd,bkhd->bhqk', q_nope[:, q0:q1], k_nope[:, :q1])
             + jnp.einsum('bqhd,bkd->bhqk', q_rope[:, q0:q1], k_rope[:, :q1]))
        s = s.astype(jnp.float32) * scale
        mask = jnp.arange(q1)[None, :] <= jnp.arange(q0, q1)[:, None]
        s = jnp.where(mask, s, jnp.finfo(jnp.float32).min)
        p = jax.nn.softmax(s, axis=-1).astype(v.dtype)
        outs.append(jnp.einsum('bhqk,bkhd->bqhd', p, v[:, :q1]))
    o = jnp.concatenate(outs, axis=1).reshape(B, S, MLA_HEADS * V_DIM)
    return o @ attn_proj


def _normal(key, shape, scale):
    return jax.random.normal(key, shape, jnp.float32) * scale


def _gain(key, n):
    return 1.0 + 0.02 * jax.random.normal(key, (DEPTH, n), jnp.float32)


def _bias(key, n):
    return 0.02 * jax.random.normal(key, (DEPTH, n), jnp.float32)


def _fwd_setup_inputs(seed: int = 0) -> dict:
    key = jax.random.key(seed)
    ks = jax.random.split(key, 27)
    L = DEPTH
    return {
        'x': _normal(ks[0], (BATCH, SEQ, D_MODEL), 1.0),
        'positions': jnp.broadcast_to(jnp.arange(SEQ, dtype=jnp.int32)[None, :], (BATCH, SEQ)),
        'pre_mix_g': _gain(ks[1], D_MODEL),
        'w_in': _normal(ks[2], (L, D_MODEL, N_IN), D_MODEL ** -0.5),
        'pool_w': _normal(ks[3], (L, POOL_GROUPS, POOL_GDIM, POOL_GDIM), POOL_GDIM ** -0.5),
        'pool_scale': _gain(ks[4], POOL_DIM),
        'pool_proj': _normal(ks[5], (L, POOL_DIM, D_MODEL), POOL_DIM ** -0.5),
        'conv_w': _normal(ks[6], (L, CONV_WIDTH, CONV_DIM), CONV_WIDTH ** -0.5),
        'conv_b': _bias(ks[7], CONV_DIM),
        'conv_norm_g': _gain(ks[8], CONV_DIM),
        'conv_norm_b': _bias(ks[9], CONV_DIM),
        'conv_proj': _normal(ks[10], (L, CONV_DIM, D_MODEL), CONV_DIM ** -0.5),
        'sgu_norm_g': _gain(ks[11], SGU_DIM),
        'sgu_norm_b': _bias(ks[12], SGU_DIM),
        'sgu_w': _normal(ks[13], (L, SGU_GROUPS, CHUNK, CHUNK), CHUNK ** -0.5),
        'sgu_b': 1.0 + 0.02 * jax.random.normal(ks[14], (L, SGU_GROUPS, CHUNK), jnp.float32),
        'sgu_proj': _normal(ks[15], (L, SGU_DIM, D_MODEL), SGU_DIM ** -0.5),
        'q_norm_g': _gain(ks[16], Q_LORA),
        'w_uq': _normal(ks[17], (L, Q_LORA, MLA_HEADS * (QK_NOPE + QK_ROPE)), Q_LORA ** -0.5),
        'kv_norm_g': _gain(ks[18], KV_LORA),
        'w_ukv': _normal(ks[19], (L, KV_LORA, MLA_HEADS * (QK_NOPE + V_DIM)), KV_LORA ** -0.5),
        'attn_proj': _normal(ks[20], (L, MLA_HEADS * V_DIM, D_MODEL), (MLA_HEADS * V_DIM) ** -0.5),
        'w_out': _normal(ks[21], (L, D_MODEL, D_MODEL), D_MODEL ** -0.5),
        'post_mix_g': _gain(ks[22], D_MODEL),
        'pre_mlp_g': _gain(ks[23], D_MODEL),
        'w_up': _normal(ks[24], (L, D_MODEL, D_FF), D_MODEL ** -0.5),
        'w_down': _normal(ks[25], (L, D_FF, D_MODEL), D_FF ** -0.5),
        'post_mlp_g': _gain(ks[26], D_MODEL),
    }


def _fwd_reference(x, positions, pre_mix_g, w_in, pool_w, pool_scale, pool_proj, conv_w, conv_b,
              conv_norm_g, conv_norm_b, conv_proj, sgu_norm_g, sgu_norm_b, sgu_w, sgu_b,
              sgu_proj, q_norm_g, w_uq, kv_norm_g, w_ukv, attn_proj, w_out, post_mix_g,
              pre_mlp_g, w_up, w_down, post_mlp_g):
    B, S, _ = x.shape
    inv_freq = ROPE_THETA ** (-jnp.arange(0, QK_ROPE, 2, dtype=jnp.float32) / QK_ROPE)
    ang = positions.astype(jnp.float32)[..., None] * inv_freq
    cos, sin = jnp.cos(ang), jnp.sin(ang)
    for l in range(DEPTH):
        h = rmsnorm(x, pre_mix_g[l])
        z = h @ w_in[l]
        y_pool = pool_mixer(z[..., OFF_POOL:OFF_CONV], pool_w[l], pool_scale[l]) @ pool_proj[l]
        y_conv = conformer_conv(z[..., OFF_CONV:OFF_SGU], conv_w[l], conv_b[l],
                                conv_norm_g[l], conv_norm_b[l]) @ conv_proj[l]
        y_sgu = spatial_gating(z[..., OFF_SGU:OFF_Q], sgu_norm_g[l], sgu_norm_b[l],
                               sgu_w[l], sgu_b[l]) @ sgu_proj[l]
        y_attn = latent_attention(z[..., OFF_Q:OFF_KV], z[..., OFF_KV:OFF_KR],
                                  z[..., OFF_KR:OFF_GATE], cos, sin, q_norm_g[l], w_uq[l],
                                  kv_norm_g[l], w_ukv[l], attn_proj[l])
        gates = jax.nn.sigmoid(z[..., OFF_GATE:].reshape(B, S, N_BRANCH, D_MODEL))
        merged = (gates[:, :, 0] * y_pool + gates[:, :, 1] * y_conv
                  + gates[:, :, 2] * y_sgu + gates[:, :, 3] * y_attn)
        x = x + rmsnorm(merged @ w_out[l], post_mix_g[l])
        h = rmsnorm(x, pre_mlp_g[l])
        f = jnp.square(jax.nn.relu(h @ w_up[l])) @ w_down[l]
        x = x + rmsnorm(f, post_mlp_g[l])
    return x


import jax as _jax
import jax.numpy as _jnp

TWIN_FORMAT = 'train_step'
FWD_PARAMS = ['x', 'positions', 'pre_mix_g', 'w_in', 'pool_w', 'pool_scale', 'pool_proj', 'conv_w', 'conv_b', 'conv_norm_g', 'conv_norm_b', 'conv_proj', 'sgu_norm_g', 'sgu_norm_b', 'sgu_w', 'sgu_b', 'sgu_proj', 'q_norm_g', 'w_uq', 'kv_norm_g', 'w_ukv', 'attn_proj', 'w_out', 'post_mix_g', 'pre_mlp_g', 'w_up', 'w_down', 'post_mlp_g']
TWIN_WEIGHTS = ['pre_mix_g', 'w_in', 'pool_w', 'pool_scale', 'pool_proj', 'conv_w', 'conv_b', 'conv_norm_g', 'conv_norm_b', 'conv_proj', 'sgu_norm_g', 'sgu_norm_b', 'sgu_w', 'sgu_b', 'sgu_proj', 'q_norm_g', 'w_uq', 'kv_norm_g', 'w_ukv', 'attn_proj', 'w_out', 'post_mix_g', 'pre_mlp_g', 'w_up', 'w_down', 'post_mlp_g']
TWIN_DIFF_INPUT = 'x'
TWIN_INPUTS = ['x', 'positions', 'pre_mix_g', 'w_in', 'pool_w', 'pool_scale', 'pool_proj', 'conv_w', 'conv_b', 'conv_norm_g', 'conv_norm_b', 'conv_proj', 'sgu_norm_g', 'sgu_norm_b', 'sgu_w', 'sgu_b', 'sgu_proj', 'q_norm_g', 'w_uq', 'kv_norm_g', 'w_ukv', 'attn_proj', 'w_out', 'post_mix_g', 'pre_mlp_g', 'w_up', 'w_down', 'post_mlp_g', 'loss_target', 'm_pre_mix_g', 'm_w_in', 'm_pool_w', 'm_pool_scale', 'm_pool_proj', 'm_conv_w', 'm_conv_b', 'm_conv_norm_g', 'm_conv_norm_b', 'm_conv_proj', 'm_sgu_norm_g', 'm_sgu_norm_b', 'm_sgu_w', 'm_sgu_b', 'm_sgu_proj', 'm_q_norm_g', 'm_w_uq', 'm_kv_norm_g', 'm_w_ukv', 'm_attn_proj', 'm_w_out', 'm_post_mix_g', 'm_pre_mlp_g', 'm_w_up', 'm_w_down', 'm_post_mlp_g', 'v_pre_mix_g', 'v_w_in', 'v_pool_w', 'v_pool_scale', 'v_pool_proj', 'v_conv_w', 'v_conv_b', 'v_conv_norm_g', 'v_conv_norm_b', 'v_conv_proj', 'v_sgu_norm_g', 'v_sgu_norm_b', 'v_sgu_w', 'v_sgu_b', 'v_sgu_proj', 'v_q_norm_g', 'v_w_uq', 'v_kv_norm_g', 'v_w_ukv', 'v_attn_proj', 'v_w_out', 'v_post_mix_g', 'v_pre_mlp_g', 'v_w_up', 'v_w_down', 'v_post_mlp_g']
TWIN_OUTPUTS = ['loss', 'grad_x', 'grad_pre_mix_g', 'grad_w_in', 'grad_pool_w', 'grad_pool_scale', 'grad_pool_proj', 'grad_conv_w', 'grad_conv_b', 'grad_conv_norm_g', 'grad_conv_norm_b', 'grad_conv_proj', 'grad_sgu_norm_g', 'grad_sgu_norm_b', 'grad_sgu_w', 'grad_sgu_b', 'grad_sgu_proj', 'grad_q_norm_g', 'grad_w_uq', 'grad_kv_norm_g', 'grad_w_ukv', 'grad_attn_proj', 'grad_w_out', 'grad_post_mix_g', 'grad_pre_mlp_g', 'grad_w_up', 'grad_w_down', 'grad_post_mlp_g', 'delta_pre_mix_g', 'delta_w_in', 'delta_pool_w', 'delta_pool_scale', 'delta_pool_proj', 'delta_conv_w', 'delta_conv_b', 'delta_conv_norm_g', 'delta_conv_norm_b', 'delta_conv_proj', 'delta_sgu_norm_g', 'delta_sgu_norm_b', 'delta_sgu_w', 'delta_sgu_b', 'delta_sgu_proj', 'delta_q_norm_g', 'delta_w_uq', 'delta_kv_norm_g', 'delta_w_ukv', 'delta_attn_proj', 'delta_w_out', 'delta_post_mix_g', 'delta_pre_mlp_g', 'delta_w_up', 'delta_w_down', 'delta_post_mlp_g', 'new_m_pre_mix_g', 'new_m_w_in', 'new_m_pool_w', 'new_m_pool_scale', 'new_m_pool_proj', 'new_m_conv_w', 'new_m_conv_b', 'new_m_conv_norm_g', 'new_m_conv_norm_b', 'new_m_conv_proj', 'new_m_sgu_norm_g', 'new_m_sgu_norm_b', 'new_m_sgu_w', 'new_m_sgu_b', 'new_m_sgu_proj', 'new_m_q_norm_g', 'new_m_w_uq', 'new_m_kv_norm_g', 'new_m_w_ukv', 'new_m_attn_proj', 'new_m_w_out', 'new_m_post_mix_g', 'new_m_pre_mlp_g', 'new_m_w_up', 'new_m_w_down', 'new_m_post_mlp_g', 'new_v_pre_mix_g', 'new_v_w_in', 'new_v_pool_w', 'new_v_pool_scale', 'new_v_pool_proj', 'new_v_conv_w', 'new_v_conv_b', 'new_v_conv_norm_g', 'new_v_conv_norm_b', 'new_v_conv_proj', 'new_v_sgu_norm_g', 'new_v_sgu_norm_b', 'new_v_sgu_w', 'new_v_sgu_b', 'new_v_sgu_proj', 'new_v_q_norm_g', 'new_v_w_uq', 'new_v_kv_norm_g', 'new_v_w_ukv', 'new_v_attn_proj', 'new_v_w_out', 'new_v_post_mix_g', 'new_v_pre_mlp_g', 'new_v_w_up', 'new_v_w_down', 'new_v_post_mlp_g']
TWIN_LEAF_KINDS = {'loss': 'loss', 'grad_x': 'grad_x', 'grad_pre_mix_g': 'grad_w', 'grad_w_in': 'grad_w', 'grad_pool_w': 'grad_w', 'grad_pool_scale': 'grad_w', 'grad_pool_proj': 'grad_w', 'grad_conv_w': 'grad_w', 'grad_conv_b': 'grad_w', 'grad_conv_norm_g': 'grad_w', 'grad_conv_norm_b': 'grad_w', 'grad_conv_proj': 'grad_w', 'grad_sgu_norm_g': 'grad_w', 'grad_sgu_norm_b': 'grad_w', 'grad_sgu_w': 'grad_w', 'grad_sgu_b': 'grad_w', 'grad_sgu_proj': 'grad_w', 'grad_q_norm_g': 'grad_w', 'grad_w_uq': 'grad_w', 'grad_kv_norm_g': 'grad_w', 'grad_w_ukv': 'grad_w', 'grad_attn_proj': 'grad_w', 'grad_w_out': 'grad_w', 'grad_post_mix_g': 'grad_w', 'grad_pre_mlp_g': 'grad_w', 'grad_w_up': 'grad_w', 'grad_w_down': 'grad_w', 'grad_post_mlp_g': 'grad_w', 'delta_pre_mix_g': 'delta_w', 'delta_w_in': 'delta_w', 'delta_pool_w': 'delta_w', 'delta_pool_scale': 'delta_w', 'delta_pool_proj': 'delta_w', 'delta_conv_w': 'delta_w', 'delta_conv_b': 'delta_w', 'delta_conv_norm_g': 'delta_w', 'delta_conv_norm_b': 'delta_w', 'delta_conv_proj': 'delta_w', 'delta_sgu_norm_g': 'delta_w', 'delta_sgu_norm_b': 'delta_w', 'delta_sgu_w': 'delta_w', 'delta_sgu_b': 'delta_w', 'delta_sgu_proj': 'delta_w', 'delta_q_norm_g': 'delta_w', 'delta_w_uq': 'delta_w', 'delta_kv_norm_g': 'delta_w', 'delta_w_ukv': 'delta_w', 'delta_attn_proj': 'delta_w', 'delta_w_out': 'delta_w', 'delta_post_mix_g': 'delta_w', 'delta_pre_mlp_g': 'delta_w', 'delta_w_up': 'delta_w', 'delta_w_down': 'delta_w', 'delta_post_mlp_g': 'delta_w', 'new_m_pre_mix_g': 'new_m', 'new_m_w_in': 'new_m', 'new_m_pool_w': 'new_m', 'new_m_pool_scale': 'new_m', 'new_m_pool_proj': 'new_m', 'new_m_conv_w': 'new_m', 'new_m_conv_b': 'new_m', 'new_m_conv_norm_g': 'new_m', 'new_m_conv_norm_b': 'new_m', 'new_m_conv_proj': 'new_m', 'new_m_sgu_norm_g': 'new_m', 'new_m_sgu_norm_b': 'new_m', 'new_m_sgu_w': 'new_m', 'new_m_sgu_b': 'new_m', 'new_m_sgu_proj': 'new_m', 'new_m_q_norm_g': 'new_m', 'new_m_w_uq': 'new_m', 'new_m_kv_norm_g': 'new_m', 'new_m_w_ukv': 'new_m', 'new_m_attn_proj': 'new_m', 'new_m_w_out': 'new_m', 'new_m_post_mix_g': 'new_m', 'new_m_pre_mlp_g': 'new_m', 'new_m_w_up': 'new_m', 'new_m_w_down': 'new_m', 'new_m_post_mlp_g': 'new_m', 'new_v_pre_mix_g': 'new_v', 'new_v_w_in': 'new_v', 'new_v_pool_w': 'new_v', 'new_v_pool_scale': 'new_v', 'new_v_pool_proj': 'new_v', 'new_v_conv_w': 'new_v', 'new_v_conv_b': 'new_v', 'new_v_conv_norm_g': 'new_v', 'new_v_conv_norm_b': 'new_v', 'new_v_conv_proj': 'new_v', 'new_v_sgu_norm_g': 'new_v', 'new_v_sgu_norm_b': 'new_v', 'new_v_sgu_w': 'new_v', 'new_v_sgu_b': 'new_v', 'new_v_sgu_proj': 'new_v', 'new_v_q_norm_g': 'new_v', 'new_v_w_uq': 'new_v', 'new_v_kv_norm_g': 'new_v', 'new_v_w_ukv': 'new_v', 'new_v_attn_proj': 'new_v', 'new_v_w_out': 'new_v', 'new_v_post_mix_g': 'new_v', 'new_v_pre_mlp_g': 'new_v', 'new_v_w_up': 'new_v', 'new_v_w_down': 'new_v', 'new_v_post_mlp_g': 'new_v'}


def _forward(args):
    return _fwd_reference(*[args[k] for k in FWD_PARAMS])


def _output_shape():
    out = _jax.eval_shape(lambda: _forward(_fwd_setup_inputs(0)))
    return out.shape, out.dtype

N_MICROBATCH = 1
ADAM_LR = 0.001
ADAM_B1 = 0.9
ADAM_B2 = 0.999
ADAM_EPS = 1e-08
ADAM_WD = 0.01
ADAM_STEP = 10
PER_EXAMPLE_BATCH_AXIS = {'x': 0, 'positions': 0, 'loss_target': 0}
SHARED_INPUTS = []
_WEIGHT_DTYPES = {'pre_mix_g': _jnp.float32, 'w_in': _jnp.float32, 'pool_w': _jnp.float32, 'pool_scale': _jnp.float32, 'pool_proj': _jnp.float32, 'conv_w': _jnp.float32, 'conv_b': _jnp.float32, 'conv_norm_g': _jnp.float32, 'conv_norm_b': _jnp.float32, 'conv_proj': _jnp.float32, 'sgu_norm_g': _jnp.float32, 'sgu_norm_b': _jnp.float32, 'sgu_w': _jnp.float32, 'sgu_b': _jnp.float32, 'sgu_proj': _jnp.float32, 'q_norm_g': _jnp.float32, 'w_uq': _jnp.float32, 'kv_norm_g': _jnp.float32, 'w_ukv': _jnp.float32, 'attn_proj': _jnp.float32, 'w_out': _jnp.float32, 'post_mix_g': _jnp.float32, 'pre_mlp_g': _jnp.float32, 'w_up': _jnp.float32, 'w_down': _jnp.float32, 'post_mlp_g': _jnp.float32}
MOMENT_SCALE = {'pre_mix_g': 1.727393e+00, 'w_in': 6.872504e-01, 'pool_w': 7.366863e-01, 'pool_scale': 9.305509e-01, 'pool_proj': 4.087239e-01, 'conv_w': 1.580255e+00, 'conv_b': 1.505505e+01, 'conv_norm_g': 6.101928e+00, 'conv_norm_b': 8.835231e+00, 'conv_proj': 2.015865e+00, 'sgu_norm_g': 2.495644e-01, 'sgu_norm_b': 2.752298e-01, 'sgu_w': 2.063568e-01, 'sgu_b': 3.295505e-01, 'sgu_proj': 2.536902e+00, 'q_norm_g': 8.798703e-02, 'w_uq': 5.035018e-02, 'kv_norm_g': 3.179127e+00, 'w_ukv': 1.579348e+00, 'attn_proj': 1.558434e+00, 'w_out': 3.538153e+00, 'post_mix_g': 1.655087e+01, 'pre_mlp_g': 1.606856e+00, 'w_up': 8.025084e-01, 'w_down': 4.665120e+00, 'post_mlp_g': 1.739073e+01}


def _to_microbatches(a, axis):
    t = _jnp.moveaxis(a, axis, 0)
    t = t.reshape((N_MICROBATCH, t.shape[0] // N_MICROBATCH) + t.shape[1:])
    return _jnp.moveaxis(t, 1, axis + 1)


def setup_inputs(seed: int = 0) -> dict:
    inp = _fwd_setup_inputs(seed)
    key = _jax.random.fold_in(_jax.random.key(seed), 7919)
    shape, _ = _output_shape()
    out = dict(inp)
    out["loss_target"] = _jax.random.normal(_jax.random.fold_in(key, 0), shape, _jnp.float32)
    for i, name in enumerate(TWIN_WEIGHTS):
        w = inp[name].astype(_jnp.float32)
        if MOMENT_SCALE is None:
            s = _jnp.sqrt(_jnp.mean(_jnp.square(w)) + 1e-30)
        else:
            s = MOMENT_SCALE[name]
        km, kv = _jax.random.split(_jax.random.fold_in(key, i + 1))
        out[name] = w
        out["m_" + name] = s * _jax.random.normal(km, w.shape, _jnp.float32)
        out["v_" + name] = (s * s) * _jax.random.uniform(kv, w.shape, _jnp.float32, 0.5, 1.5)
    if N_MICROBATCH > 1:
        for name, axis in PER_EXAMPLE_BATCH_AXIS.items():
            out[name] = _to_microbatches(out[name], axis)
    return {'x': out['x'], 'positions': out['positions'], 'pre_mix_g': out['pre_mix_g'], 'w_in': out['w_in'], 'pool_w': out['pool_w'], 'pool_scale': out['pool_scale'], 'pool_proj': out['pool_proj'], 'conv_w': out['conv_w'], 'conv_b': out['conv_b'], 'conv_norm_g': out['conv_norm_g'], 'conv_norm_b': out['conv_norm_b'], 'conv_proj': out['conv_proj'], 'sgu_norm_g': out['sgu_norm_g'], 'sgu_norm_b': out['sgu_norm_b'], 'sgu_w': out['sgu_w'], 'sgu_b': out['sgu_b'], 'sgu_proj': out['sgu_proj'], 'q_norm_g': out['q_norm_g'], 'w_uq': out['w_uq'], 'kv_norm_g': out['kv_norm_g'], 'w_ukv': out['w_ukv'], 'attn_proj': out['attn_proj'], 'w_out': out['w_out'], 'post_mix_g': out['post_mix_g'], 'pre_mlp_g': out['pre_mlp_g'], 'w_up': out['w_up'], 'w_down': out['w_down'], 'post_mlp_g': out['post_mlp_g'], 'loss_target': out['loss_target'], 'm_pre_mix_g': out['m_pre_mix_g'], 'm_w_in': out['m_w_in'], 'm_pool_w': out['m_pool_w'], 'm_pool_scale': out['m_pool_scale'], 'm_pool_proj': out['m_pool_proj'], 'm_conv_w': out['m_conv_w'], 'm_conv_b': out['m_conv_b'], 'm_conv_norm_g': out['m_conv_norm_g'], 'm_conv_norm_b': out['m_conv_norm_b'], 'm_conv_proj': out['m_conv_proj'], 'm_sgu_norm_g': out['m_sgu_norm_g'], 'm_sgu_norm_b': out['m_sgu_norm_b'], 'm_sgu_w': out['m_sgu_w'], 'm_sgu_b': out['m_sgu_b'], 'm_sgu_proj': out['m_sgu_proj'], 'm_q_norm_g': out['m_q_norm_g'], 'm_w_uq': out['m_w_uq'], 'm_kv_norm_g': out['m_kv_norm_g'], 'm_w_ukv': out['m_w_ukv'], 'm_attn_proj': out['m_attn_proj'], 'm_w_out': out['m_w_out'], 'm_post_mix_g': out['m_post_mix_g'], 'm_pre_mlp_g': out['m_pre_mlp_g'], 'm_w_up': out['m_w_up'], 'm_w_down': out['m_w_down'], 'm_post_mlp_g': out['m_post_mlp_g'], 'v_pre_mix_g': out['v_pre_mix_g'], 'v_w_in': out['v_w_in'], 'v_pool_w': out['v_pool_w'], 'v_pool_scale': out['v_pool_scale'], 'v_pool_proj': out['v_pool_proj'], 'v_conv_w': out['v_conv_w'], 'v_conv_b': out['v_conv_b'], 'v_conv_norm_g': out['v_conv_norm_g'], 'v_conv_norm_b': out['v_conv_norm_b'], 'v_conv_proj': out['v_conv_proj'], 'v_sgu_norm_g': out['v_sgu_norm_g'], 'v_sgu_norm_b': out['v_sgu_norm_b'], 'v_sgu_w': out['v_sgu_w'], 'v_sgu_b': out['v_sgu_b'], 'v_sgu_proj': out['v_sgu_proj'], 'v_q_norm_g': out['v_q_norm_g'], 'v_w_uq': out['v_w_uq'], 'v_kv_norm_g': out['v_kv_norm_g'], 'v_w_ukv': out['v_w_ukv'], 'v_attn_proj': out['v_attn_proj'], 'v_w_out': out['v_w_out'], 'v_post_mix_g': out['v_post_mix_g'], 'v_pre_mlp_g': out['v_pre_mlp_g'], 'v_w_up': out['v_w_up'], 'v_w_down': out['v_w_down'], 'v_post_mlp_g': out['v_post_mlp_g']}


def _loss(weights, diff, rest, loss_target):
    with _jax.named_scope("forward"):
        args = {**rest, TWIN_DIFF_INPUT: diff, **{k: w.astype(_WEIGHT_DTYPES[k]) for k, w in weights.items()}}
        y = _forward(args)
    with _jax.named_scope("loss_head"):
        err = _jnp.square(y.astype(_jnp.float32) - loss_target)
        return 0.5 * _jnp.sum(_jnp.mean(err, axis=-1)) if err.ndim else 0.5 * err


def _adamw(w, g, m, v):
    m = ADAM_B1 * m + (1.0 - ADAM_B1) * g
    v = ADAM_B2 * v + (1.0 - ADAM_B2) * _jnp.square(g)
    m_hat = m / (1.0 - ADAM_B1 ** ADAM_STEP)
    v_hat = v / (1.0 - ADAM_B2 ** ADAM_STEP)
    delta = -ADAM_LR * (m_hat / (_jnp.sqrt(v_hat) + ADAM_EPS) + ADAM_WD * w)
    return delta, m, v


def reference(x, positions, pre_mix_g, w_in, pool_w, pool_scale, pool_proj, conv_w, conv_b, conv_norm_g, conv_norm_b, conv_proj, sgu_norm_g, sgu_norm_b, sgu_w, sgu_b, sgu_proj, q_norm_g, w_uq, kv_norm_g, w_ukv, attn_proj, w_out, post_mix_g, pre_mlp_g, w_up, w_down, post_mlp_g, loss_target, m_pre_mix_g, m_w_in, m_pool_w, m_pool_scale, m_pool_proj, m_conv_w, m_conv_b, m_conv_norm_g, m_conv_norm_b, m_conv_proj, m_sgu_norm_g, m_sgu_norm_b, m_sgu_w, m_sgu_b, m_sgu_proj, m_q_norm_g, m_w_uq, m_kv_norm_g, m_w_ukv, m_attn_proj, m_w_out, m_post_mix_g, m_pre_mlp_g, m_w_up, m_w_down, m_post_mlp_g, v_pre_mix_g, v_w_in, v_pool_w, v_pool_scale, v_pool_proj, v_conv_w, v_conv_b, v_conv_norm_g, v_conv_norm_b, v_conv_proj, v_sgu_norm_g, v_sgu_norm_b, v_sgu_w, v_sgu_b, v_sgu_proj, v_q_norm_g, v_w_uq, v_kv_norm_g, v_w_ukv, v_attn_proj, v_w_out, v_post_mix_g, v_pre_mlp_g, v_w_up, v_w_down, v_post_mlp_g):
    given = dict(x=x, positions=positions, pre_mix_g=pre_mix_g, w_in=w_in, pool_w=pool_w, pool_scale=pool_scale, pool_proj=pool_proj, conv_w=conv_w, conv_b=conv_b, conv_norm_g=conv_norm_g, conv_norm_b=conv_norm_b, conv_proj=conv_proj, sgu_norm_g=sgu_norm_g, sgu_norm_b=sgu_norm_b, sgu_w=sgu_w, sgu_b=sgu_b, sgu_proj=sgu_proj, q_norm_g=q_norm_g, w_uq=w_uq, kv_norm_g=kv_norm_g, w_ukv=w_ukv, attn_proj=attn_proj, w_out=w_out, post_mix_g=post_mix_g, pre_mlp_g=pre_mlp_g, w_up=w_up, w_down=w_down, post_mlp_g=post_mlp_g, loss_target=loss_target, m_pre_mix_g=m_pre_mix_g, m_w_in=m_w_in, m_pool_w=m_pool_w, m_pool_scale=m_pool_scale, m_pool_proj=m_pool_proj, m_conv_w=m_conv_w, m_conv_b=m_conv_b, m_conv_norm_g=m_conv_norm_g, m_conv_norm_b=m_conv_norm_b, m_conv_proj=m_conv_proj, m_sgu_norm_g=m_sgu_norm_g, m_sgu_norm_b=m_sgu_norm_b, m_sgu_w=m_sgu_w, m_sgu_b=m_sgu_b, m_sgu_proj=m_sgu_proj, m_q_norm_g=m_q_norm_g, m_w_uq=m_w_uq, m_kv_norm_g=m_kv_norm_g, m_w_ukv=m_w_ukv, m_attn_proj=m_attn_proj, m_w_out=m_w_out, m_post_mix_g=m_post_mix_g, m_pre_mlp_g=m_pre_mlp_g, m_w_up=m_w_up, m_w_down=m_w_down, m_post_mlp_g=m_post_mlp_g, v_pre_mix_g=v_pre_mix_g, v_w_in=v_w_in, v_pool_w=v_pool_w, v_pool_scale=v_pool_scale, v_pool_proj=v_pool_proj, v_conv_w=v_conv_w, v_conv_b=v_conv_b, v_conv_norm_g=v_conv_norm_g, v_conv_norm_b=v_conv_norm_b, v_conv_proj=v_conv_proj, v_sgu_norm_g=v_sgu_norm_g, v_sgu_norm_b=v_sgu_norm_b, v_sgu_w=v_sgu_w, v_sgu_b=v_sgu_b, v_sgu_proj=v_sgu_proj, v_q_norm_g=v_q_norm_g, v_w_uq=v_w_uq, v_kv_norm_g=v_kv_norm_g, v_w_ukv=v_w_ukv, v_attn_proj=v_attn_proj, v_w_out=v_w_out, v_post_mix_g=v_post_mix_g, v_pre_mlp_g=v_pre_mlp_g, v_w_up=v_w_up, v_w_down=v_w_down, v_post_mlp_g=v_post_mlp_g)
    weights = {n: given[n] for n in TWIN_WEIGHTS}
    shared = {n: given[n] for n in SHARED_INPUTS}
    per_example = {n: given[n] for n in ['x', 'positions']}
    grad_fn = _jax.value_and_grad(_loss, argnums=(0, 1))

    def one_microbatch(ex, loss_target):
        ex = dict(ex)
        diff = ex.pop(TWIN_DIFF_INPUT)
        return grad_fn(weights, diff, {**shared, **ex}, loss_target)

    if N_MICROBATCH == 1:
        loss, (grad_w, grad_x) = one_microbatch(per_example, given["loss_target"])
    else:
        def body(carry, xs):
            loss_sum, grad_sum = carry
            l_k, (gw_k, gx_k) = one_microbatch(xs[0], xs[1])
            with _jax.named_scope("update"):
                return (loss_sum + l_k, _jax.tree.map(_jnp.add, grad_sum, gw_k)), gx_k

        init = (_jnp.zeros((), _jnp.float32), _jax.tree.map(_jnp.zeros_like, weights))
        (loss, grad_w), grad_x = _jax.lax.scan(body, init, (per_example, given["loss_target"]))
    with _jax.named_scope("update"):
        delta_w, new_m, new_v = {}, {}, {}
        for n in TWIN_WEIGHTS:
            delta_w[n], new_m[n], new_v[n] = _adamw(weights[n], grad_w[n], given["m_" + n], given["v_" + n])
    return (loss, grad_x, *[grad_w[n] for n in TWIN_WEIGHTS], *[delta_w[n] for n in TWIN_WEIGHTS],
            *[new_m[n] for n in TWIN_WEIGHTS], *[new_v[n] for n in TWIN_WEIGHTS])
```

```python
import functools
import math

import jax
import jax.numpy as jnp
from jax import lax
from jax.experimental import pallas as pl
from jax.experimental.pallas import tpu as pltpu

F32 = jnp.float32
BF16 = jnp.bfloat16
MESH = pl.DeviceIdType.MESH

EPS = 1e-6
POOL_WINDOWS = (2, 4, 8, 16)
GROUP = 128
BR = 512
CONV_WIDTH = 31
HEADS = 8
QK_NOPE = 128
QK_ROPE = 64
V_DIM = 128
ROPE_THETA = 10000.0
ATT_SCALE = (QK_NOPE + QK_ROPE) ** -0.5
GELU_C = math.sqrt(2.0 / math.pi)
ADAM_LR, ADAM_B1, ADAM_B2, ADAM_EPS, ADAM_WD, ADAM_STEP = 0.001, 0.9, 0.999, 1e-08, 0.01, 10

VMEM_LIMIT = 48 * 1024 * 1024
PACK_COLS = 1024
PACK_ROW_ALIGN = 16
CONV_HALO = 32
POOL_HALO = 16

TOK_WIDE = 256
TOK_NARROW = 512
ATT_TILE = 512
MM_TILES_M = (1024, 512, 256, 128)
MM_TILES_N = (1024, 512, 256, 128)
MM_TILES_K = (512, 256, 128)
EW_ROWS = 512

WEIGHT_NAMES = ['pre_mix_g', 'w_in', 'pool_w', 'pool_scale', 'pool_proj', 'conv_w', 'conv_b', 'conv_norm_g',
                'conv_norm_b', 'conv_proj', 'sgu_norm_g', 'sgu_norm_b', 'sgu_w', 'sgu_b', 'sgu_proj', 'q_norm_g',
                'w_uq', 'kv_norm_g', 'w_ukv', 'attn_proj', 'w_out', 'post_mix_g', 'pre_mlp_g', 'w_up', 'w_down',
                'post_mlp_g']
BIG = [('w_in', 'col'), ('pool_proj', 'col'), ('conv_proj', 'col'), ('sgu_proj', 'col'), ('w_uq', 'col'),
       ('w_ukv', 'col'), ('attn_proj', 'col'), ('w_out', 'row'), ('w_up', 'col'), ('w_down', 'row')]
BIG_NAMES = [n for n, _ in BIG]
SMALL_NAMES = [n for n in WEIGHT_NAMES if n not in BIG_NAMES]


def _bs(shape, index_map):
    return pl.BlockSpec(shape, index_map)


def _sds(shape, dtype):
    return jax.ShapeDtypeStruct(shape, dtype)


def _tile(n, candidates):
    for t in candidates:
        if n % t == 0:
            return t
    return n


def _call(body, *, name, grid, in_specs, out_specs, out_shape, scratch=(), sem=None):
    return pl.pallas_call(
        body, name=name, grid=grid, in_specs=in_specs, out_specs=out_specs, out_shape=out_shape,
        scratch_shapes=list(scratch),
        compiler_params=pltpu.CompilerParams(dimension_semantics=sem, vmem_limit_bytes=VMEM_LIMIT))


def _sigmoid(v):
    return 1.0 / (1.0 + jnp.exp(-v))


def _gelu(v):
    return 0.5 * v * (1.0 + jnp.tanh(GELU_C * (v + 0.044715 * v * v * v)))


def _gelu_grad(v):
    t = jnp.tanh(GELU_C * (v + 0.044715 * v * v * v))
    return 0.5 * (1.0 + t) + 0.5 * v * (1.0 - t * t) * GELU_C * (1.0 + 3.0 * 0.044715 * v * v)


def _rstd(v):
    return lax.rsqrt(jnp.mean(v * v, axis=-1, keepdims=True) + EPS)


def _rms_bwd(v, r, t):
    return r * t - v * (r * r * r) * jnp.mean(v * t, axis=-1, keepdims=True)


def _ln_stats(v):
    mu = jnp.mean(v, axis=-1, keepdims=True)
    d = v - mu
    r = lax.rsqrt(jnp.mean(d * d, axis=-1, keepdims=True) + EPS)
    return d * r, r


def _ln_bwd(xh, r, dxh):
    return r * (dxh - jnp.mean(dxh, axis=-1, keepdims=True) - xh * jnp.mean(dxh * xh, axis=-1, keepdims=True))


def _colsum(v):
    return jnp.sum(v, axis=0, keepdims=True)


def _dot(a, b):
    return jnp.dot(a, b, preferred_element_type=F32)


def _dot_nt(a, b):
    return lax.dot_general(a, b, (((1,), (1,)), ((), ())), preferred_element_type=F32)


def _dot_tn(a, b):
    return lax.dot_general(a, b, (((0,), (0,)), ((), ())), preferred_element_type=F32)


def _mm(a, b, *, name, ta=False, tb=False, out_dtypes=(F32,), epilogue=None, extras=()):
    m = a.shape[1] if ta else a.shape[0]
    k = a.shape[0] if ta else a.shape[1]
    n = b.shape[0] if tb else b.shape[1]
    assert k == (b.shape[1] if tb else b.shape[0])
    tm, tn, tk = _tile(m, MM_TILES_M), _tile(n, MM_TILES_N), _tile(k, MM_TILES_K)
    nk = k // tk
    n_extra, n_out = len(extras), len(out_dtypes)
    dims = (((0 if ta else 1,), (1 if tb else 0,)), ((), ()))

    def body(a_ref, b_ref, *rest):
        extra_refs, out_refs, acc = rest[:n_extra], rest[n_extra:n_extra + n_out], rest[-1]
        kk = pl.program_id(2)

        @pl.when(kk == 0)
        def _():
            acc[...] = jnp.zeros_like(acc)

        acc[...] += lax.dot_general(a_ref[...].astype(BF16), b_ref[...].astype(BF16), dims,
                                    preferred_element_type=F32)

        @pl.when(kk == nk - 1)
        def _():
            res = acc[...]
            res = (res,) if epilogue is None else epilogue(res, *[e[...] for e in extra_refs])
            for o, r in zip(out_refs, res):
                o[...] = r.astype(o.dtype)

    a_spec = _bs((tk, tm), lambda i, j, kk: (kk, i)) if ta else _bs((tm, tk), lambda i, j, kk: (i, kk))
    b_spec = _bs((tn, tk), lambda i, j, kk: (j, kk)) if tb else _bs((tk, tn), lambda i, j, kk: (kk, j))
    o_spec = _bs((tm, tn), lambda i, j, kk: (i, j))
    outs = _call(body, name=name, grid=(m // tm, n // tn, nk),
                 in_specs=[a_spec, b_spec] + [o_spec] * n_extra,
                 out_specs=[o_spec] * n_out, out_shape=[_sds((m, n), d) for d in out_dtypes],
                 scratch=[pltpu.VMEM((tm, tn), F32)], sem=("parallel", "parallel", "arbitrary"))(a, b, *extras)
    return outs[0] if n_out == 1 else outs


def _rms_fwd(x, g, name):
    s, d = x.shape
    tt = _tile(s, (TOK_WIDE,))

    def body(x_ref, g_ref, h_ref):
        v = x_ref[...]
        h_ref[...] = (v * _rstd(v) * g_ref[...]).astype(BF16)

    row = _bs((tt, d), lambda i: (i, 0))
    return _call(body, name=name, grid=(s // tt,), in_specs=[row, _bs((1, d), lambda i: (0, 0))],
                 out_specs=row, out_shape=_sds((s, d), BF16), sem=("parallel",))(x, g)


def _resid_norm_fwd(xres, y, g_post, g_next, name):
    s, d = xres.shape
    tt = _tile(s, (TOK_WIDE,))

    def body(xr_ref, y_ref, gp_ref, gn_ref, xn_ref, h_ref):
        yv = y_ref[...]
        xn = xr_ref[...] + yv * _rstd(yv) * gp_ref[...]
        xn_ref[...] = xn
        h_ref[...] = (xn * _rstd(xn) * gn_ref[...]).astype(BF16)

    row, vec = _bs((tt, d), lambda i: (i, 0)), _bs((1, d), lambda i: (0, 0))
    return _call(body, name=name, grid=(s // tt,), in_specs=[row, row, vec, vec], out_specs=[row, row],
                 out_shape=[_sds((s, d), F32), _sds((s, d), BF16)], sem=("parallel",))(xres, y, g_post, g_next)


def _resid_norm_loss(xres, y, g_post, target, name):
    s, d = xres.shape
    tt = _tile(s, (TOK_WIDE,))

    def body(xr_ref, y_ref, gp_ref, t_ref, dy_ref, loss_ref):
        @pl.when(pl.program_id(0) == 0)
        def _():
            loss_ref[...] = jnp.zeros_like(loss_ref)

        yv = y_ref[...]
        err = xr_ref[...] + yv * _rstd(yv) * gp_ref[...] - t_ref[...]
        dy_ref[...] = err * (1.0 / d)
        loss_ref[...] += 0.5 * jnp.sum(jnp.mean(err * err, axis=-1, keepdims=True))

    row, vec = _bs((tt, d), lambda i: (i, 0)), _bs((1, d), lambda i: (0, 0))
    dy, loss = _call(body, name=name, grid=(s // tt,), in_specs=[row, row, vec, row],
                     out_specs=[row, _bs((8, 128), lambda i: (0, 0))],
                     out_shape=[_sds((s, d), F32), _sds((8, 128), F32)], sem=("arbitrary",))(xres, y, g_post, target)
    return dy, loss[0, 0]


def _resid_norm_bwd(d_out, d_h, x_new, y, g_post, g_next, name):
    s, d = d_out.shape
    tt = _tile(s, (TOK_WIDE,))
    has_next, has_y = d_h is not None, y is not None

    def body(*refs):
        it = iter(refs)
        do_ref = next(it)
        if has_next:
            dh_ref, xn_ref, gn_ref = next(it), next(it), next(it)
        if has_y:
            y_ref, gp_ref = next(it), next(it)
        if has_next:
            dx_ref = next(it)
        if has_y:
            dy_ref, dgp_ref = next(it), next(it)
        if has_next:
            dgn_ref = next(it)

        first = pl.program_id(0) == 0
        dx = do_ref[...]
        if has_next:
            xn, dh = xn_ref[...], dh_ref[...]
            r = _rstd(xn)
            dx = dx + _rms_bwd(xn, r, dh * gn_ref[...])
            dx_ref[...] = dx

            @pl.when(first)
            def _():
                dgn_ref[...] = jnp.zeros_like(dgn_ref)

            dgn_ref[...] += _colsum(dh * xn * r)
        if has_y:
            yv = y_ref[...]
            ry = _rstd(yv)
            dy_ref[...] = _rms_bwd(yv, ry, dx * gp_ref[...]).astype(BF16)

            @pl.when(first)
            def _():
                dgp_ref[...] = jnp.zeros_like(dgp_ref)

            dgp_ref[...] += _colsum(dx * yv * ry)

    row, vec = _bs((tt, d), lambda i: (i, 0)), _bs((1, d), lambda i: (0, 0))
    args, in_specs, out_specs, out_shape = [d_out], [row], [], []
    if has_next:
        args += [d_h, x_new, g_next]
        in_specs += [row, row, vec]
    if has_y:
        args += [y, g_post]
        in_specs += [row, vec]
    if has_next:
        out_specs.append(row)
        out_shape.append(_sds((s, d), F32))
    if has_y:
        out_specs += [row, vec]
        out_shape += [_sds((s, d), BF16), _sds((1, d), F32)]
    if has_next:
        out_specs.append(vec)
        out_shape.append(_sds((1, d), F32))
    outs = list(_call(body, name=name, grid=(s // tt,), in_specs=in_specs, out_specs=out_specs, out_shape=out_shape,
                      sem=("arbitrary",))(*args))
    d_x = outs.pop(0) if has_next else None
    d_y, d_gp = (outs.pop(0), outs.pop(0)) if has_y else (None, None)
    d_gn = outs.pop(0) if has_next else None
    return d_x, d_y, d_gp, d_gn


def _pool_counts(t0, tt, w):
    t = t0 + lax.broadcasted_iota(jnp.int32, (tt, 1), 0)
    return jnp.minimum(t + 1, w).astype(F32)


def _pool_pooled(ext, a, t0, tt, g, w):
    cols = pl.ds(g * GROUP, GROUP)
    sm = ext[pl.ds(POOL_HALO, tt), cols]
    for j in range(1, w):
        sm = sm + ext[pl.ds(POOL_HALO - j, tt), cols]
    return sm / _pool_counts(t0, tt, w) - a[:, g * GROUP:(g + 1) * GROUP]


def _pool_fwd(z, cb, pool_w, pool_scale, name):
    s = z.shape[0]
    tt = _tile(s, (TOK_NARROW,))
    hb = tt // POOL_HALO

    def body(zc_ref, zp_ref, pw_ref, sc_ref, out_ref, ext):
        i = pl.program_id(0)
        a = zc_ref[...]
        ext[pl.ds(0, POOL_HALO), :] = jnp.where(i > 0, zp_ref[...], 0.0)
        ext[pl.ds(POOL_HALO, tt), :] = a
        for g, w in enumerate(POOL_WINDOWS):
            pooled = _pool_pooled(ext, a, i * tt, tt, g, w).astype(BF16)
            mixed = _dot(pooled, pw_ref[g])
            out_ref[:, g * GROUP:(g + 1) * GROUP] = (mixed * sc_ref[:, g * GROUP:(g + 1) * GROUP]).astype(BF16)

    return _call(body, name=name, grid=(s // tt,),
                 in_specs=[_bs((tt, BR), lambda i: (i, cb)),
                           _bs((POOL_HALO, BR), lambda i: (jnp.maximum(i * hb - 1, 0), cb * (BR // BR))),
                           _bs((len(POOL_WINDOWS), GROUP, GROUP), lambda i: (0, 0, 0)),
                           _bs((1, BR), lambda i: (0, 0))],
                 out_specs=_bs((tt, BR), lambda i: (i, 0)), out_shape=_sds((s, BR), BF16),
                 scratch=[pltpu.VMEM((tt + POOL_HALO, BR), F32)], sem=("parallel",))(z, z, pool_w, pool_scale)


def _pool_bwd(z, cb, d_b, pool_w, pool_scale, name):
    s = z.shape[0]
    tt = _tile(s, (TOK_NARROW,))
    nt, hb, ng = s // tt, tt // POOL_HALO, len(POOL_WINDOWS)

    def body(zc_ref, zp_ref, db_ref, pw_ref, sc_ref, dz_ref, dpw_ref, dsc_ref, ext, ext_e, carry):
        step = pl.program_id(0)
        i = nt - 1 - step

        @pl.when(step == 0)
        def _():
            dpw_ref[...] = jnp.zeros_like(dpw_ref)
            dsc_ref[...] = jnp.zeros_like(dsc_ref)
            carry[...] = jnp.zeros_like(carry)

        a = zc_ref[...]
        ext[pl.ds(0, POOL_HALO), :] = jnp.where(i > 0, zp_ref[...], 0.0)
        ext[pl.ds(POOL_HALO, tt), :] = a
        ext_e[pl.ds(tt, POOL_HALO), :] = carry[...]
        db = db_ref[...]
        for g, w in enumerate(POOL_WINDOWS):
            c0, c1 = g * GROUP, (g + 1) * GROUP
            pooled = _pool_pooled(ext, a, i * tt, tt, g, w).astype(BF16)
            mixed = _dot(pooled, pw_ref[g])
            dsc_ref[:, c0:c1] += _colsum(db[:, c0:c1] * mixed)
            dmixed = (db[:, c0:c1] * sc_ref[:, c0:c1]).astype(BF16)
            dpw_ref[g] += _dot_tn(pooled, dmixed)
            dpooled = _dot_nt(dmixed, pw_ref[g])
            ext_e[pl.ds(0, tt), pl.ds(c0, GROUP)] = dpooled / _pool_counts(i * tt, tt, w)
            acc = -dpooled
            for j in range(w):
                acc = acc + ext_e[pl.ds(j, tt), pl.ds(c0, GROUP)]
            dz_ref[:, c0:c1] = acc.astype(BF16)
        carry[...] = ext_e[pl.ds(0, POOL_HALO), :]

    rev = lambda st: nt - 1 - st
    dz, dpw, dsc = _call(
        body, name=name, grid=(nt,),
        in_specs=[_bs((tt, BR), lambda st: (rev(st), cb)),
                  _bs((POOL_HALO, BR), lambda st: (jnp.maximum(rev(st) * hb - 1, 0), cb)),
                  _bs((tt, BR), lambda st: (rev(st), 0)),
                  _bs((ng, GROUP, GROUP), lambda st: (0, 0, 0)),
                  _bs((1, BR), lambda st: (0, 0))],
        out_specs=[_bs((tt, BR), lambda st: (rev(st), 0)), _bs((ng, GROUP, GROUP), lambda st: (0, 0, 0)),
                   _bs((1, BR), lambda st: (0, 0))],
        out_shape=[_sds((s, BR), BF16), _sds((ng, GROUP, GROUP), F32), _sds((1, BR), F32)],
        scratch=[pltpu.VMEM((tt + POOL_HALO, BR), F32), pltpu.VMEM((tt + POOL_HALO, BR), F32),
                 pltpu.VMEM((POOL_HALO, BR), F32)],
        sem=("arbitrary",))(z, z, d_b, pool_w, pool_scale)
    return dz, dpw, dsc


def _conv_fwd(z, cb, conv_w, conv_b, ng, nb, name):
    s = z.shape[0]
    tt = _tile(s, (TOK_NARROW,))
    hb = tt // CONV_HALO

    def body(a_ref, g_ref, ap_ref, gp_ref, w_ref, b_ref, ng_ref, nb_ref, out_ref, ypre_ref, ext):
        i = pl.program_id(0)
        ext[pl.ds(0, CONV_HALO), :] = jnp.where(i > 0, ap_ref[...] * _sigmoid(gp_ref[...]), 0.0)
        ext[pl.ds(CONV_HALO, tt), :] = a_ref[...] * _sigmoid(g_ref[...])
        acc = jnp.zeros((tt, BR), F32)
        for k in range(CONV_WIDTH):
            acc = acc + w_ref[pl.ds(k, 1), :] * ext[pl.ds(CONV_HALO - (CONV_WIDTH - 1) + k, tt), :]
        ypre = acc + b_ref[...]
        ypre_ref[...] = ypre
        xh, _ = _ln_stats(ypre)
        yl = xh * ng_ref[...] + nb_ref[...]
        out_ref[...] = (yl * _sigmoid(yl)).astype(BF16)

    cur = lambda c: _bs((tt, BR), lambda i: (i, c))
    prev = lambda c: _bs((CONV_HALO, BR), lambda i: (jnp.maximum(i * hb - 1, 0), c))
    vec = _bs((1, BR), lambda i: (0, 0))
    row = _bs((tt, BR), lambda i: (i, 0))
    return _call(body, name=name, grid=(s // tt,),
                 in_specs=[cur(cb), cur(cb + 1), prev(cb), prev(cb + 1),
                           _bs((CONV_WIDTH, BR), lambda i: (0, 0)), vec, vec, vec],
                 out_specs=[row, row], out_shape=[_sds((s, BR), BF16), _sds((s, BR), F32)],
                 scratch=[pltpu.VMEM((tt + CONV_HALO, BR), F32)], sem=("parallel",))(
        z, z, z, z, conv_w, conv_b, ng, nb)


def _conv_bwd(z, cb, d_b, ypre, conv_w, ng, nb, name):
    s = z.shape[0]
    tt = _tile(s, (TOK_NARROW,))
    nt, hb = s // tt, tt // CONV_HALO
    lead = CONV_HALO - (CONV_WIDTH - 1)

    def body(a_ref, g_ref, ap_ref, gp_ref, db_ref, yp_ref, w_ref, ng_ref, nb_ref,
             da_ref, dg_ref, dw_ref, dcb_ref, dng_ref, dnb_ref, ext, ext_d, carry):
        step = pl.program_id(0)
        i = nt - 1 - step

        @pl.when(step == 0)
        def _():
            dw_ref[...] = jnp.zeros_like(dw_ref)
            dcb_ref[...] = jnp.zeros_like(dcb_ref)
            dng_ref[...] = jnp.zeros_like(dng_ref)
            dnb_ref[...] = jnp.zeros_like(dnb_ref)
            carry[...] = jnp.zeros_like(carry)

        xh, r = _ln_stats(yp_ref[...])
        yl = xh * ng_ref[...] + nb_ref[...]
        sg = _sigmoid(yl)
        dyl = db_ref[...] * (sg * (1.0 + yl * (1.0 - sg)))
        dng_ref[...] += _colsum(dyl * xh)
        dnb_ref[...] += _colsum(dyl)
        dypre = _ln_bwd(xh, r, dyl * ng_ref[...])
        dcb_ref[...] += _colsum(dypre)

        a, gate = a_ref[...], g_ref[...]
        sgate = _sigmoid(gate)
        ext[pl.ds(0, CONV_HALO), :] = jnp.where(i > 0, ap_ref[...] * _sigmoid(gp_ref[...]), 0.0)
        ext[pl.ds(CONV_HALO, tt), :] = a * sgate
        ext_d[pl.ds(0, tt), :] = dypre
        ext_d[pl.ds(tt, CONV_HALO), :] = carry[...]
        dglu = jnp.zeros((tt, BR), F32)
        for k in range(CONV_WIDTH):
            dw_ref[pl.ds(k, 1), :] += _colsum(dypre * ext[pl.ds(lead + k, tt), :])
            dglu = dglu + w_ref[pl.ds(k, 1), :] * ext_d[pl.ds(CONV_WIDTH - 1 - k, tt), :]
        carry[...] = ext_d[pl.ds(0, CONV_HALO), :]
        da_ref[...] = (dglu * sgate).astype(BF16)
        dg_ref[...] = (dglu * a * sgate * (1.0 - sgate)).astype(BF16)

    rev = lambda st: nt - 1 - st
    cur = lambda c: _bs((tt, BR), lambda st: (rev(st), c))
    prev = lambda c: _bs((CONV_HALO, BR), lambda st: (jnp.maximum(rev(st) * hb - 1, 0), c))
    vec = _bs((1, BR), lambda st: (0, 0))
    row = _bs((tt, BR), lambda st: (rev(st), 0))
    wsp = _bs((CONV_WIDTH, BR), lambda st: (0, 0))
    return _call(body, name=name, grid=(nt,),
                 in_specs=[cur(cb), cur(cb + 1), prev(cb), prev(cb + 1), row, row, wsp, vec, vec],
                 out_specs=[row, row, wsp, vec, vec, vec],
                 out_shape=[_sds((s, BR), BF16), _sds((s, BR), BF16), _sds((CONV_WIDTH, BR), F32),
                            _sds((1, BR), F32), _sds((1, BR), F32), _sds((1, BR), F32)],
                 scratch=[pltpu.VMEM((tt + CONV_HALO, BR), F32), pltpu.VMEM((tt + CONV_HALO, BR), F32),
                          pltpu.VMEM((CONV_HALO, BR), F32)],
                 sem=("arbitrary",))(z, z, z, z, d_b, ypre, conv_w, ng, nb)


def _sgu_fwd(z, cb, ng, nb, w_masked, bias_full, name):
    s = z.shape[0]
    tt = _tile(s, (TOK_NARROW,))
    ngr = BR // GROUP

    def body(u_ref, v_ref, ng_ref, nb_ref, w_ref, bb_ref, out_ref):
        ua = _gelu(u_ref[...])
        xh, _ = _ln_stats(_gelu(v_ref[...]))
        vn = (xh * ng_ref[...] + nb_ref[...]).astype(BF16)
        for n in range(tt // GROUP):
            for g in range(ngr):
                r0, c0 = n * GROUP, g * GROUP
                sp = _dot(w_ref[g], vn[r0:r0 + GROUP, c0:c0 + GROUP]) + bb_ref[g]
                out_ref[r0:r0 + GROUP, c0:c0 + GROUP] = (ua[r0:r0 + GROUP, c0:c0 + GROUP] * sp).astype(BF16)

    vec = _bs((1, BR), lambda i: (0, 0))
    sq = _bs((ngr, GROUP, GROUP), lambda i: (0, 0, 0))
    return _call(body, name=name, grid=(s // tt,),
                 in_specs=[_bs((tt, BR), lambda i: (i, cb)), _bs((tt, BR), lambda i: (i, cb + 1)), vec, vec, sq, sq],
                 out_specs=_bs((tt, BR), lambda i: (i, 0)), out_shape=_sds((s, BR), BF16),
                 sem=("parallel",))(z, z, ng, nb, w_masked, bias_full)


def _sgu_bwd(z, cb, d_b, ng, nb, w_masked, bias_full, name):
    s = z.shape[0]
    tt = _tile(s, (TOK_NARROW,))
    ngr = BR // GROUP

    def body(u_ref, v_ref, db_ref, ng_ref, nb_ref, w_ref, bb_ref,
             du_ref, dv_ref, dw_ref, dbias_ref, dng_ref, dnb_ref, dvn_s):
        @pl.when(pl.program_id(0) == 0)
        def _():
            dw_ref[...] = jnp.zeros_like(dw_ref)
            dbias_ref[...] = jnp.zeros_like(dbias_ref)
            dng_ref[...] = jnp.zeros_like(dng_ref)
            dnb_ref[...] = jnp.zeros_like(dnb_ref)

        u, v, db = u_ref[...], v_ref[...], db_ref[...]
        ua = _gelu(u)
        xh, r = _ln_stats(_gelu(v))
        vn = (xh * ng_ref[...] + nb_ref[...]).astype(BF16)
        for n in range(tt // GROUP):
            for g in range(ngr):
                rows, cols = slice(n * GROUP, (n + 1) * GROUP), slice(g * GROUP, (g + 1) * GROUP)
                vn_c = vn[rows, cols]
                sp = _dot(w_ref[g], vn_c) + bb_ref[g]
                du_ref[rows, cols] = (db[rows, cols] * sp * _gelu_grad(u[rows, cols])).astype(BF16)
                dsp = db[rows, cols] * ua[rows, cols]
                dbias_ref[g] += dsp
                dsp16 = dsp.astype(BF16)
                dw_ref[g] += _dot_nt(dsp16, vn_c)
                dvn_s[rows, cols] = _dot_tn(w_ref[g], dsp16)
        dvn = dvn_s[...]
        dng_ref[...] += _colsum(dvn * xh)
        dnb_ref[...] += _colsum(dvn)
        dv_ref[...] = (_ln_bwd(xh, r, dvn * ng_ref[...]) * _gelu_grad(v)).astype(BF16)

    vec = _bs((1, BR), lambda i: (0, 0))
    sq = _bs((ngr, GROUP, GROUP), lambda i: (0, 0, 0))
    row = _bs((tt, BR), lambda i: (i, 0))
    return _call(body, name=name, grid=(s // tt,),
                 in_specs=[_bs((tt, BR), lambda i: (i, cb)), _bs((tt, BR), lambda i: (i, cb + 1)), row, vec, vec, sq, sq],
                 out_specs=[row, row, sq, sq, vec, vec],
                 out_shape=[_sds((s, BR), BF16), _sds((s, BR), BF16), _sds((ngr, GROUP, GROUP), F32),
                            _sds((ngr, GROUP, GROUP), F32), _sds((1, BR), F32), _sds((1, BR), F32)],
                 scratch=[pltpu.VMEM((tt, BR), F32)], sem=("arbitrary",))(z, z, d_b, ng, nb, w_masked, bias_full)


def _rope(rv, c_t, s1_t, s2_t):
    return rv * c_t + pltpu.roll(rv, 96, 1) * s1_t + pltpu.roll(rv, 32, 1) * s2_t


def _rope_bwd(gv, c_t, s1_t, s2_t):
    return gv * c_t - pltpu.roll(gv, 96, 1) * s1_t - pltpu.roll(gv, 32, 1) * s2_t


def _mla_prep(z, cb_q, cb_kv, cb_kr, qg, kvg, c_t, s1_t, s2_t, name):
    s = z.shape[0]
    tt = _tile(s, (TOK_NARROW,))

    def body(cq_ref, ckv_ref, kr_ref, qg_ref, kvg_ref, c_ref, s1_ref, s2_ref, qn_ref, kvn_ref, krp_ref):
        cq, ckv = cq_ref[...], ckv_ref[...]
        qn_ref[...] = (cq * _rstd(cq) * qg_ref[...]).astype(BF16)
        kvn_ref[...] = (ckv * _rstd(ckv) * kvg_ref[...]).astype(BF16)
        krp_ref[...] = _rope(kr_ref[...], c_ref[...], s1_ref[...], s2_ref[...]).astype(BF16)

    vec = _bs((1, BR), lambda i: (0, 0))
    rp = _bs((tt, 128), lambda i: (i, 0))
    row = _bs((tt, BR), lambda i: (i, 0))
    return _call(body, name=name, grid=(s // tt,),
                 in_specs=[_bs((tt, BR), lambda i: (i, cb_q)), _bs((tt, BR), lambda i: (i, cb_kv)),
                           _bs((tt, 128), lambda i: (i, cb_kr)), vec, vec, rp, rp, rp],
                 out_specs=[row, row, rp], out_shape=[_sds((s, BR), BF16), _sds((s, BR), BF16), _sds((s, 128), BF16)],
                 sem=("parallel",))(z, z, z, qg, kvg, c_t, s1_t, s2_t)


def _mla_prep_bwd(z, cb_q, cb_kv, d_qn, d_kvn, d_krp, qg, kvg, c_t, s1_t, s2_t, name):
    s = z.shape[0]
    tt = _tile(s, (TOK_NARROW,))

    def body(cq_ref, ckv_ref, dqn_ref, dkvn_ref, dkr_ref, qg_ref, kvg_ref, c_ref, s1_ref, s2_ref,
             dcq_ref, dckv_ref, dkro_ref, dqg_ref, dkvg_ref):
        @pl.when(pl.program_id(0) == 0)
        def _():
            dqg_ref[...] = jnp.zeros_like(dqg_ref)
            dkvg_ref[...] = jnp.zeros_like(dkvg_ref)

        cq, ckv, dqn, dkvn = cq_ref[...], ckv_ref[...], dqn_ref[...], dkvn_ref[...]
        rq, rkv = _rstd(cq), _rstd(ckv)
        dcq_ref[...] = _rms_bwd(cq, rq, dqn * qg_ref[...]).astype(BF16)
        dckv_ref[...] = _rms_bwd(ckv, rkv, dkvn * kvg_ref[...]).astype(BF16)
        dqg_ref[...] += _colsum(dqn * cq * rq)
        dkvg_ref[...] += _colsum(dkvn * ckv * rkv)
        dkro_ref[...] = _rope_bwd(dkr_ref[...], c_ref[...], s1_ref[...], s2_ref[...]).astype(BF16)

    vec = _bs((1, BR), lambda i: (0, 0))
    rp = _bs((tt, 128), lambda i: (i, 0))
    row = _bs((tt, BR), lambda i: (i, 0))
    return _call(body, name=name, grid=(s // tt,),
                 in_specs=[_bs((tt, BR), lambda i: (i, cb_q)), _bs((tt, BR), lambda i: (i, cb_kv)),
                           row, row, rp, vec, vec, rp, rp, rp],
                 out_specs=[row, row, rp, vec, vec],
                 out_shape=[_sds((s, BR), BF16), _sds((s, BR), BF16), _sds((s, 128), BF16),
                            _sds((1, BR), F32), _sds((1, BR), F32)],
                 sem=("arbitrary",))(z, z, d_qn, d_kvn, d_krp, qg, kvg, c_t, s1_t, s2_t)


def _q_rope(q, c_t, s1_t, s2_t, name):
    s, n = q.shape
    tt = _tile(s, (TOK_WIDE,))

    def body(q_ref, c_ref, s1_ref, s2_ref, out_ref):
        c_v, s1_v, s2_v = c_ref[...], s1_ref[...], s2_ref[...]
        for h in range(HEADS):
            b0 = h * 256
            out_ref[:, b0:b0 + 128] = q_ref[:, b0:b0 + 128].astype(BF16)
            out_ref[:, b0 + 128:b0 + 256] = _rope(q_ref[:, b0 + 128:b0 + 256], c_v, s1_v, s2_v).astype(BF16)

    rp = _bs((tt, 128), lambda i: (i, 0))
    row = _bs((tt, n), lambda i: (i, 0))
    return _call(body, name=name, grid=(s // tt,), in_specs=[row, rp, rp, rp], out_specs=row,
                 out_shape=_sds((s, n), BF16), sem=("parallel",))(q, c_t, s1_t, s2_t)


def _att_scores(q_ref, kn_ref, kr_ref, i, j, tq, tk):
    q = q_ref[...]
    sc = (_dot_nt(q[:, :128], kn_ref[...]) + _dot_nt(q[:, 128:], kr_ref[...])) * ATT_SCALE
    row = i * tq + lax.broadcasted_iota(jnp.int32, (tq, tk), 0)
    col = j * tk + lax.broadcasted_iota(jnp.int32, (tq, tk), 1)
    return jnp.where(col <= row, sc, -1e30)


def _flash_fwd(qb, kv, krp, name):
    s = qb.shape[0]
    t = _tile(s, (ATT_TILE,))
    nq = s // t

    def body(q_ref, kn_ref, v_ref, kr_ref, o_ref, lse_ref, m_s, l_s, acc):
        i, j = pl.program_id(1), pl.program_id(2)

        @pl.when(j == 0)
        def _():
            m_s[...] = jnp.full_like(m_s, -1e30)
            l_s[...] = jnp.zeros_like(l_s)
            acc[...] = jnp.zeros_like(acc)

        @pl.when(j <= i)
        def _():
            sc = _att_scores(q_ref, kn_ref, kr_ref, i, j, t, t)
            m_new = jnp.maximum(m_s[...], jnp.max(sc, axis=-1, keepdims=True))
            p = jnp.exp(sc - m_new)
            alpha = jnp.exp(m_s[...] - m_new)
            l_s[...] = alpha * l_s[...] + jnp.sum(p, axis=-1, keepdims=True)
            acc[...] = alpha * acc[...] + _dot(p.astype(BF16), v_ref[...])
            m_s[...] = m_new

        @pl.when(j == i)
        def _():
            o_ref[...] = (acc[...] / l_s[...]).astype(BF16)
            lse_ref[...] = m_s[...] + jnp.log(l_s[...])

    kmap = lambda off: (lambda h, i, j: (jnp.minimum(j, i), 2 * h + off))
    return _call(body, name=name, grid=(HEADS, nq, nq),
                 in_specs=[_bs((t, 256), lambda h, i, j: (i, h)), _bs((t, 128), kmap(0)), _bs((t, 128), kmap(1)),
                           _bs((t, 128), lambda h, i, j: (jnp.minimum(j, i), 0))],
                 out_specs=[_bs((t, 128), lambda h, i, j: (i, h)), _bs((None, t, 1), lambda h, i, j: (h, i, 0))],
                 out_shape=[_sds((s, HEADS * V_DIM), BF16), _sds((HEADS, s, 1), F32)],
                 scratch=[pltpu.VMEM((t, 1), F32), pltpu.VMEM((t, 1), F32), pltpu.VMEM((t, 128), F32)],
                 sem=("parallel", "parallel", "arbitrary"))(qb, kv, kv, krp)


def _flash_bwd_dq(qb, kv, krp, o, d_o, lse, c_t, s1_t, s2_t, name):
    s = qb.shape[0]
    t = _tile(s, (ATT_TILE,))
    nq = s // t

    def body(q_ref, kn_ref, v_ref, kr_ref, o_ref, do_ref, lse_ref, c_ref, s1_ref, s2_ref, dq_ref, dqn_s, dqr_s, dl_s):
        i, j = pl.program_id(1), pl.program_id(2)

        @pl.when(j == 0)
        def _():
            dqn_s[...] = jnp.zeros_like(dqn_s)
            dqr_s[...] = jnp.zeros_like(dqr_s)
            dl_s[...] = jnp.sum(do_ref[...].astype(F32) * o_ref[...].astype(F32), axis=-1, keepdims=True)

        @pl.when(j <= i)
        def _():
            p = jnp.exp(_att_scores(q_ref, kn_ref, kr_ref, i, j, t, t) - lse_ref[...])
            dp = _dot_nt(do_ref[...], v_ref[...])
            ds = (p * (dp - dl_s[...]) * ATT_SCALE).astype(BF16)
            dqn_s[...] += _dot(ds, kn_ref[...])
            dqr_s[...] += _dot(ds, kr_ref[...])

        @pl.when(j == i)
        def _():
            dq_ref[:, :128] = dqn_s[...].astype(BF16)
            dq_ref[:, 128:] = _rope_bwd(dqr_s[...], c_ref[...], s1_ref[...], s2_ref[...]).astype(BF16)

    kmap = lambda off: (lambda h, i, j: (jnp.minimum(j, i), 2 * h + off))
    qrow = _bs((t, 128), lambda h, i, j: (i, h))
    rp = _bs((t, 128), lambda h, i, j: (i, 0))
    return _call(body, name=name, grid=(HEADS, nq, nq),
                 in_specs=[_bs((t, 256), lambda h, i, j: (i, h)), _bs((t, 128), kmap(0)), _bs((t, 128), kmap(1)),
                           _bs((t, 128), lambda h, i, j: (jnp.minimum(j, i), 0)), qrow, qrow,
                           _bs((None, t, 1), lambda h, i, j: (h, i, 0)), rp, rp, rp],
                 out_specs=_bs((t, 256), lambda h, i, j: (i, h)), out_shape=_sds((s, HEADS * 256), BF16),
                 scratch=[pltpu.VMEM((t, 128), F32), pltpu.VMEM((t, 128), F32), pltpu.VMEM((t, 1), F32)],
                 sem=("parallel", "parallel", "arbitrary"))(qb, kv, kv, krp, o, d_o, lse, c_t, s1_t, s2_t)


def _flash_bwd_dkv(qb, kv, krp, o, d_o, lse, name):
    s = qb.shape[0]
    t = _tile(s, (ATT_TILE,))
    nq = s // t

    def body(q_ref, kn_ref, v_ref, kr_ref, o_ref, do_ref, lse_ref, dkv_ref, dkr_ref, dk_s, dv_s, dkr_s):
        j, h, i = pl.program_id(0), pl.program_id(1), pl.program_id(2)

        @pl.when(i == j)
        def _():
            dk_s[...] = jnp.zeros_like(dk_s)
            dv_s[...] = jnp.zeros_like(dv_s)

        @pl.when(jnp.logical_and(i == j, h == 0))
        def _():
            dkr_s[...] = jnp.zeros_like(dkr_s)

        @pl.when(i >= j)
        def _():
            q, do = q_ref[...], do_ref[...]
            p = jnp.exp(_att_scores(q_ref, kn_ref, kr_ref, i, j, t, t) - lse_ref[...])
            delta = jnp.sum(do.astype(F32) * o_ref[...].astype(F32), axis=-1, keepdims=True)
            dv_s[...] += _dot_tn(p.astype(BF16), do)
            ds = (p * (_dot_nt(do, v_ref[...]) - delta) * ATT_SCALE).astype(BF16)
            dk_s[...] += _dot_tn(ds, q[:, :128])
            dkr_s[...] += _dot_tn(ds, q[:, 128:])

        @pl.when(i == nq - 1)
        def _():
            dkv_ref[:, :128] = dk_s[...].astype(BF16)
            dkv_ref[:, 128:] = dv_s[...].astype(BF16)

        @pl.when(jnp.logical_and(i == nq - 1, h == HEADS - 1))
        def _():
            dkr_ref[...] = dkr_s[...]

    qi = lambda j, h, i: jnp.maximum(i, j)
    qrow = _bs((t, 128), lambda j, h, i: (qi(j, h, i), h))
    return _call(body, name=name, grid=(nq, HEADS, nq),
                 in_specs=[_bs((t, 256), lambda j, h, i: (qi(j, h, i), h)),
                           _bs((t, 128), lambda j, h, i: (j, 2 * h)), _bs((t, 128), lambda j, h, i: (j, 2 * h + 1)),
                           _bs((t, 128), lambda j, h, i: (j, 0)), qrow, qrow,
                           _bs((None, t, 1), lambda j, h, i: (h, qi(j, h, i), 0))],
                 out_specs=[_bs((t, 256), lambda j, h, i: (j, h)), _bs((t, 128), lambda j, h, i: (j, 0))],
                 out_shape=[_sds((s, HEADS * 256), BF16), _sds((s, 128), F32)],
                 scratch=[pltpu.VMEM((t, 128), F32), pltpu.VMEM((t, 128), F32), pltpu.VMEM((t, 128), F32)],
                 sem=("parallel", "arbitrary", "arbitrary"))(qb, kv, kv, krp, o, d_o, lse)


def _merge_specs(s, d, tt, tn):
    nb = d // tn
    br = lambda w: _bs((tt, w), lambda i, n: (i, 0))
    pw = lambda k: _bs((k, tn), lambda i, n: (0, n))
    gate = lambda g: _bs((tt, tn), lambda i, n: (i, g * nb + n))
    return [br(BR), br(BR), br(BR), br(HEADS * V_DIM), pw(BR), pw(BR), pw(BR), pw(HEADS * V_DIM)] + \
           [gate(g) for g in range(4)]


def _merge_fwd(z, branches, projs, name):
    s, d = z.shape[0], projs[0].shape[1]
    tt, tn = _tile(s, (TOK_WIDE,)), _tile(d, (512, 256, 128))

    def body(*refs):
        b_refs, p_refs, g_refs, out_ref = refs[0:4], refs[4:8], refs[8:12], refs[12]
        acc = None
        for b_ref, p_ref, g_ref in zip(b_refs, p_refs, g_refs):
            term = _sigmoid(g_ref[...]) * _dot(b_ref[...], p_ref[...])
            acc = term if acc is None else acc + term
        out_ref[...] = acc.astype(BF16)

    return _call(body, name=name, grid=(s // tt, d // tn), in_specs=_merge_specs(s, d, tt, tn),
                 out_specs=_bs((tt, tn), lambda i, n: (i, n)), out_shape=_sds((s, d), BF16),
                 sem=("parallel", "parallel"))(*branches, *projs, z, z, z, z)


def _merge_bwd(z, branches, projs, d_merged, name):
    s, d = z.shape[0], projs[0].shape[1]
    tt, tn = _tile(s, (TOK_WIDE,)), _tile(d, (512, 256, 128))

    def body(*refs):
        b_refs, p_refs, g_refs, dm_ref = refs[0:4], refs[4:8], refs[8:12], refs[12]
        dy_refs, dg_refs = refs[13:17], refs[17:21]
        dm = dm_ref[...]
        for b_ref, p_ref, g_ref, dy_ref, dg_ref in zip(b_refs, p_refs, g_refs, dy_refs, dg_refs):
            sg = _sigmoid(g_ref[...])
            dy_ref[...] = (dm * sg).astype(BF16)
            dg_ref[...] = (dm * _dot(b_ref[...], p_ref[...]) * sg * (1.0 - sg)).astype(BF16)

    tile = _bs((tt, tn), lambda i, n: (i, n))
    outs = _call(body, name=name, grid=(s // tt, d // tn), in_specs=_merge_specs(s, d, tt, tn) + [tile],
                 out_specs=[tile] * 8, out_shape=[_sds((s, d), BF16)] * 8,
                 sem=("parallel", "parallel"))(*branches, *projs, z, z, z, z, d_merged)
    return outs[:4], outs[4:]


def _ew_rows(rows, cols, n_arrays):
    budget = VMEM_LIMIT // (2 * 4 * cols * max(n_arrays, 1) * 2)
    cands = [t for t in (EW_ROWS, 256, 128, 64, 32, 16, 8) if t <= max(budget, 8)]
    return _tile(rows, cands)


def _add_pair(a, b, name):
    n, r, c = a.shape
    tr = _ew_rows(r, c, 3)

    def body(a_ref, b_ref, o_ref):
        o_ref[...] = (a_ref[...].astype(F32) + b_ref[...].astype(F32)).astype(BF16)

    blk = _bs((None, tr, c), lambda k, i: (k, i, 0))
    return _call(body, name=name, grid=(n, r // tr), in_specs=[blk, blk], out_specs=blk,
                 out_shape=_sds((n, r, c), BF16), sem=("parallel", "parallel"))(a, b)


def _sum_slots(buf, name):
    n, r, c = buf.shape
    tr = _ew_rows(r, c, n + 1)

    def body(b_ref, o_ref):
        acc = b_ref[0].astype(F32)
        for k in range(1, n):
            acc = acc + b_ref[k].astype(F32)
        o_ref[...] = acc

    return _call(body, name=name, grid=(r // tr,), in_specs=[_bs((n, tr, c), lambda i: (0, i, 0))],
                 out_specs=_bs((tr, c), lambda i: (i, 0)), out_shape=_sds((r, c), F32), sem=("parallel",))(buf)


def _adamw(w, g, m, v, name):
    shape = w.shape
    cols = shape[-1]
    rows = 1
    for dim in shape[:-1]:
        rows *= dim
    w2, g2, m2, v2 = (t.reshape(rows, cols) for t in (w, g, m, v))
    tr = _ew_rows(rows, -(-cols // 128) * 128, 7)

    def body(w_ref, g_ref, m_ref, v_ref, d_ref, nm_ref, nv_ref):
        gv = g_ref[...]
        nm = ADAM_B1 * m_ref[...] + (1.0 - ADAM_B1) * gv
        nv = ADAM_B2 * v_ref[...] + (1.0 - ADAM_B2) * jnp.square(gv)
        m_hat = nm / (1.0 - ADAM_B1 ** ADAM_STEP)
        v_hat = nv / (1.0 - ADAM_B2 ** ADAM_STEP)
        d_ref[...] = -ADAM_LR * (m_hat / (jnp.sqrt(v_hat) + ADAM_EPS) + ADAM_WD * w_ref[...])
        nm_ref[...] = nm
        nv_ref[...] = nv

    blk = _bs((tr, cols), lambda i: (i, 0))
    outs = _call(body, name=name, grid=(rows // tr,), in_specs=[blk] * 4, out_specs=[blk] * 3,
                 out_shape=[_sds((rows, cols), F32)] * 3, sem=("parallel",))(w2, g2, m2, v2)
    return [o.reshape(shape) for o in outs]


ANY = pl.BlockSpec(memory_space=pl.ANY)


def _place():
    return lax.axis_index("x"), lax.axis_index("y"), lax.axis_index("c")


def _comm_call(body, name, out_shape, scratch):
    return pl.pallas_call(body, name=name, out_shape=out_shape, in_specs=[ANY], out_specs=ANY,
                          scratch_shapes=scratch,
                          compiler_params=pltpu.CompilerParams(has_side_effects=True))


def _gather_weights(wp):
    _, r, c_ = wp.shape

    def body(src, dst, send_sems, recv_sems, fsend_sems, frecv_sems, loc_sems):
        x, y, c = _place()
        my = 2 * x + y
        sib = (x, y, 1 - c)
        chips = [(1 - x, y), (x, 1 - y), (1 - x, 1 - y)]
        loc = [pltpu.make_async_copy(src.at[c], dst.at[my, c], loc_sems.at[0]),
               pltpu.make_async_copy(src.at[1 - c], dst.at[my, 1 - c], loc_sems.at[1])]
        for cp in loc:
            cp.start()

        def ici(j, chip, slot):
            return pltpu.make_async_remote_copy(src_ref=src.at[c], dst_ref=dst.at[slot, c], send_sem=send_sems.at[j],
                                                recv_sem=recv_sems.at[j], device_id=(*chip, c), device_id_type=MESH)

        def d2d(j, slot, layer):
            return pltpu.make_async_remote_copy(src_ref=dst.at[slot, layer], dst_ref=dst.at[slot, layer],
                                                send_sem=fsend_sems.at[j], recv_sem=frecv_sems.at[j],
                                                device_id=sib, device_id_type=MESH)

        sends = [ici(j, chip, my) for j, chip in enumerate(chips)]
        for cp in sends:
            cp.start()
        passed = []
        for j, chip in enumerate(chips):
            slot = 2 * chip[0] + chip[1]
            ici(j, chip, slot).wait_recv()
            cp = d2d(j, slot, c)
            cp.start()
            passed.append(cp)
        for j, chip in enumerate(chips):
            d2d(j, 2 * chip[0] + chip[1], 1 - c).wait_recv()
        for cp in sends + passed:
            cp.wait_send()
        for cp in loc:
            cp.wait()

    return _comm_call(body, "gather_weights", _sds((4, 2, r, c_), wp.dtype),
                      [pltpu.SemaphoreType.DMA((3,)), pltpu.SemaphoreType.DMA((3,)), pltpu.SemaphoreType.DMA((3,)),
                       pltpu.SemaphoreType.DMA((3,)), pltpu.SemaphoreType.DMA((2,))])(wp)


def _swap_sibling(buf, pick_other_layer, name):
    out_shape = buf.shape[1:] if pick_other_layer else buf.shape

    def body(src, dst, send_sem, recv_sem):
        x, y, c = _place()
        cp = pltpu.make_async_remote_copy(src_ref=src.at[1 - c] if pick_other_layer else src, dst_ref=dst,
                                          send_sem=send_sem, recv_sem=recv_sem, device_id=(x, y, 1 - c),
                                          device_id_type=MESH)
        cp.start()
        cp.wait()

    return _comm_call(body, name, _sds(out_shape, buf.dtype),
                      [pltpu.SemaphoreType.DMA(()), pltpu.SemaphoreType.DMA(())])(buf)


def _scatter_to_chips(pb):
    def body(src, dst, send_sems, recv_sems, loc_sem):
        x, y, c = _place()
        my = 2 * x + y
        chips = [(1 - x, y), (x, 1 - y), (1 - x, 1 - y)]
        loc = pltpu.make_async_copy(src.at[my], dst.at[my], loc_sem)
        loc.start()

        def ici(j, chip, src_slot, dst_slot):
            return pltpu.make_async_remote_copy(src_ref=src.at[src_slot], dst_ref=dst.at[dst_slot],
                                                send_sem=send_sems.at[j], recv_sem=recv_sems.at[j],
                                                device_id=(*chip, c), device_id_type=MESH)

        sends = [ici(j, chip, 2 * chip[0] + chip[1], my) for j, chip in enumerate(chips)]
        for cp in sends:
            cp.start()
        for j, chip in enumerate(chips):
            slot = 2 * chip[0] + chip[1]
            ici(j, chip, slot, slot).wait_recv()
        for cp in sends:
            cp.wait_send()
        loc.wait()

    return _comm_call(body, "scatter_to_chips", _sds(pb.shape, pb.dtype),
                      [pltpu.SemaphoreType.DMA((3,)), pltpu.SemaphoreType.DMA((3,)), pltpu.SemaphoreType.DMA(())])(pb)


def _gather_all(buf, name):
    flips = [(fx, fy, fc) for fx in (0, 1) for fy in (0, 1) for fc in (0, 1) if (fx, fy, fc) != (0, 0, 0)]

    def body(src, dst, send_sems, recv_sems, loc_sem):
        x, y, c = _place()
        me = 4 * x + 2 * y + c
        loc = pltpu.make_async_copy(src, dst.at[me], loc_sem)
        loc.start()
        peers = [(1 - x if fx else x, 1 - y if fy else y, 1 - c if fc else c) for fx, fy, fc in flips]

        def cp(j, peer, slot):
            return pltpu.make_async_remote_copy(src_ref=src, dst_ref=dst.at[slot], send_sem=send_sems.at[j],
                                                recv_sem=recv_sems.at[j], device_id=peer, device_id_type=MESH)

        sends = [cp(j, peer, me) for j, peer in enumerate(peers)]
        for s_ in sends:
            s_.start()
        for j, peer in enumerate(peers):
            cp(j, peer, 4 * peer[0] + 2 * peer[1] + peer[2]).wait_recv()
        for s_ in sends:
            s_.wait_send()
        loc.wait()

    return _comm_call(body, name, _sds((8,) + buf.shape, buf.dtype),
                      [pltpu.SemaphoreType.DMA((7,)), pltpu.SemaphoreType.DMA((7,)), pltpu.SemaphoreType.DMA(())])(buf)


def _layout(shapes):
    out, r0 = [], 0
    for name, shape in shapes:
        n = 1
        for dim in shape:
            n *= dim
        nr = -(-n // PACK_COLS)
        nr = -(-nr // PACK_ROW_ALIGN) * PACK_ROW_ALIGN
        out.append((name, tuple(shape), r0, nr))
        r0 += nr
    return out, r0


def _pack(layout, get, dtype):
    parts = []
    for name, shape, _, nr in layout:
        flat = get(name).astype(dtype).reshape(-1)
        pad = nr * PACK_COLS - flat.shape[0]
        if pad:
            flat = jnp.pad(flat, (0, pad))
        parts.append(flat.reshape(nr, PACK_COLS))
    return jnp.concatenate(parts, axis=0)


def _unpack(layout, buf):
    out = {}
    for name, shape, r0, nr in layout:
        n = 1
        for dim in shape:
            n *= dim
        out[name] = buf[r0:r0 + nr].reshape(-1)[:n].reshape(shape)
    return out


def _rope_tables(positions):
    inv_freq = ROPE_THETA ** (-jnp.arange(0, QK_ROPE, 2, dtype=F32) / QK_ROPE)
    ang = positions.astype(F32)[:, None] * inv_freq
    cos, sin, zero = jnp.cos(ang), jnp.sin(ang), jnp.zeros_like(ang)
    c_t = jnp.concatenate([cos, cos, zero, zero], axis=1)
    s1_t = jnp.concatenate([-sin, zero, zero, zero], axis=1)
    s2_t = jnp.concatenate([zero, sin, zero, zero], axis=1)
    return c_t, s1_t, s2_t


def _layer_forward(l, x_in, h, w, sm, tabs, col):
    c_t, s1_t, s2_t = tabs
    tag = f"l{l}"
    z = _mm(h, w['w_in_x'], name=f"z_{tag}")
    b_pool = _pool_fwd(z, col['pool'], sm['pool_w16'], sm['pool_scale'], f"pool_fwd_{tag}")
    b_conv, ypre = _conv_fwd(z, col['conv'], sm['conv_w'], sm['conv_b'], sm['conv_norm_g'], sm['conv_norm_b'],
                             f"conv_fwd_{tag}")
    b_sgu = _sgu_fwd(z, col['sgu'], sm['sgu_norm_g'], sm['sgu_norm_b'], sm['sgu_w16'], sm['sgu_bias_full'],
                     f"sgu_fwd_{tag}")
    qn, kvn, krp = _mla_prep(z, col['q'], col['kv'], col['kr'], sm['q_norm_g'], sm['kv_norm_g'], c_t, s1_t, s2_t,
                             f"mla_prep_{tag}")
    q = _mm(qn, w['w_uq_x'], name=f"q_{tag}")
    qb = _q_rope(q, c_t, s1_t, s2_t, f"q_rope_{tag}")
    kv = _mm(kvn, w['w_ukv'], name=f"kv_{tag}", out_dtypes=(BF16,))
    o, lse = _flash_fwd(qb, kv, krp, f"flash_fwd_{tag}")
    branches = (b_pool, b_conv, b_sgu, o)
    projs = (w['pool_proj'], w['conv_proj'], w['sgu_proj'], w['attn_proj'])
    merged = _merge_fwd(z, branches, projs, f"merge_fwd_{tag}")
    o2 = _mm(merged, w['w_out'], name=f"o2_{tag}")
    x1, h2 = _resid_norm_fwd(x_in, o2, sm['post_mix_g'], sm['pre_mlp_g'], f"mix_out_{tag}")
    u, a = _mm(h2, w['w_up'], name=f"up_{tag}", out_dtypes=(F32, BF16),
               epilogue=lambda acc: (acc, jnp.square(jnp.maximum(acc, 0.0))))
    f = _mm(a, w['w_down'], name=f"down_{tag}")
    return dict(x_in=x_in, h=h, z=z, ypre=ypre, branches=branches, projs=projs, qn=qn, kvn=kvn, krp=krp, qb=qb, kv=kv,
                o=o, lse=lse, merged=merged, o2=o2, x1=x1, h2=h2, u=u, a=a, f=f)


def _layer_backward(l, sv, d_f, d_x2, w, sm, tabs, col, nz):
    c_t, s1_t, s2_t = tabs
    tag = f"l{l}"
    s = sv['z'].shape[0]
    gb, gs = {}, {}
    d_u = _mm(d_f, w['w_down'], name=f"d_u_{tag}", tb=True, out_dtypes=(BF16,), extras=(sv['u'],),
              epilogue=lambda acc, u: (acc * (2.0 * jnp.maximum(u, 0.0)),))
    gb['w_down'] = _mm(sv['a'], d_f, name=f"g_down_{tag}", ta=True)
    d_h2 = _mm(d_u, w['w_up'], name=f"d_h2_{tag}", tb=True)
    gb['w_up'] = _mm(sv['h2'], d_u, name=f"g_up_{tag}", ta=True)
    d_x1, d_o2, gs['post_mix_g'], gs['pre_mlp_g'] = _resid_norm_bwd(
        d_x2, d_h2, sv['x1'], sv['o2'], sm['post_mix_g'], sm['pre_mlp_g'], f"mix_out_bwd_{tag}")
    d_merged = _mm(d_o2, w['w_out'], name=f"d_merged_{tag}", tb=True)
    gb['w_out'] = _mm(sv['merged'], d_o2, name=f"g_out_{tag}", ta=True)
    d_ys, d_gates = _merge_bwd(sv['z'], sv['branches'], sv['projs'], d_merged, f"merge_bwd_{tag}")
    d_br = []
    for k, pname in enumerate(('pool_proj', 'conv_proj', 'sgu_proj', 'attn_proj')):
        last = pname == 'attn_proj'
        d_br.append(_mm(d_ys[k], w[pname], name=f"d_{pname}_in_{tag}", tb=True, out_dtypes=(BF16 if last else F32,)))
        gb[pname] = _mm(sv['branches'][k], d_ys[k], name=f"g_{pname}_{tag}", ta=True)
    dz_pool, gs['pool_w'], gs['pool_scale'] = _pool_bwd(sv['z'], col['pool'], d_br[0], sm['pool_w16'],
                                                        sm['pool_scale'], f"pool_bwd_{tag}")
    dz_ca, dz_cg, gs['conv_w'], gs['conv_b'], gs['conv_norm_g'], gs['conv_norm_b'] = _conv_bwd(
        sv['z'], col['conv'], d_br[1], sv['ypre'], sm['conv_w'], sm['conv_norm_g'], sm['conv_norm_b'],
        f"conv_bwd_{tag}")
    dz_su, dz_sv, g_sgu_w, g_sgu_bfull, gs['sgu_norm_g'], gs['sgu_norm_b'] = _sgu_bwd(
        sv['z'], col['sgu'], d_br[2], sm['sgu_norm_g'], sm['sgu_norm_b'], sm['sgu_w16'], sm['sgu_bias_full'],
        f"sgu_bwd_{tag}")
    gs['sgu_w'] = g_sgu_w * sm['tril']
    gs['sgu_b'] = jnp.sum(g_sgu_bfull, axis=-1)
    d_qb = _flash_bwd_dq(sv['qb'], sv['kv'], sv['krp'], sv['o'], d_br[3], sv['lse'], c_t, s1_t, s2_t,
                         f"flash_dq_{tag}")
    d_kv, d_krp = _flash_bwd_dkv(sv['qb'], sv['kv'], sv['krp'], sv['o'], d_br[3], sv['lse'], f"flash_dkv_{tag}")
    d_qn = _mm(d_qb, w['w_uq_x'], name=f"d_qn_{tag}", tb=True)
    gb['w_uq_x'] = _mm(sv['qn'], d_qb, name=f"g_uq_{tag}", ta=True)
    d_kvn = _mm(d_kv, w['w_ukv'], name=f"d_kvn_{tag}", tb=True)
    gb['w_ukv'] = _mm(sv['kvn'], d_kv, name=f"g_ukv_{tag}", ta=True)
    dz_q, dz_kv, dz_kr, gs['q_norm_g'], gs['kv_norm_g'] = _mla_prep_bwd(
        sv['z'], col['q'], col['kv'], d_qn, d_kvn, d_krp, sm['q_norm_g'], sm['kv_norm_g'], c_t, s1_t, s2_t,
        f"mla_prep_bwd_{tag}")
    used = 4 * d_gates[0].shape[1] + 7 * BR + 128
    dz = jnp.concatenate(list(d_gates) + [dz_pool, dz_ca, dz_cg, dz_su, dz_sv, dz_q, dz_kv, dz_kr,
                                          jnp.zeros((s, nz - used), BF16)], axis=1)
    d_h = _mm(dz, w['w_in_x'], name=f"d_h_{tag}", tb=True)
    gb['w_in_x'] = _mm(sv['h'], dz, name=f"g_in_{tag}", ta=True)
    return d_x1, d_h, gb, gs


def kernel(x, positions, pre_mix_g, w_in, pool_w, pool_scale, pool_proj, conv_w, conv_b, conv_norm_g, conv_norm_b, conv_proj, sgu_norm_g, sgu_norm_b, sgu_w, sgu_b, sgu_proj, q_norm_g, w_uq, kv_norm_g, w_ukv, attn_proj, w_out, post_mix_g, pre_mlp_g, w_up, w_down, post_mlp_g, loss_target, m_pre_mix_g, m_w_in, m_pool_w, m_pool_scale, m_pool_proj, m_conv_w, m_conv_b, m_conv_norm_g, m_conv_norm_b, m_conv_proj, m_sgu_norm_g, m_sgu_norm_b, m_sgu_w, m_sgu_b, m_sgu_proj, m_q_norm_g, m_w_uq, m_kv_norm_g, m_w_ukv, m_attn_proj, m_w_out, m_post_mix_g, m_pre_mlp_g, m_w_up, m_w_down, m_post_mlp_g, v_pre_mix_g, v_w_in, v_pool_w, v_pool_scale, v_pool_proj, v_conv_w, v_conv_b, v_conv_norm_g, v_conv_norm_b, v_conv_proj, v_sgu_norm_g, v_sgu_norm_b, v_sgu_w, v_sgu_b, v_sgu_proj, v_q_norm_g, v_w_uq, v_kv_norm_g, v_w_ukv, v_attn_proj, v_w_out, v_post_mix_g, v_pre_mlp_g, v_w_up, v_w_down, v_post_mlp_g):
    arg = dict(locals())
    depth = pre_mix_g.shape[0]
    s, d = x.shape[1], x.shape[2]
    n_branch_cols = 7 * BR + QK_ROPE
    n_in = 4 * d + n_branch_cols
    nz = 4 * d + -(-(7 * BR + 128) // 512) * 512
    g0 = 4 * d // BR
    col = dict(pool=g0, conv=g0 + 1, sgu=g0 + 3, q=g0 + 5, kv=g0 + 6, kr=(4 * d + 7 * BR) // 128)
    my_x, my_y, my_c = _place()
    my_chip = 2 * my_x + my_y
    x2 = x.reshape(s, d)
    target = loss_target.reshape(s, d)
    tabs = _rope_tables(positions.reshape(s))

    big_layout, _ = _layout([(n, arg[n].shape[1:]) for n in BIG_NAMES])
    wp = jnp.stack([_pack(big_layout, lambda n: arg[n][l], BF16) for l in range(depth)])
    wg = _gather_weights(wp)
    conv_layout, _ = _layout([('conv_w', conv_w.shape)])
    conv_all = _gather_all(_pack(conv_layout, lambda n: arg[n], F32), "gather_conv_w")
    conv_w_full = jnp.concatenate([_unpack(conv_layout, conv_all[4 * (k // 2) + 2 * (k % 2)])['conv_w']
                                   for k in range(4)], axis=-1)

    weights, smalls = [], []
    tril = jnp.tril(jnp.ones((GROUP, GROUP), F32))
    for l in range(depth):
        shards = [_unpack(big_layout, wg[k, l]) for k in range(4)]
        full = {n: jnp.concatenate([sh[n] for sh in shards], axis=1 if how == 'col' else 0) for n, how in BIG}
        w_l = {n: full[n] for n in BIG_NAMES if n not in ('w_in', 'w_uq')}
        win = full['w_in']
        w_l['w_in_x'] = jnp.concatenate([win[:, n_branch_cols:], win[:, :n_branch_cols],
                                         jnp.zeros((d, nz - n_in), BF16)], axis=1)
        wuq = full['w_uq'].reshape(-1, HEADS, QK_NOPE + QK_ROPE)
        w_l['w_uq_x'] = jnp.concatenate([wuq, jnp.zeros((wuq.shape[0], HEADS, 256 - QK_NOPE - QK_ROPE), BF16)],
                                        axis=2).reshape(-1, HEADS * 256)
        weights.append(w_l)
        sm = {n: arg[n][l][None, :] for n in ('pre_mix_g', 'pool_scale', 'conv_b', 'conv_norm_g', 'conv_norm_b',
                                               'sgu_norm_g', 'sgu_norm_b', 'q_norm_g', 'kv_norm_g', 'post_mix_g',
                                               'pre_mlp_g', 'post_mlp_g')}
        sm['pool_w16'] = pool_w[l].astype(BF16)
        sm['sgu_w16'] = (sgu_w[l] * tril).astype(BF16)
        sm['sgu_bias_full'] = jnp.broadcast_to(sgu_b[l][:, :, None], (BR // GROUP, GROUP, GROUP))
        sm['conv_w'] = conv_w_full[l]
        sm['tril'] = tril
        smalls.append(sm)

    saved = []
    x_cur = x2
    h = _rms_fwd(x2, smalls[0]['pre_mix_g'], "rms_in")
    for l in range(depth):
        sv = _layer_forward(l, x_cur, h, weights[l], smalls[l], tabs, col)
        saved.append(sv)
        if l + 1 < depth:
            x_cur, h = _resid_norm_fwd(sv['x1'], sv['f'], smalls[l]['post_mlp_g'], smalls[l + 1]['pre_mix_g'],
                                       f"mlp_out_l{l}")
    d_y, loss_local = _resid_norm_loss(saved[-1]['x1'], saved[-1]['f'], smalls[-1]['post_mlp_g'], target, "loss_head")

    grads_big = [None] * depth
    grads_small = [dict() for _ in range(depth)]
    d_x2 = d_y
    _, d_f, g_post, _ = _resid_norm_bwd(d_y, None, None, saved[-1]['f'], smalls[-1]['post_mlp_g'], None,
                                        f"mlp_out_bwd_l{depth - 1}")
    grads_small[-1]['post_mlp_g'] = g_post
    grad_x = None
    for l in reversed(range(depth)):
        d_x1, d_h, grads_big[l], gs = _layer_backward(l, saved[l], d_f, d_x2, weights[l], smalls[l], tabs, col, nz)
        grads_small[l].update(gs)
        if l > 0:
            d_x2, d_f, g_post, g_pre = _resid_norm_bwd(d_x1, d_h, saved[l]['x_in'], saved[l - 1]['f'],
                                                      smalls[l - 1]['post_mlp_g'], smalls[l]['pre_mix_g'],
                                                      f"mlp_out_bwd_l{l - 1}")
            grads_small[l - 1]['post_mlp_g'] = g_post
            grads_small[l]['pre_mix_g'] = g_pre
        else:
            grad_x, _, _, g_pre = _resid_norm_bwd(d_x1, d_h, saved[0]['x_in'], None, None, smalls[0]['pre_mix_g'],
                                                  "rms_in_bwd")
            grads_small[0]['pre_mix_g'] = g_pre

    def shard_of(l, name, how, k):
        g = grads_big[l]
        if name == 'w_in':
            gx = g['w_in_x']
            full = jnp.concatenate([gx[:, 4 * d:4 * d + n_branch_cols], gx[:, :4 * d]], axis=1)
        elif name == 'w_uq':
            full = g['w_uq_x'].reshape(-1, HEADS, 256)[:, :, :QK_NOPE + QK_ROPE].reshape(-1, HEADS * (QK_NOPE + QK_ROPE))
        else:
            full = g[name]
        if how == 'col':
            wd = full.shape[1] // 4
            return full[:, k * wd:(k + 1) * wd]
        ht = full.shape[0] // 4
        return full[k * ht:(k + 1) * ht]

    how_of = dict(BIG)
    gpk = jnp.stack([jnp.stack([_pack(big_layout, lambda n: shard_of(l, n, how_of[n], k), BF16) for k in range(4)])
                     for l in range(depth)])
    from_sib = _swap_sibling(gpk, True, "swap_grads")
    mine = lax.dynamic_index_in_dim(gpk, my_c, 0, keepdims=False)
    chip_partial = _add_pair(mine, from_sib, "add_sibling")
    from_chips = _scatter_to_chips(chip_partial)
    g_final = _sum_slots(from_chips, "sum_chips")
    g_other = _swap_sibling(g_final, False, "swap_final")
    g_layers = jnp.where(my_c == 0, jnp.stack([g_final, g_other]), jnp.stack([g_other, g_final]))
    g_big = [_unpack(big_layout, g_layers[l]) for l in range(depth)]

    small_shapes = []
    for n in SMALL_NAMES:
        shp = arg[n].shape[1:]
        if n == 'conv_w':
            shp = (shp[0], shp[1] * 4)
        small_shapes.append((n, shp))
    small_layout, _ = _layout([(f"{n}.{l}", shp) for l in range(depth) for n, shp in small_shapes])

    def small_get(key):
        n, l = key.rsplit('.', 1)
        return grads_small[int(l)][n]

    small_all = _gather_all(_pack(small_layout, small_get, F32), "gather_small_grads")
    g_small = _unpack(small_layout, _sum_slots(small_all, "sum_small_grads"))

    grad, delta, new_m, new_v = {}, {}, {}, {}
    for n in WEIGHT_NAMES:
        if n in BIG_NAMES:
            g = jnp.stack([g_big[l][n] for l in range(depth)])
        else:
            g = jnp.stack([g_small[f"{n}.{l}"] for l in range(depth)])
            if n == 'conv_w':
                wd = conv_w.shape[2]
                g = lax.dynamic_slice_in_dim(g, my_chip * wd, wd, axis=2)
        grad[n] = g
        delta[n], new_m[n], new_v[n] = _adamw(arg[n], g, arg['m_' + n], arg['v_' + n], f"adamw_{n}")

    loss = lax.psum(loss_local, ("x", "y", "c"))
    return (loss, grad_x.reshape(x.shape), *[grad[n] for n in WEIGHT_NAMES], *[delta[n] for n in WEIGHT_NAMES],
            *[new_m[n] for n in WEIGHT_NAMES], *[new_v[n] for n in WEIGHT_NAMES])
```

```python
import math

import jax
import jax.numpy as jnp
from jax import lax
from jax.experimental import pallas as pl
from jax.experimental.pallas import tpu as pltpu

F32 = jnp.float32
BF16 = jnp.bfloat16
MESH = pl.DeviceIdType.MESH

EPS = 1e-6
POOL_WINDOWS = (2, 4, 8, 16)
GROUP = 128
BR = 512
CONV_WIDTH = 31
HEADS = 8
QK_NOPE = 128
QK_ROPE = 64
V_DIM = 128
ROPE_THETA = 10000.0
ATT_SCALE = (QK_NOPE + QK_ROPE) ** -0.5
GELU_C = math.sqrt(2.0 / math.pi)
ADAM_LR, ADAM_B1, ADAM_B2, ADAM_EPS, ADAM_WD, ADAM_STEP = 0.001, 0.9, 0.999, 1e-08, 0.01, 10

VMEM_LIMIT = 48 * 1024 * 1024
PACK_COLS = 1024
PACK_ROW_ALIGN = 16
CONV_HALO = 32
POOL_HALO = 16

TOK_WIDE = 256
TOK_NARROW = 512
ATT_TILE = 512
MM_TILE_M = 1024
MM_TILE_N = 1024
MM_TILE_K = 2048
EW_BLOCK_BYTES = 1536 * 1024
N_CHIPS = 4

WEIGHT_NAMES = ['pre_mix_g', 'w_in', 'pool_w', 'pool_scale', 'pool_proj', 'conv_w', 'conv_b', 'conv_norm_g',
                'conv_norm_b', 'conv_proj', 'sgu_norm_g', 'sgu_norm_b', 'sgu_w', 'sgu_b', 'sgu_proj', 'q_norm_g',
                'w_uq', 'kv_norm_g', 'w_ukv', 'attn_proj', 'w_out', 'post_mix_g', 'pre_mlp_g', 'w_up', 'w_down',
                'post_mlp_g']
BIG = [('w_in', 'col'), ('pool_proj', 'col'), ('conv_proj', 'col'), ('sgu_proj', 'col'), ('w_uq', 'col'),
       ('w_ukv', 'col'), ('attn_proj', 'col'), ('w_out', 'row'), ('w_up', 'col'), ('w_down', 'row')]
BIG_NAMES = [n for n, _ in BIG]
SMALL_NAMES = [n for n in WEIGHT_NAMES if n not in BIG_NAMES]


def _bs(shape, index_map):
    return pl.BlockSpec(shape, index_map)


def _sds(shape, dtype):
    return jax.ShapeDtypeStruct(shape, dtype)


def _tile(n, candidates):
    for t in candidates:
        if n % t == 0:
            return t
    return n


def _div_tile(n, target, align=8):
    t = min(n, target) // align * align
    while t >= align:
        if n % t == 0:
            return t
        t -= align
    return n


def _call(body, *, name, grid, in_specs, out_specs, out_shape, scratch=(), sem=None):
    return pl.pallas_call(
        body, name=name, grid=grid, in_specs=in_specs, out_specs=out_specs, out_shape=out_shape,
        scratch_shapes=list(scratch),
        compiler_params=pltpu.CompilerParams(dimension_semantics=sem, vmem_limit_bytes=VMEM_LIMIT))


def _sigmoid(v):
    return 1.0 / (1.0 + jnp.exp(-v))


def _gelu(v):
    return 0.5 * v * (1.0 + jnp.tanh(GELU_C * (v + 0.044715 * v * v * v)))


def _gelu_grad(v):
    t = jnp.tanh(GELU_C * (v + 0.044715 * v * v * v))
    return 0.5 * (1.0 + t) + 0.5 * v * (1.0 - t * t) * GELU_C * (1.0 + 3.0 * 0.044715 * v * v)


def _rstd(v):
    return lax.rsqrt(jnp.mean(v * v, axis=-1, keepdims=True) + EPS)


def _rms_bwd(v, r, t):
    return r * t - v * (r * r * r) * jnp.mean(v * t, axis=-1, keepdims=True)


def _ln_stats(v):
    mu = jnp.mean(v, axis=-1, keepdims=True)
    d = v - mu
    r = lax.rsqrt(jnp.mean(d * d, axis=-1, keepdims=True) + EPS)
    return d * r, r


def _ln_bwd(xh, r, dxh):
    return r * (dxh - jnp.mean(dxh, axis=-1, keepdims=True) - xh * jnp.mean(dxh * xh, axis=-1, keepdims=True))


def _colsum(v):
    return jnp.sum(v, axis=0, keepdims=True)


def _dot(a, b):
    return jnp.dot(a, b, preferred_element_type=F32)


def _dot_nt(a, b):
    return lax.dot_general(a, b, (((1,), (1,)), ((), ())), preferred_element_type=F32)


def _dot_tn(a, b):
    return lax.dot_general(a, b, (((0,), (0,)), ((), ())), preferred_element_type=F32)


def _mm(a, b, *, name, ta=False, tb=False, b_split=False, out_split=False, out_dtypes=(F32,), epilogue=None,
        extras=()):
    m = a.shape[1] if ta else a.shape[0]
    k = a.shape[0] if ta else a.shape[1]
    shard = b.shape[2] if b_split else None
    b_rows, b_cols = (b.shape[1], N_CHIPS * shard) if b_split else b.shape
    n = b_rows if tb else b_cols
    assert k == (b_cols if tb else b_rows)
    assert not (out_split and extras)
    tm = _div_tile(m, MM_TILE_M, 128)
    tn = _div_tile(n // N_CHIPS if (out_split or (b_split and not tb)) else n, MM_TILE_N, 128)
    tk = _div_tile(shard if (b_split and tb) else k, MM_TILE_K, 128)
    nk = k // tk
    n_extra, n_out = len(extras), len(out_dtypes)
    dims = (((0 if ta else 1,), (1 if tb else 0,)), ((), ()))

    def body(a_ref, b_ref, *rest):
        extra_refs, out_refs = rest[:n_extra], rest[n_extra:n_extra + n_out]
        part = lax.dot_general(a_ref[...].astype(BF16), b_ref[...].astype(BF16), dims, preferred_element_type=F32)

        def finish(res):
            res = (res,) if epilogue is None else epilogue(res, *[e[...] for e in extra_refs])
            for o, r in zip(out_refs, res):
                o[...] = r.astype(o.dtype)

        if nk == 1:
            finish(part)
        else:
            acc, kk = rest[-1], pl.program_id(2)

            @pl.when(kk == 0)
            def _():
                acc[...] = part

            @pl.when(jnp.logical_and(kk > 0, kk < nk - 1))
            def _():
                acc[...] += part

            @pl.when(kk == nk - 1)
            def _():
                finish(acc[...] + part)

    a_spec = _bs((tk, tm), lambda i, j, kk: (kk, i)) if ta else _bs((tm, tk), lambda i, j, kk: (i, kk))
    if not b_split:
        b_spec = _bs((tn, tk), lambda i, j, kk: (j, kk)) if tb else _bs((tk, tn), lambda i, j, kk: (kk, j))
    elif tb:
        kpb = shard // tk
        b_spec = _bs((None, tn, tk), lambda i, j, kk: (kk // kpb, j, kk % kpb))
    else:
        npb = shard // tn
        b_spec = _bs((None, tk, tn), lambda i, j, kk: (j // npb, kk, j % npb))
    e_spec = _bs((tm, tn), lambda i, j, kk: (i, j))
    if out_split:
        npo = (n // N_CHIPS) // tn
        o_spec = _bs((None, tm, tn), lambda i, j, kk: (j // npo, i, j % npo))
        out_shape = [_sds((N_CHIPS, m, n // N_CHIPS), dt) for dt in out_dtypes]
    else:
        o_spec, out_shape = e_spec, [_sds((m, n), dt) for dt in out_dtypes]
    outs = _call(body, name=name, grid=(m // tm, n // tn, nk), in_specs=[a_spec, b_spec] + [e_spec] * n_extra,
                 out_specs=[o_spec] * n_out, out_shape=out_shape,
                 scratch=[pltpu.VMEM((tm, tn), F32)] if nk > 1 else [],
                 sem=("parallel", "parallel", "arbitrary"))(a, b, *extras)
    return outs[0] if n_out == 1 else outs


def _rms_fwd(x, g, name):
    s, d = x.shape
    tt = _tile(s, (TOK_WIDE,))

    def body(x_ref, g_ref, h_ref):
        v = x_ref[...]
        h_ref[...] = (v * _rstd(v) * g_ref[...]).astype(BF16)

    row = _bs((tt, d), lambda i: (i, 0))
    return _call(body, name=name, grid=(s // tt,), in_specs=[row, _bs((1, d), lambda i: (0, 0))],
                 out_specs=row, out_shape=_sds((s, d), BF16), sem=("parallel",))(x, g)


def _resid_norm_fwd(xres, y, g_post, g_next, name):
    s, d = xres.shape
    tt = _tile(s, (TOK_WIDE,))

    def body(xr_ref, y_ref, gp_ref, gn_ref, xn_ref, h_ref):
        yv = y_ref[...]
        xn = xr_ref[...] + yv * _rstd(yv) * gp_ref[...]
        xn_ref[...] = xn
        h_ref[...] = (xn * _rstd(xn) * gn_ref[...]).astype(BF16)

    row, vec = _bs((tt, d), lambda i: (i, 0)), _bs((1, d), lambda i: (0, 0))
    return _call(body, name=name, grid=(s // tt,), in_specs=[row, row, vec, vec], out_specs=[row, row],
                 out_shape=[_sds((s, d), F32), _sds((s, d), BF16)], sem=("parallel",))(xres, y, g_post, g_next)


def _resid_norm_loss(xres, y, g_post, target, name):
    s, d = xres.shape
    tt = _tile(s, (TOK_WIDE,))

    def body(xr_ref, y_ref, gp_ref, t_ref, dy_ref, loss_ref):
        @pl.when(pl.program_id(0) == 0)
        def _():
            loss_ref[...] = jnp.zeros_like(loss_ref)

        yv = y_ref[...]
        err = xr_ref[...] + yv * _rstd(yv) * gp_ref[...] - t_ref[...]
        dy_ref[...] = err * (1.0 / d)
        loss_ref[...] += 0.5 * jnp.sum(jnp.mean(err * err, axis=-1, keepdims=True))

    row, vec = _bs((tt, d), lambda i: (i, 0)), _bs((1, d), lambda i: (0, 0))
    dy, loss = _call(body, name=name, grid=(s // tt,), in_specs=[row, row, vec, row],
                     out_specs=[row, _bs((8, 128), lambda i: (0, 0))],
                     out_shape=[_sds((s, d), F32), _sds((8, 128), F32)], sem=("arbitrary",))(xres, y, g_post, target)
    return dy, loss[0, 0]


def _resid_norm_bwd(d_out, d_h, x_new, y, g_post, g_next, name):
    s, d = d_out.shape
    tt = _tile(s, (TOK_WIDE,))
    has_next, has_y = d_h is not None, y is not None

    def body(*refs):
        it = iter(refs)
        do_ref = next(it)
        if has_next:
            dh_ref, xn_ref, gn_ref = next(it), next(it), next(it)
        if has_y:
            y_ref, gp_ref = next(it), next(it)
        if has_next:
            dx_ref = next(it)
        if has_y:
            dy_ref, dgp_ref = next(it), next(it)
        if has_next:
            dgn_ref = next(it)

        first = pl.program_id(0) == 0
        dx = do_ref[...]
        if has_next:
            xn, dh = xn_ref[...], dh_ref[...]
            r = _rstd(xn)
            dx = dx + _rms_bwd(xn, r, dh * gn_ref[...])
            dx_ref[...] = dx

            @pl.when(first)
            def _():
                dgn_ref[...] = jnp.zeros_like(dgn_ref)

            dgn_ref[...] += _colsum(dh * xn * r)
        if has_y:
            yv = y_ref[...]
            ry = _rstd(yv)
            dy_ref[...] = _rms_bwd(yv, ry, dx * gp_ref[...]).astype(BF16)

            @pl.when(first)
            def _():
                dgp_ref[...] = jnp.zeros_like(dgp_ref)

            dgp_ref[...] += _colsum(dx * yv * ry)

    row, vec = _bs((tt, d), lambda i: (i, 0)), _bs((1, d), lambda i: (0, 0))
    args, in_specs, out_specs, out_shape = [d_out], [row], [], []
    if has_next:
        args += [d_h, x_new, g_next]
        in_specs += [row, row, vec]
    if has_y:
        args += [y, g_post]
        in_specs += [row, vec]
    if has_next:
        out_specs.append(row)
        out_shape.append(_sds((s, d), F32))
    if has_y:
        out_specs += [row, vec]
        out_shape += [_sds((s, d), BF16), _sds((1, d), F32)]
    if has_next:
        out_specs.append(vec)
        out_shape.append(_sds((1, d), F32))
    outs = list(_call(body, name=name, grid=(s // tt,), in_specs=in_specs, out_specs=out_specs, out_shape=out_shape,
                      sem=("arbitrary",))(*args))
    d_x = outs.pop(0) if has_next else None
    d_y, d_gp = (outs.pop(0), outs.pop(0)) if has_y else (None, None)
    d_gn = outs.pop(0) if has_next else None
    return d_x, d_y, d_gp, d_gn


def _pool_counts(t0, tt, w):
    t = t0 + lax.broadcasted_iota(jnp.int32, (tt, 1), 0)
    return jnp.minimum(t + 1, w).astype(F32)


def _pool_pooled(ext, a, t0, tt, g, w):
    cols = pl.ds(g * GROUP, GROUP)
    sm = ext[pl.ds(POOL_HALO, tt), cols]
    for j in range(1, w):
        sm = sm + ext[pl.ds(POOL_HALO - j, tt), cols]
    return sm / _pool_counts(t0, tt, w) - a[:, g * GROUP:(g + 1) * GROUP]


def _pool_fwd(z, cb, pool_w, pool_scale, name):
    s = z.shape[0]
    tt = _tile(s, (TOK_NARROW,))
    hb = tt // POOL_HALO

    def body(zc_ref, zp_ref, pw_ref, sc_ref, out_ref, ext):
        i = pl.program_id(0)
        a = zc_ref[...]
        ext[pl.ds(0, POOL_HALO), :] = jnp.where(i > 0, zp_ref[...], 0.0)
        ext[pl.ds(POOL_HALO, tt), :] = a
        for g, w in enumerate(POOL_WINDOWS):
            pooled = _pool_pooled(ext, a, i * tt, tt, g, w).astype(BF16)
            mixed = _dot(pooled, pw_ref[g])
            out_ref[:, g * GROUP:(g + 1) * GROUP] = (mixed * sc_ref[:, g * GROUP:(g + 1) * GROUP]).astype(BF16)

    return _call(body, name=name, grid=(s // tt,),
                 in_specs=[_bs((tt, BR), lambda i: (i, cb)),
                           _bs((POOL_HALO, BR), lambda i: (jnp.maximum(i * hb - 1, 0), cb)),
                           _bs((len(POOL_WINDOWS), GROUP, GROUP), lambda i: (0, 0, 0)),
                           _bs((1, BR), lambda i: (0, 0))],
                 out_specs=_bs((tt, BR), lambda i: (i, 0)), out_shape=_sds((s, BR), BF16),
                 scratch=[pltpu.VMEM((tt + POOL_HALO, BR), F32)], sem=("parallel",))(z, z, pool_w, pool_scale)


def _pool_bwd(z, cb, d_b, pool_w, pool_scale, name):
    s = z.shape[0]
    tt = _tile(s, (TOK_NARROW,))
    nt, hb, ng = s // tt, tt // POOL_HALO, len(POOL_WINDOWS)

    def body(zc_ref, zp_ref, db_ref, pw_ref, sc_ref, dz_ref, dpw_ref, dsc_ref, ext, ext_e, carry):
        step = pl.program_id(0)
        i = nt - 1 - step

        @pl.when(step == 0)
        def _():
            dpw_ref[...] = jnp.zeros_like(dpw_ref)
            dsc_ref[...] = jnp.zeros_like(dsc_ref)
            carry[...] = jnp.zeros_like(carry)

        a = zc_ref[...]
        ext[pl.ds(0, POOL_HALO), :] = jnp.where(i > 0, zp_ref[...], 0.0)
        ext[pl.ds(POOL_HALO, tt), :] = a
        ext_e[pl.ds(tt, POOL_HALO), :] = carry[...]
        db = db_ref[...]
        for g, w in enumerate(POOL_WINDOWS):
            c0, c1 = g * GROUP, (g + 1) * GROUP
            pooled = _pool_pooled(ext, a, i * tt, tt, g, w).astype(BF16)
            mixed = _dot(pooled, pw_ref[g])
            dsc_ref[:, c0:c1] += _colsum(db[:, c0:c1] * mixed)
            dmixed = (db[:, c0:c1] * sc_ref[:, c0:c1]).astype(BF16)
            dpw_ref[g] += _dot_tn(pooled, dmixed)
            dpooled = _dot_nt(dmixed, pw_ref[g])
            ext_e[pl.ds(0, tt), pl.ds(c0, GROUP)] = dpooled / _pool_counts(i * tt, tt, w)
            acc = -dpooled
            for j in range(w):
                acc = acc + ext_e[pl.ds(j, tt), pl.ds(c0, GROUP)]
            dz_ref[:, c0:c1] = acc.astype(BF16)
        carry[...] = ext_e[pl.ds(0, POOL_HALO), :]

    rev = lambda st: nt - 1 - st
    dz, dpw, dsc = _call(
        body, name=name, grid=(nt,),
        in_specs=[_bs((tt, BR), lambda st: (rev(st), cb)),
                  _bs((POOL_HALO, BR), lambda st: (jnp.maximum(rev(st) * hb - 1, 0), cb)),
                  _bs((tt, BR), lambda st: (rev(st), 0)),
                  _bs((ng, GROUP, GROUP), lambda st: (0, 0, 0)),
                  _bs((1, BR), lambda st: (0, 0))],
        out_specs=[_bs((tt, BR), lambda st: (rev(st), 0)), _bs((ng, GROUP, GROUP), lambda st: (0, 0, 0)),
                   _bs((1, BR), lambda st: (0, 0))],
        out_shape=[_sds((s, BR), BF16), _sds((ng, GROUP, GROUP), F32), _sds((1, BR), F32)],
        scratch=[pltpu.VMEM((tt + POOL_HALO, BR), F32), pltpu.VMEM((tt + POOL_HALO, BR), F32),
                 pltpu.VMEM((POOL_HALO, BR), F32)],
        sem=("arbitrary",))(z, z, d_b, pool_w, pool_scale)
    return dz, dpw, dsc


def _conv_fwd(z, cb, conv_w, conv_b, ng, nb, name):
    s = z.shape[0]
    tt = _tile(s, (TOK_NARROW,))
    hb = tt // CONV_HALO

    def body(a_ref, g_ref, ap_ref, gp_ref, w_ref, b_ref, ng_ref, nb_ref, out_ref, ypre_ref, ext):
        i = pl.program_id(0)
        ext[pl.ds(0, CONV_HALO), :] = jnp.where(i > 0, ap_ref[...] * _sigmoid(gp_ref[...]), 0.0)
        ext[pl.ds(CONV_HALO, tt), :] = a_ref[...] * _sigmoid(g_ref[...])
        acc = jnp.zeros((tt, BR), F32)
        for k in range(CONV_WIDTH):
            acc = acc + w_ref[pl.ds(k, 1), :] * ext[pl.ds(CONV_HALO - (CONV_WIDTH - 1) + k, tt), :]
        ypre = acc + b_ref[...]
        ypre_ref[...] = ypre
        xh, _ = _ln_stats(ypre)
        yl = xh * ng_ref[...] + nb_ref[...]
        out_ref[...] = (yl * _sigmoid(yl)).astype(BF16)

    cur = lambda c: _bs((tt, BR), lambda i: (i, c))
    prev = lambda c: _bs((CONV_HALO, BR), lambda i: (jnp.maximum(i * hb - 1, 0), c))
    vec = _bs((1, BR), lambda i: (0, 0))
    row = _bs((tt, BR), lambda i: (i, 0))
    return _call(body, name=name, grid=(s // tt,),
                 in_specs=[cur(cb), cur(cb + 1), prev(cb), prev(cb + 1),
                           _bs((CONV_WIDTH, BR), lambda i: (0, 0)), vec, vec, vec],
                 out_specs=[row, row], out_shape=[_sds((s, BR), BF16), _sds((s, BR), F32)],
                 scratch=[pltpu.VMEM((tt + CONV_HALO, BR), F32)], sem=("parallel",))(
        z, z, z, z, conv_w, conv_b, ng, nb)


def _conv_bwd(z, cb, d_b, ypre, conv_w, ng, nb, name):
    s = z.shape[0]
    tt = _tile(s, (TOK_NARROW,))
    nt, hb = s // tt, tt // CONV_HALO
    lead = CONV_HALO - (CONV_WIDTH - 1)

    def body(a_ref, g_ref, ap_ref, gp_ref, db_ref, yp_ref, w_ref, ng_ref, nb_ref,
             da_ref, dg_ref, dw_ref, dcb_ref, dng_ref, dnb_ref, ext, ext_d, carry):
        step = pl.program_id(0)
        i = nt - 1 - step

        @pl.when(step == 0)
        def _():
            dw_ref[...] = jnp.zeros_like(dw_ref)
            dcb_ref[...] = jnp.zeros_like(dcb_ref)
            dng_ref[...] = jnp.zeros_like(dng_ref)
            dnb_ref[...] = jnp.zeros_like(dnb_ref)
            carry[...] = jnp.zeros_like(carry)

        xh, r = _ln_stats(yp_ref[...])
        yl = xh * ng_ref[...] + nb_ref[...]
        sg = _sigmoid(yl)
        dyl = db_ref[...] * (sg * (1.0 + yl * (1.0 - sg)))
        dng_ref[...] += _colsum(dyl * xh)
        dnb_ref[...] += _colsum(dyl)
        dypre = _ln_bwd(xh, r, dyl * ng_ref[...])
        dcb_ref[...] += _colsum(dypre)

        a, gate = a_ref[...], g_ref[...]
        sgate = _sigmoid(gate)
        ext[pl.ds(0, CONV_HALO), :] = jnp.where(i > 0, ap_ref[...] * _sigmoid(gp_ref[...]), 0.0)
        ext[pl.ds(CONV_HALO, tt), :] = a * sgate
        ext_d[pl.ds(0, tt), :] = dypre
        ext_d[pl.ds(tt, CONV_HALO), :] = carry[...]
        dglu = jnp.zeros((tt, BR), F32)
        for k in range(CONV_WIDTH):
            dw_ref[pl.ds(k, 1), :] += _colsum(dypre * ext[pl.ds(lead + k, tt), :])
            dglu = dglu + w_ref[pl.ds(k, 1), :] * ext_d[pl.ds(CONV_WIDTH - 1 - k, tt), :]
        carry[...] = ext_d[pl.ds(0, CONV_HALO), :]
        da_ref[...] = (dglu * sgate).astype(BF16)
        dg_ref[...] = (dglu * a * sgate * (1.0 - sgate)).astype(BF16)

    rev = lambda st: nt - 1 - st
    cur = lambda c: _bs((tt, BR), lambda st: (rev(st), c))
    prev = lambda c: _bs((CONV_HALO, BR), lambda st: (jnp.maximum(rev(st) * hb - 1, 0), c))
    vec = _bs((1, BR), lambda st: (0, 0))
    row = _bs((tt, BR), lambda st: (rev(st), 0))
    wsp = _bs((CONV_WIDTH, BR), lambda st: (0, 0))
    return _call(body, name=name, grid=(nt,),
                 in_specs=[cur(cb), cur(cb + 1), prev(cb), prev(cb + 1), row, row, wsp, vec, vec],
                 out_specs=[row, row, wsp, vec, vec, vec],
                 out_shape=[_sds((s, BR), BF16), _sds((s, BR), BF16), _sds((CONV_WIDTH, BR), F32),
                            _sds((1, BR), F32), _sds((1, BR), F32), _sds((1, BR), F32)],
                 scratch=[pltpu.VMEM((tt + CONV_HALO, BR), F32), pltpu.VMEM((tt + CONV_HALO, BR), F32),
                          pltpu.VMEM((CONV_HALO, BR), F32)],
                 sem=("arbitrary",))(z, z, z, z, d_b, ypre, conv_w, ng, nb)


def _sgu_fwd(z, cb, ng, nb, w_masked, bias_full, name):
    s = z.shape[0]
    tt = _tile(s, (TOK_NARROW,))
    ngr = BR // GROUP

    def body(u_ref, v_ref, ng_ref, nb_ref, w_ref, bb_ref, out_ref):
        ua = _gelu(u_ref[...])
        xh, _ = _ln_stats(_gelu(v_ref[...]))
        vn = (xh * ng_ref[...] + nb_ref[...]).astype(BF16)
        for n in range(tt // GROUP):
            for g in range(ngr):
                r0, c0 = n * GROUP, g * GROUP
                sp = _dot(w_ref[g], vn[r0:r0 + GROUP, c0:c0 + GROUP]) + bb_ref[g]
                out_ref[r0:r0 + GROUP, c0:c0 + GROUP] = (ua[r0:r0 + GROUP, c0:c0 + GROUP] * sp).astype(BF16)

    vec = _bs((1, BR), lambda i: (0, 0))
    sq = _bs((ngr, GROUP, GROUP), lambda i: (0, 0, 0))
    return _call(body, name=name, grid=(s // tt,),
                 in_specs=[_bs((tt, BR), lambda i: (i, cb)), _bs((tt, BR), lambda i: (i, cb + 1)), vec, vec, sq, sq],
                 out_specs=_bs((tt, BR), lambda i: (i, 0)), out_shape=_sds((s, BR), BF16),
                 sem=("parallel",))(z, z, ng, nb, w_masked, bias_full)


def _sgu_bwd(z, cb, d_b, ng, nb, w_masked, bias_full, name):
    s = z.shape[0]
    tt = _tile(s, (TOK_NARROW,))
    ngr = BR // GROUP

    def body(u_ref, v_ref, db_ref, ng_ref, nb_ref, w_ref, bb_ref,
             du_ref, dv_ref, dw_ref, dbias_ref, dng_ref, dnb_ref, dvn_s):
        @pl.when(pl.program_id(0) == 0)
        def _():
            dw_ref[...] = jnp.zeros_like(dw_ref)
            dbias_ref[...] = jnp.zeros_like(dbias_ref)
            dng_ref[...] = jnp.zeros_like(dng_ref)
            dnb_ref[...] = jnp.zeros_like(dnb_ref)

        u, v, db = u_ref[...], v_ref[...], db_ref[...]
        ua = _gelu(u)
        xh, r = _ln_stats(_gelu(v))
        vn = (xh * ng_ref[...] + nb_ref[...]).astype(BF16)
        for n in range(tt // GROUP):
            for g in range(ngr):
                rows, cols = slice(n * GROUP, (n + 1) * GROUP), slice(g * GROUP, (g + 1) * GROUP)
                vn_c = vn[rows, cols]
                sp = _dot(w_ref[g], vn_c) + bb_ref[g]
                du_ref[rows, cols] = (db[rows, cols] * sp * _gelu_grad(u[rows, cols])).astype(BF16)
                dsp = db[rows, cols] * ua[rows, cols]
                dbias_ref[g] += dsp
                dsp16 = dsp.astype(BF16)
                dw_ref[g] += _dot_nt(dsp16, vn_c)
                dvn_s[rows, cols] = _dot_tn(w_ref[g], dsp16)
        dvn = dvn_s[...]
        dng_ref[...] += _colsum(dvn * xh)
        dnb_ref[...] += _colsum(dvn)
        dv_ref[...] = (_ln_bwd(xh, r, dvn * ng_ref[...]) * _gelu_grad(v)).astype(BF16)

    vec = _bs((1, BR), lambda i: (0, 0))
    sq = _bs((ngr, GROUP, GROUP), lambda i: (0, 0, 0))
    row = _bs((tt, BR), lambda i: (i, 0))
    return _call(body, name=name, grid=(s // tt,),
                 in_specs=[_bs((tt, BR), lambda i: (i, cb)), _bs((tt, BR), lambda i: (i, cb + 1)), row, vec, vec, sq, sq],
                 out_specs=[row, row, sq, sq, vec, vec],
                 out_shape=[_sds((s, BR), BF16), _sds((s, BR), BF16), _sds((ngr, GROUP, GROUP), F32),
                            _sds((ngr, GROUP, GROUP), F32), _sds((1, BR), F32), _sds((1, BR), F32)],
                 scratch=[pltpu.VMEM((tt, BR), F32)], sem=("arbitrary",))(z, z, d_b, ng, nb, w_masked, bias_full)


def _rope(rv, c_t, s1_t, s2_t):
    return rv * c_t + pltpu.roll(rv, 96, 1) * s1_t + pltpu.roll(rv, 32, 1) * s2_t


def _rope_bwd(gv, c_t, s1_t, s2_t):
    return gv * c_t - pltpu.roll(gv, 96, 1) * s1_t - pltpu.roll(gv, 32, 1) * s2_t


def _mla_prep(z, cb_q, cb_kv, cb_kr, qg, kvg, c_t, s1_t, s2_t, name):
    s = z.shape[0]
    tt = _tile(s, (TOK_NARROW,))

    def body(cq_ref, ckv_ref, kr_ref, qg_ref, kvg_ref, c_ref, s1_ref, s2_ref, qn_ref, kvn_ref, krp_ref):
        cq, ckv = cq_ref[...], ckv_ref[...]
        qn_ref[...] = (cq * _rstd(cq) * qg_ref[...]).astype(BF16)
        kvn_ref[...] = (ckv * _rstd(ckv) * kvg_ref[...]).astype(BF16)
        krp_ref[...] = _rope(kr_ref[...], c_ref[...], s1_ref[...], s2_ref[...]).astype(BF16)

    vec = _bs((1, BR), lambda i: (0, 0))
    rp = _bs((tt, 128), lambda i: (i, 0))
    row = _bs((tt, BR), lambda i: (i, 0))
    return _call(body, name=name, grid=(s // tt,),
                 in_specs=[_bs((tt, BR), lambda i: (i, cb_q)), _bs((tt, BR), lambda i: (i, cb_kv)),
                           _bs((tt, 128), lambda i: (i, cb_kr)), vec, vec, rp, rp, rp],
                 out_specs=[row, row, rp], out_shape=[_sds((s, BR), BF16), _sds((s, BR), BF16), _sds((s, 128), BF16)],
                 sem=("parallel",))(z, z, z, qg, kvg, c_t, s1_t, s2_t)


def _mla_prep_bwd(z, cb_q, cb_kv, d_qn, d_kvn, d_krp, qg, kvg, c_t, s1_t, s2_t, name):
    s = z.shape[0]
    tt = _tile(s, (TOK_NARROW,))

    def body(cq_ref, ckv_ref, dqn_ref, dkvn_ref, dkr_ref, qg_ref, kvg_ref, c_ref, s1_ref, s2_ref,
             dcq_ref, dckv_ref, dkro_ref, dqg_ref, dkvg_ref):
        @pl.when(pl.program_id(0) == 0)
        def _():
            dqg_ref[...] = jnp.zeros_like(dqg_ref)
            dkvg_ref[...] = jnp.zeros_like(dkvg_ref)

        cq, ckv, dqn, dkvn = cq_ref[...], ckv_ref[...], dqn_ref[...], dkvn_ref[...]
        rq, rkv = _rstd(cq), _rstd(ckv)
        dcq_ref[...] = _rms_bwd(cq, rq, dqn * qg_ref[...]).astype(BF16)
        dckv_ref[...] = _rms_bwd(ckv, rkv, dkvn * kvg_ref[...]).astype(BF16)
        dqg_ref[...] += _colsum(dqn * cq * rq)
        dkvg_ref[...] += _colsum(dkvn * ckv * rkv)
        dkro_ref[...] = _rope_bwd(dkr_ref[...], c_ref[...], s1_ref[...], s2_ref[...]).astype(BF16)

    vec = _bs((1, BR), lambda i: (0, 0))
    rp = _bs((tt, 128), lambda i: (i, 0))
    row = _bs((tt, BR), lambda i: (i, 0))
    return _call(body, name=name, grid=(s // tt,),
                 in_specs=[_bs((tt, BR), lambda i: (i, cb_q)), _bs((tt, BR), lambda i: (i, cb_kv)),
                           row, row, rp, vec, vec, rp, rp, rp],
                 out_specs=[row, row, rp, vec, vec],
                 out_shape=[_sds((s, BR), BF16), _sds((s, BR), BF16), _sds((s, 128), BF16),
                            _sds((1, BR), F32), _sds((1, BR), F32)],
                 sem=("arbitrary",))(z, z, d_qn, d_kvn, d_krp, qg, kvg, c_t, s1_t, s2_t)


def _q_rope(q, c_t, s1_t, s2_t, name):
    s, n = q.shape
    tt = _tile(s, (TOK_WIDE,))

    def body(q_ref, c_ref, s1_ref, s2_ref, out_ref):
        c_v, s1_v, s2_v = c_ref[...], s1_ref[...], s2_ref[...]
        for h in range(HEADS):
            b0 = h * 256
            out_ref[:, b0:b0 + 128] = q_ref[:, b0:b0 + 128].astype(BF16)
            out_ref[:, b0 + 128:b0 + 256] = _rope(q_ref[:, b0 + 128:b0 + 256], c_v, s1_v, s2_v).astype(BF16)

    rp = _bs((tt, 128), lambda i: (i, 0))
    row = _bs((tt, n), lambda i: (i, 0))
    return _call(body, name=name, grid=(s // tt,), in_specs=[row, rp, rp, rp], out_specs=row,
                 out_shape=_sds((s, n), BF16), sem=("parallel",))(q, c_t, s1_t, s2_t)


def _att_scores(q_ref, kn_ref, kr_ref, i, j, tq, tk):
    q = q_ref[...]
    sc = (_dot_nt(q[:, :128], kn_ref[...]) + _dot_nt(q[:, 128:], kr_ref[...])) * ATT_SCALE
    row = i * tq + lax.broadcasted_iota(jnp.int32, (tq, tk), 0)
    col = j * tk + lax.broadcasted_iota(jnp.int32, (tq, tk), 1)
    return jnp.where(col <= row, sc, -1e30)


def _flash_fwd(qb, kv, krp, name):
    s = qb.shape[0]
    t = _tile(s, (ATT_TILE,))
    nq = s // t

    def body(q_ref, kn_ref, v_ref, kr_ref, o_ref, lse_ref, m_s, l_s, acc):
        i, j = pl.program_id(1), pl.program_id(2)

        @pl.when(j == 0)
        def _():
            m_s[...] = jnp.full_like(m_s, -1e30)
            l_s[...] = jnp.zeros_like(l_s)
            acc[...] = jnp.zeros_like(acc)

        @pl.when(j <= i)
        def _():
            sc = _att_scores(q_ref, kn_ref, kr_ref, i, j, t, t)
            m_new = jnp.maximum(m_s[...], jnp.max(sc, axis=-1, keepdims=True))
            p = jnp.exp(sc - m_new)
            alpha = jnp.exp(m_s[...] - m_new)
            l_s[...] = alpha * l_s[...] + jnp.sum(p, axis=-1, keepdims=True)
            acc[...] = alpha * acc[...] + _dot(p.astype(BF16), v_ref[...])
            m_s[...] = m_new

        @pl.when(j == i)
        def _():
            o_ref[...] = (acc[...] / l_s[...]).astype(BF16)
            lse_ref[...] = m_s[...] + jnp.log(l_s[...])

    kmap = lambda off: (lambda h, i, j: (jnp.minimum(j, i), 2 * h + off))
    return _call(body, name=name, grid=(HEADS, nq, nq),
                 in_specs=[_bs((t, 256), lambda h, i, j: (i, h)), _bs((t, 128), kmap(0)), _bs((t, 128), kmap(1)),
                           _bs((t, 128), lambda h, i, j: (jnp.minimum(j, i), 0))],
                 out_specs=[_bs((t, 128), lambda h, i, j: (i, h)), _bs((None, t, 1), lambda h, i, j: (h, i, 0))],
                 out_shape=[_sds((s, HEADS * V_DIM), BF16), _sds((HEADS, s, 1), F32)],
                 scratch=[pltpu.VMEM((t, 1), F32), pltpu.VMEM((t, 1), F32), pltpu.VMEM((t, 128), F32)],
                 sem=("parallel", "parallel", "arbitrary"))(qb, kv, kv, krp)


def _flash_bwd_dq(qb, kv, krp, o, d_o, lse, c_t, s1_t, s2_t, name):
    s = qb.shape[0]
    t = _tile(s, (ATT_TILE,))
    nq = s // t

    def body(q_ref, kn_ref, v_ref, kr_ref, o_ref, do_ref, lse_ref, c_ref, s1_ref, s2_ref, dq_ref, dqn_s, dqr_s, dl_s):
        i, j = pl.program_id(1), pl.program_id(2)

        @pl.when(j == 0)
        def _():
            dqn_s[...] = jnp.zeros_like(dqn_s)
            dqr_s[...] = jnp.zeros_like(dqr_s)
            dl_s[...] = jnp.sum(do_ref[...].astype(F32) * o_ref[...].astype(F32), axis=-1, keepdims=True)

        @pl.when(j <= i)
        def _():
            p = jnp.exp(_att_scores(q_ref, kn_ref, kr_ref, i, j, t, t) - lse_ref[...])
            dp = _dot_nt(do_ref[...], v_ref[...])
            ds = (p * (dp - dl_s[...]) * ATT_SCALE).astype(BF16)
            dqn_s[...] += _dot(ds, kn_ref[...])
            dqr_s[...] += _dot(ds, kr_ref[...])

        @pl.when(j == i)
        def _():
            dq_ref[:, :128] = dqn_s[...].astype(BF16)
            dq_ref[:, 128:] = _rope_bwd(dqr_s[...], c_ref[...], s1_ref[...], s2_ref[...]).astype(BF16)

    kmap = lambda off: (lambda h, i, j: (jnp.minimum(j, i), 2 * h + off))
    qrow = _bs((t, 128), lambda h, i, j: (i, h))
    rp = _bs((t, 128), lambda h, i, j: (i, 0))
    return _call(body, name=name, grid=(HEADS, nq, nq),
                 in_specs=[_bs((t, 256), lambda h, i, j: (i, h)), _bs((t, 128), kmap(0)), _bs((t, 128), kmap(1)),
                           _bs((t, 128), lambda h, i, j: (jnp.minimum(j, i), 0)), qrow, qrow,
                           _bs((None, t, 1), lambda h, i, j: (h, i, 0)), rp, rp, rp],
                 out_specs=_bs((t, 256), lambda h, i, j: (i, h)), out_shape=_sds((s, HEADS * 256), BF16),
                 scratch=[pltpu.VMEM((t, 128), F32), pltpu.VMEM((t, 128), F32), pltpu.VMEM((t, 1), F32)],
                 sem=("parallel", "parallel", "arbitrary"))(qb, kv, kv, krp, o, d_o, lse, c_t, s1_t, s2_t)


def _flash_bwd_dkv(qb, kv, krp, o, d_o, lse, name):
    s = qb.shape[0]
    t = _tile(s, (ATT_TILE,))
    nq = s // t

    def body(q_ref, kn_ref, v_ref, kr_ref, o_ref, do_ref, lse_ref, dkv_ref, dkr_ref, dk_s, dv_s, dkr_s):
        j, h, i = pl.program_id(0), pl.program_id(1), pl.program_id(2)

        @pl.when(i == j)
        def _():
            dk_s[...] = jnp.zeros_like(dk_s)
            dv_s[...] = jnp.zeros_like(dv_s)

        @pl.when(jnp.logical_and(i == j, h == 0))
        def _():
            dkr_s[...] = jnp.zeros_like(dkr_s)

        @pl.when(i >= j)
        def _():
            q, do = q_ref[...], do_ref[...]
            p = jnp.exp(_att_scores(q_ref, kn_ref, kr_ref, i, j, t, t) - lse_ref[...])
            delta = jnp.sum(do.astype(F32) * o_ref[...].astype(F32), axis=-1, keepdims=True)
            dv_s[...] += _dot_tn(p.astype(BF16), do)
            ds = (p * (_dot_nt(do, v_ref[...]) - delta) * ATT_SCALE).astype(BF16)
            dk_s[...] += _dot_tn(ds, q[:, :128])
            dkr_s[...] += _dot_tn(ds, q[:, 128:])

        @pl.when(i == nq - 1)
        def _():
            dkv_ref[:, :128] = dk_s[...].astype(BF16)
            dkv_ref[:, 128:] = dv_s[...].astype(BF16)

        @pl.when(jnp.logical_and(i == nq - 1, h == HEADS - 1))
        def _():
            dkr_ref[...] = dkr_s[...]

    qi = lambda j, h, i: jnp.maximum(i, j)
    qrow = _bs((t, 128), lambda j, h, i: (qi(j, h, i), h))
    return _call(body, name=name, grid=(nq, HEADS, nq),
                 in_specs=[_bs((t, 256), lambda j, h, i: (qi(j, h, i), h)),
                           _bs((t, 128), lambda j, h, i: (j, 2 * h)), _bs((t, 128), lambda j, h, i: (j, 2 * h + 1)),
                           _bs((t, 128), lambda j, h, i: (j, 0)), qrow, qrow,
                           _bs((None, t, 1), lambda j, h, i: (h, qi(j, h, i), 0))],
                 out_specs=[_bs((t, 256), lambda j, h, i: (j, h)), _bs((t, 128), lambda j, h, i: (j, 0))],
                 out_shape=[_sds((s, HEADS * 256), BF16), _sds((s, 128), F32)],
                 scratch=[pltpu.VMEM((t, 128), F32), pltpu.VMEM((t, 128), F32), pltpu.VMEM((t, 128), F32)],
                 sem=("parallel", "arbitrary", "arbitrary"))(qb, kv, kv, krp, o, d_o, lse)


def _merge_tiles(s, projs):
    return _tile(s, (TOK_WIDE,)), _div_tile(projs[0].shape[2], 512, 128)


def _merge_specs(s, d, tt, tn):
    nb, npb = d // tn, d // N_CHIPS // tn
    br = lambda w: _bs((tt, w), lambda i, n: (i, 0))
    pw = lambda k: _bs((None, k, tn), lambda i, n: (n // npb, 0, n % npb))
    gate = lambda g: _bs((tt, tn), lambda i, n: (i, g * nb + n))
    return [br(BR), br(BR), br(BR), br(HEADS * V_DIM), pw(BR), pw(BR), pw(BR), pw(HEADS * V_DIM)] + \
           [gate(g) for g in range(4)]


def _merge_fwd(z, branches, projs, name):
    s, d = z.shape[0], N_CHIPS * projs[0].shape[2]
    tt, tn = _merge_tiles(s, projs)

    def body(*refs):
        b_refs, p_refs, g_refs, out_ref = refs[0:4], refs[4:8], refs[8:12], refs[12]
        acc = None
        for b_ref, p_ref, g_ref in zip(b_refs, p_refs, g_refs):
            term = _sigmoid(g_ref[...]) * _dot(b_ref[...], p_ref[...])
            acc = term if acc is None else acc + term
        out_ref[...] = acc.astype(BF16)

    return _call(body, name=name, grid=(s // tt, d // tn), in_specs=_merge_specs(s, d, tt, tn),
                 out_specs=_bs((tt, tn), lambda i, n: (i, n)), out_shape=_sds((s, d), BF16),
                 sem=("parallel", "parallel"))(*branches, *projs, z, z, z, z)


def _merge_bwd(z, branches, projs, d_merged, name):
    s, d = z.shape[0], N_CHIPS * projs[0].shape[2]
    tt, tn = _merge_tiles(s, projs)

    def body(*refs):
        b_refs, p_refs, g_refs, dm_ref = refs[0:4], refs[4:8], refs[8:12], refs[12]
        dy_refs, dg_refs = refs[13:17], refs[17:21]
        dm = dm_ref[...]
        for b_ref, p_ref, g_ref, dy_ref, dg_ref in zip(b_refs, p_refs, g_refs, dy_refs, dg_refs):
            sg = _sigmoid(g_ref[...])
            dy_ref[...] = (dm * sg).astype(BF16)
            dg_ref[...] = (dm * _dot(b_ref[...], p_ref[...]) * sg * (1.0 - sg)).astype(BF16)

    tile = _bs((tt, tn), lambda i, n: (i, n))
    outs = _call(body, name=name, grid=(s // tt, d // tn), in_specs=_merge_specs(s, d, tt, tn) + [tile],
                 out_specs=[tile] * 8, out_shape=[_sds((s, d), BF16)] * 8,
                 sem=("parallel", "parallel"))(*branches, *projs, z, z, z, z, d_merged)
    return outs[:4], outs[4:]


def _ew_rows(rows, cols, align=16):
    lanes = -(-cols // 128) * 128
    return _div_tile(rows, max(EW_BLOCK_BYTES // (lanes * 4), align), align)


def _add_pair(a, b, name):
    n, r, c = a.shape
    tr = _ew_rows(r, c)

    def body(a_ref, b_ref, o_ref):
        o_ref[...] = (a_ref[...].astype(F32) + b_ref[...].astype(F32)).astype(BF16)

    blk = _bs((None, tr, c), lambda k, i: (k, i, 0))
    return _call(body, name=name, grid=(n, r // tr), in_specs=[blk, blk], out_specs=blk,
                 out_shape=_sds((n, r, c), BF16), sem=("parallel", "parallel"))(a, b)


def _sum_chips(land, own, name):
    n, r, c = land.shape
    tr = _ew_rows(r, c)

    def body(l_ref, o_ref, out_ref):
        acc = l_ref[0].astype(F32)
        for k in range(1, n):
            acc = acc + l_ref[k].astype(F32)
        out_ref[...] = acc + o_ref[...].astype(F32)

    blk = _bs((tr, c), lambda i: (i, 0))
    return _call(body, name=name, grid=(r // tr,), in_specs=[_bs((n, tr, c), lambda i: (0, i, 0)), blk],
                 out_specs=blk, out_shape=_sds((r, c), F32), sem=("parallel",))(land, own)


def _sum_slots(buf, name):
    n, r, c = buf.shape
    tr = _div_tile(r, max(EW_BLOCK_BYTES // (c * 4 * n), 8), 8)

    def body(b_ref, o_ref):
        acc = b_ref[0].astype(F32)
        for k in range(1, n):
            acc = acc + b_ref[k].astype(F32)
        o_ref[...] = acc

    return _call(body, name=name, grid=(r // tr,), in_specs=[_bs((n, tr, c), lambda i: (0, i, 0))],
                 out_specs=_bs((tr, c), lambda i: (i, 0)), out_shape=_sds((r, c), F32), sem=("parallel",))(buf)


def _adam_update(w, g, m, v):
    nm = ADAM_B1 * m + (1.0 - ADAM_B1) * g
    nv = ADAM_B2 * v + (1.0 - ADAM_B2) * jnp.square(g)
    m_hat = nm / (1.0 - ADAM_B1 ** ADAM_STEP)
    v_hat = nv / (1.0 - ADAM_B2 ** ADAM_STEP)
    return -ADAM_LR * (m_hat / (jnp.sqrt(v_hat) + ADAM_EPS) + ADAM_WD * w), nm, nv


def _adamw(w, g, m, v, name):
    shape = w.shape
    cols = shape[-1]
    rows = 1
    for dim in shape[:-1]:
        rows *= dim
    w2, g2, m2, v2 = (t.reshape(rows, cols) for t in (w, g, m, v))
    tr = _ew_rows(rows, cols, align=8)

    def body(w_ref, g_ref, m_ref, v_ref, d_ref, nm_ref, nv_ref):
        d_ref[...], nm_ref[...], nv_ref[...] = _adam_update(w_ref[...], g_ref[...], m_ref[...], v_ref[...])

    blk = _bs((tr, cols), lambda i: (i, 0))
    outs = _call(body, name=name, grid=(rows // tr,), in_specs=[blk] * 4, out_specs=[blk] * 3,
                 out_shape=[_sds((rows, cols), F32)] * 3, sem=("parallel",))(w2, g2, m2, v2)
    return [o.reshape(shape) for o in outs]


def _adamw_big(w, gs, m, v, name):
    depth, r, c = w.shape
    tr = _ew_rows(r, c, align=8)
    nb = r // tr

    def body(w_ref, m_ref, v_ref, *rest):
        g_refs, (g_out, d_ref, nm_ref, nv_ref) = rest[:depth], rest[depth:]
        layer = pl.program_id(0)
        gv = g_refs[0][...]
        for k in range(1, depth):
            gv = jnp.where(layer == k, g_refs[k][...], gv)
        g_out[...] = gv
        d_ref[...], nm_ref[...], nv_ref[...] = _adam_update(w_ref[...], gv, m_ref[...], v_ref[...])

    blk = _bs((None, tr, c), lambda l, i: (l, i, 0))
    g_spec = lambda k: _bs((tr, c), lambda l, i: (jnp.where(l == k, i, jnp.where(l < k, 0, nb - 1)), 0))
    return _call(body, name=name, grid=(depth, nb), in_specs=[blk] * 3 + [g_spec(k) for k in range(depth)],
                 out_specs=[blk] * 4, out_shape=[_sds((depth, r, c), F32)] * 4,
                 sem=("arbitrary", "arbitrary"))(w, m, v, *gs)


ANY = pl.BlockSpec(memory_space=pl.ANY)
HBM = pl.BlockSpec(memory_space=pltpu.HBM)
SEM = pl.BlockSpec(memory_space=pltpu.SEMAPHORE)
DATAFLOW = pltpu.SideEffectType.DATAFLOW_SIDE_EFFECTING
DMA_SEMS = pltpu.SemaphoreType.DMA


def _place():
    return lax.axis_index("x"), lax.axis_index("y"), lax.axis_index("c")


def _other_chips(x, y):
    return [(1 - x, y), (x, 1 - y), (1 - x, 1 - y)]


def _half_rows(rows, c):
    half = rows // 2
    assert half % 16 == 0
    return pl.ds(pl.multiple_of(c * half, 16), half)


def _in_hbm(a):
    return pltpu.with_memory_space_constraint(a, pltpu.HBM)


def _ici_copies(mode, src, land, send_sems, recv_sems):
    x, y, c = _place()
    my = 2 * x + y
    out = []
    for i in range(len(src)):
        for j, chip in enumerate(_other_chips(x, y)):
            peer = 2 * chip[0] + chip[1]
            if mode == 'gather':
                rows = _half_rows(src[i].shape[0], c)
                s_ref, d_send, d_recv = src[i].at[rows], land[i].at[my, rows], land[i].at[peer, rows]
            else:
                s_ref, d_send, d_recv = src[i].at[peer], land[i].at[j], land[i].at[j]
            pair = [pltpu.make_async_remote_copy(src_ref=s_ref, dst_ref=dst, send_sem=send_sems.at[3 * i + j],
                                                 recv_sem=recv_sems.at[3 * i + j], device_id=(chip[0], chip[1], c),
                                                 device_id_type=MESH) for dst in (d_send, d_recv)]
            out.append(pair)
    return out


def _ici_start(mode, srcs, land_shapes, name):
    n = len(srcs)
    lands = [_in_hbm(lax.empty(shp, s.dtype)) for shp, s in zip(land_shapes, srcs)]

    def body(*refs):
        src, land, send_sems, recv_sems, token = refs[:n], refs[n:2 * n], refs[2 * n], refs[2 * n + 1], refs[-1]
        for send, _ in _ici_copies(mode, src, land, send_sems, recv_sems):
            send.start()
        token[...] = jnp.zeros_like(token)

    outs = pl.pallas_call(
        body, name=name,
        out_shape=(DMA_SEMS((3 * n,)), DMA_SEMS((3 * n,)), *[pltpu.HBM(s.shape, s.dtype) for s in srcs],
                   *[pltpu.HBM(l.shape, l.dtype) for l in lands], _sds((8, 128), F32)),
        in_specs=[HBM] * (2 * n), out_specs=(SEM, SEM, *[HBM] * (2 * n), pl.BlockSpec(memory_space=pltpu.VMEM)),
        input_output_aliases={i: 2 + i for i in range(2 * n)},
        compiler_params=pltpu.CompilerParams(has_side_effects=DATAFLOW))(*[_in_hbm(s) for s in srcs], *lands)
    return outs[0], outs[1], list(outs[2:2 + n]), list(outs[2 + n:2 + 2 * n]), outs[-1]


def _ici_wait(mode, started, after, name):
    send_sems, recv_sems, src_thru, land_thru, _ = started
    n = len(src_thru)

    def body(*refs):
        src, land, send_s, recv_s = refs[:n], refs[n:2 * n], refs[2 * n], refs[2 * n + 1]
        for send, recv in _ici_copies(mode, src, land, send_s, recv_s):
            send.wait_send()
            recv.wait_recv()

    outs = pl.pallas_call(
        body, name=name, out_shape=tuple(pltpu.HBM(t.shape, t.dtype) for t in src_thru + land_thru),
        in_specs=[HBM] * (2 * n) + [SEM, SEM, ANY], out_specs=(HBM,) * (2 * n),
        input_output_aliases={i: i for i in range(2 * n)},
        compiler_params=pltpu.CompilerParams(has_side_effects=DATAFLOW))(*src_thru, *land_thru, send_sems, recv_sems,
                                                                        after)
    return list(outs[n:])


def _sibling_call(body, name, inputs, out_shape, n_copies, n_local, aliases=None):
    return pl.pallas_call(body, name=name, out_shape=out_shape, in_specs=[ANY] * len(inputs),
                          out_specs=[ANY] * len(out_shape), input_output_aliases=aliases or {},
                          scratch_shapes=[DMA_SEMS((n_copies,)), DMA_SEMS((n_copies,)), DMA_SEMS((n_local,))])(*inputs)


def _share_weights(srcs, lands, name):
    n = len(srcs)

    def body(*refs):
        src, land_in, land = refs[:n], refs[n:2 * n], refs[2 * n:3 * n]
        send_sems, recv_sems, loc_sems = refs[3 * n:3 * n + 3]
        x, y, c = _place()
        my, sib = 2 * x + y, (x, y, 1 - c)
        locs = [pltpu.make_async_copy(src[i], land[i].at[my], loc_sems.at[i]) for i in range(n)]
        for cp in locs:
            cp.start()

        def d2d(i, j, peer, rows, from_ref):
            return pltpu.make_async_remote_copy(src_ref=from_ref[i].at[peer, rows], dst_ref=land[i].at[peer, rows],
                                                send_sem=send_sems.at[3 * i + j], recv_sem=recv_sems.at[3 * i + j],
                                                device_id=sib, device_id_type=MESH)

        sends = []
        for i in range(n):
            for j, chip in enumerate(_other_chips(x, y)):
                cp = d2d(i, j, 2 * chip[0] + chip[1], _half_rows(src[i].shape[0], c), land_in)
                cp.start()
                sends.append(cp)
        for i in range(n):
            for j, chip in enumerate(_other_chips(x, y)):
                d2d(i, j, 2 * chip[0] + chip[1], _half_rows(src[i].shape[0], 1 - c), land_in).wait_recv()
        for cp in sends:
            cp.wait_send()
        for cp in locs:
            cp.wait()

    return list(_sibling_call(body, name, list(srcs) + list(lands), [_sds(l.shape, l.dtype) for l in lands], 3 * n, n,
                              aliases={n + i: i for i in range(n)}))


def _swap_grad_halves(gs, name):
    n = len(gs)

    def body(*refs):
        g, own, got = refs[:n], refs[n:2 * n], refs[2 * n:3 * n]
        send_sems, recv_sems, loc_sems = refs[3 * n:3 * n + 3]
        x, y, c = _place()
        sib = (x, y, 1 - c)
        locs, sends = [], []
        for i in range(n):
            r = g[i].shape[1]
            locs.append(pltpu.make_async_copy(g[i].at[:, _half_rows(r, c)], own[i], loc_sems.at[i]))
            sends.append(pltpu.make_async_remote_copy(src_ref=g[i].at[:, _half_rows(r, 1 - c)], dst_ref=got[i],
                                                      send_sem=send_sems.at[i], recv_sem=recv_sems.at[i],
                                                      device_id=sib, device_id_type=MESH))
        for cp in locs + sends:
            cp.start()
        for cp in sends:
            cp.wait()
        for cp in locs:
            cp.wait()

    half = [_sds((g.shape[0], g.shape[1] // 2, g.shape[2]), g.dtype) for g in gs]
    outs = _sibling_call(body, name, list(gs), half + half, n, n)
    return list(outs[:n]), list(outs[n:])


def _share_final(fins, name):
    n = len(fins)

    def body(*refs):
        fin, full = refs[:n], refs[n:2 * n]
        send_sems, recv_sems, loc_sems = refs[2 * n:2 * n + 3]
        x, y, c = _place()
        sib = (x, y, 1 - c)
        locs, sends = [], []
        for i in range(n):
            r = full[i].shape[0]
            locs.append(pltpu.make_async_copy(fin[i], full[i].at[_half_rows(r, c)], loc_sems.at[i]))
            sends.append(pltpu.make_async_remote_copy(src_ref=fin[i], dst_ref=full[i].at[_half_rows(r, c)],
                                                      send_sem=send_sems.at[i], recv_sem=recv_sems.at[i],
                                                      device_id=sib, device_id_type=MESH))
        for cp in locs + sends:
            cp.start()
        for i in range(n):
            sends[i].wait_send()
            pltpu.make_async_remote_copy(src_ref=fin[i], dst_ref=full[i].at[_half_rows(full[i].shape[0], 1 - c)],
                                         send_sem=send_sems.at[i], recv_sem=recv_sems.at[i], device_id=sib,
                                         device_id_type=MESH).wait_recv()
        for cp in locs:
            cp.wait()

    return list(_sibling_call(body, name, list(fins), [_sds((2 * f.shape[0], f.shape[1]), f.dtype) for f in fins],
                              n, n))


def _gather_all(buf, name):
    flips = [(fx, fy, fc) for fx in (0, 1) for fy in (0, 1) for fc in (0, 1) if (fx, fy, fc) != (0, 0, 0)]

    def body(src, dst, send_sems, recv_sems, loc_sem):
        x, y, c = _place()
        me = 4 * x + 2 * y + c
        loc = pltpu.make_async_copy(src, dst.at[me], loc_sem.at[0])
        loc.start()
        peers = [(1 - x if fx else x, 1 - y if fy else y, 1 - c if fc else c) for fx, fy, fc in flips]

        def cp(j, peer, slot):
            return pltpu.make_async_remote_copy(src_ref=src, dst_ref=dst.at[slot], send_sem=send_sems.at[j],
                                                recv_sem=recv_sems.at[j], device_id=peer, device_id_type=MESH)

        sends = [cp(j, peer, me) for j, peer in enumerate(peers)]
        for s_ in sends:
            s_.start()
        for j, peer in enumerate(peers):
            cp(j, peer, 4 * peer[0] + 2 * peer[1] + peer[2]).wait_recv()
        for s_ in sends:
            s_.wait_send()
        loc.wait()

    return _sibling_call(body, name, [buf], [_sds((8,) + buf.shape, buf.dtype)], 7, 1)[0]


def _layout(shapes):
    out, r0 = [], 0
    for name, shape in shapes:
        n = 1
        for dim in shape:
            n *= dim
        nr = -(-n // PACK_COLS)
        nr = -(-nr // PACK_ROW_ALIGN) * PACK_ROW_ALIGN
        out.append((name, tuple(shape), r0, nr))
        r0 += nr
    return out, r0


def _pack(layout, get, dtype):
    parts = []
    for name, shape, _, nr in layout:
        flat = get(name).astype(dtype).reshape(-1)
        pad = nr * PACK_COLS - flat.shape[0]
        if pad:
            flat = jnp.pad(flat, (0, pad))
        parts.append(flat.reshape(nr, PACK_COLS))
    return jnp.concatenate(parts, axis=0)


def _unpack(layout, buf):
    out = {}
    for name, shape, r0, nr in layout:
        n = 1
        for dim in shape:
            n *= dim
        out[name] = buf[r0:r0 + nr].reshape(-1)[:n].reshape(shape)
    return out


def _rope_tables(positions):
    inv_freq = ROPE_THETA ** (-jnp.arange(0, QK_ROPE, 2, dtype=F32) / QK_ROPE)
    ang = positions.astype(F32)[:, None] * inv_freq
    cos, sin, zero = jnp.cos(ang), jnp.sin(ang), jnp.zeros_like(ang)
    c_t = jnp.concatenate([cos, cos, zero, zero], axis=1)
    s1_t = jnp.concatenate([-sin, zero, zero, zero], axis=1)
    s2_t = jnp.concatenate([zero, sin, zero, zero], axis=1)
    return c_t, s1_t, s2_t


def _shard_cols(shards, lo, hi):
    ws = shards.shape[2]
    out = []
    for k in range(shards.shape[0]):
        a, b = max(lo, k * ws), min(hi, (k + 1) * ws)
        if a < b:
            out.append(shards[k][:, a - k * ws:b - k * ws])
    return out


def _w_in_to_kernel_layout(shards, nbc, nz):
    d, n_in = shards.shape[1], N_CHIPS * shards.shape[2]
    return jnp.concatenate(_shard_cols(shards, nbc, n_in) + _shard_cols(shards, 0, nbc)
                           + [jnp.zeros((d, nz - n_in), shards.dtype)], axis=1)


def _w_in_grad_to_shards(gx, nbc, n_in):
    ws, n_gate = n_in // N_CHIPS, n_in - nbc
    out = []
    for k in range(N_CHIPS):
        lo, hi, parts = k * ws, (k + 1) * ws, []
        if lo < nbc:
            parts.append(gx[:, n_gate + lo:n_gate + min(hi, nbc)])
        if hi > nbc:
            parts.append(gx[:, max(lo, nbc) - nbc:hi - nbc])
        out.append(parts[0] if len(parts) == 1 else jnp.concatenate(parts, axis=1))
    return jnp.stack(out)


def _w_uq_to_kernel_layout(shards):
    full = jnp.concatenate([shards[k] for k in range(N_CHIPS)], axis=1).reshape(-1, HEADS, QK_NOPE + QK_ROPE)
    pad = jnp.zeros((full.shape[0], HEADS, 256 - QK_NOPE - QK_ROPE), full.dtype)
    return jnp.concatenate([full, pad], axis=2).reshape(-1, HEADS * 256)


def _w_uq_grad_to_shards(gx):
    full = gx.reshape(-1, HEADS, 256)[:, :, :QK_NOPE + QK_ROPE].reshape(gx.shape[0], -1)
    ws = full.shape[1] // N_CHIPS
    return jnp.stack([full[:, k * ws:(k + 1) * ws] for k in range(N_CHIPS)])


def _layer_forward(l, x_in, h, w, sm, tabs, col):
    c_t, s1_t, s2_t = tabs
    tag = f"l{l}"
    z = _mm(h, w['w_in_x'], name=f"z_{tag}")
    b_pool = _pool_fwd(z, col['pool'], sm['pool_w16'], sm['pool_scale'], f"pool_fwd_{tag}")
    b_conv, ypre = _conv_fwd(z, col['conv'], sm['conv_w'], sm['conv_b'], sm['conv_norm_g'], sm['conv_norm_b'],
                             f"conv_fwd_{tag}")
    b_sgu = _sgu_fwd(z, col['sgu'], sm['sgu_norm_g'], sm['sgu_norm_b'], sm['sgu_w16'], sm['sgu_bias_full'],
                     f"sgu_fwd_{tag}")
    qn, kvn, krp = _mla_prep(z, col['q'], col['kv'], col['kr'], sm['q_norm_g'], sm['kv_norm_g'], c_t, s1_t, s2_t,
                             f"mla_prep_{tag}")
    q = _mm(qn, w['w_uq_x'], name=f"q_{tag}")
    qb = _q_rope(q, c_t, s1_t, s2_t, f"q_rope_{tag}")
    kv = _mm(kvn, w['w_ukv'], name=f"kv_{tag}", b_split=True, out_dtypes=(BF16,))
    o, lse = _flash_fwd(qb, kv, krp, f"flash_fwd_{tag}")
    branches = (b_pool, b_conv, b_sgu, o)
    projs = (w['pool_proj'], w['conv_proj'], w['sgu_proj'], w['attn_proj'])
    merged = _merge_fwd(z, branches, projs, f"merge_fwd_{tag}")
    o2 = _mm(merged, w['w_out'], name=f"o2_{tag}")
    x1, h2 = _resid_norm_fwd(x_in, o2, sm['post_mix_g'], sm['pre_mlp_g'], f"mix_out_{tag}")
    u, a = _mm(h2, w['w_up'], name=f"up_{tag}", b_split=True, out_dtypes=(F32, BF16),
               epilogue=lambda acc: (acc, jnp.square(jnp.maximum(acc, 0.0))))
    f = _mm(a, w['w_down'], name=f"down_{tag}")
    return dict(x_in=x_in, h=h, z=z, ypre=ypre, branches=branches, projs=projs, qn=qn, kvn=kvn, krp=krp, qb=qb, kv=kv,
                o=o, lse=lse, merged=merged, o2=o2, x1=x1, h2=h2, u=u, a=a, f=f)


def _layer_backward(l, sv, d_f, d_x2, w, sm, tabs, col, nz, nbc):
    c_t, s1_t, s2_t = tabs
    tag = f"l{l}"
    s, d = sv['x_in'].shape
    gb, gs = {}, {}
    row_shards = lambda g: g.reshape(N_CHIPS, g.shape[0] // N_CHIPS, g.shape[1])
    d_u = _mm(d_f, w['w_down'], name=f"d_u_{tag}", tb=True, out_dtypes=(BF16,), extras=(sv['u'],),
              epilogue=lambda acc, u: (acc * (2.0 * jnp.maximum(u, 0.0)),))
    gb['w_down'] = row_shards(_mm(sv['a'], d_f, name=f"g_down_{tag}", ta=True, out_dtypes=(BF16,)))
    d_h2 = _mm(d_u, w['w_up'], name=f"d_h2_{tag}", tb=True, b_split=True)
    gb['w_up'] = _mm(sv['h2'], d_u, name=f"g_up_{tag}", ta=True, out_split=True, out_dtypes=(BF16,))
    d_x1, d_o2, gs['post_mix_g'], gs['pre_mlp_g'] = _resid_norm_bwd(
        d_x2, d_h2, sv['x1'], sv['o2'], sm['post_mix_g'], sm['pre_mlp_g'], f"mix_out_bwd_{tag}")
    d_merged = _mm(d_o2, w['w_out'], name=f"d_merged_{tag}", tb=True)
    gb['w_out'] = row_shards(_mm(sv['merged'], d_o2, name=f"g_out_{tag}", ta=True, out_dtypes=(BF16,)))
    d_ys, d_gates = _merge_bwd(sv['z'], sv['branches'], sv['projs'], d_merged, f"merge_bwd_{tag}")
    d_br = []
    for k, pname in enumerate(('pool_proj', 'conv_proj', 'sgu_proj', 'attn_proj')):
        last = pname == 'attn_proj'
        d_br.append(_mm(d_ys[k], w[pname], name=f"d_{pname}_in_{tag}", tb=True, b_split=True,
                        out_dtypes=(BF16 if last else F32,)))
        gb[pname] = _mm(sv['branches'][k], d_ys[k], name=f"g_{pname}_{tag}", ta=True, out_split=True,
                        out_dtypes=(BF16,))
    dz_pool, gs['pool_w'], gs['pool_scale'] = _pool_bwd(sv['z'], col['pool'], d_br[0], sm['pool_w16'],
                                                        sm['pool_scale'], f"pool_bwd_{tag}")
    dz_ca, dz_cg, gs['conv_w'], gs['conv_b'], gs['conv_norm_g'], gs['conv_norm_b'] = _conv_bwd(
        sv['z'], col['conv'], d_br[1], sv['ypre'], sm['conv_w'], sm['conv_norm_g'], sm['conv_norm_b'],
        f"conv_bwd_{tag}")
    dz_su, dz_sv, g_sgu_w, g_sgu_bfull, gs['sgu_norm_g'], gs['sgu_norm_b'] = _sgu_bwd(
        sv['z'], col['sgu'], d_br[2], sm['sgu_norm_g'], sm['sgu_norm_b'], sm['sgu_w16'], sm['sgu_bias_full'],
        f"sgu_bwd_{tag}")
    gs['sgu_w'] = g_sgu_w * sm['tril']
    gs['sgu_b'] = jnp.sum(g_sgu_bfull, axis=-1)
    d_qb = _flash_bwd_dq(sv['qb'], sv['kv'], sv['krp'], sv['o'], d_br[3], sv['lse'], c_t, s1_t, s2_t,
                         f"flash_dq_{tag}")
    d_kv, d_krp = _flash_bwd_dkv(sv['qb'], sv['kv'], sv['krp'], sv['o'], d_br[3], sv['lse'], f"flash_dkv_{tag}")
    d_qn = _mm(d_qb, w['w_uq_x'], name=f"d_qn_{tag}", tb=True)
    gb['w_uq'] = _w_uq_grad_to_shards(_mm(sv['qn'], d_qb, name=f"g_uq_{tag}", ta=True, out_dtypes=(BF16,)))
    d_kvn = _mm(d_kv, w['w_ukv'], name=f"d_kvn_{tag}", tb=True, b_split=True)
    gb['w_ukv'] = _mm(sv['kvn'], d_kv, name=f"g_ukv_{tag}", ta=True, out_split=True, out_dtypes=(BF16,))
    dz_q, dz_kv, dz_kr, gs['q_norm_g'], gs['kv_norm_g'] = _mla_prep_bwd(
        sv['z'], col['q'], col['kv'], d_qn, d_kvn, d_krp, sm['q_norm_g'], sm['kv_norm_g'], c_t, s1_t, s2_t,
        f"mla_prep_bwd_{tag}")
    used = 4 * d + 7 * BR + 128
    dz = jnp.concatenate(list(d_gates) + [dz_pool, dz_ca, dz_cg, dz_su, dz_sv, dz_q, dz_kv, dz_kr,
                                          jnp.zeros((s, nz - used), BF16)], axis=1)
    d_h = _mm(dz, w['w_in_x'], name=f"d_h_{tag}", tb=True)
    gb['w_in'] = _w_in_grad_to_shards(_mm(sv['h'], dz, name=f"g_in_{tag}", ta=True, out_dtypes=(BF16,)),
                                      nbc, 4 * d + nbc)
    return d_x1, d_h, gb, gs


def _reduce_start(l, gb):
    own4, sib4 = _swap_grad_halves([gb[n] for n in BIG_NAMES], f"swap_grads_l{l}")
    partial = [_add_pair(o, g, f"add_sibling_{n}_l{l}") for n, o, g in zip(BIG_NAMES, own4, sib4)]
    return _ici_start('scatter', partial, [(3,) + p.shape[1:] for p in partial], f"scatter_start_l{l}")


def _reduce_finish(l, started, my_chip, after):
    own = [lax.dynamic_index_in_dim(p, my_chip, 0, keepdims=False) for p in started[2]]
    lands = _ici_wait('scatter', started, after, f"scatter_wait_l{l}")
    fins = [_sum_chips(ld, ow, f"sum_chips_{n}_l{l}") for n, ld, ow in zip(BIG_NAMES, lands, own)]
    return dict(zip(BIG_NAMES, _share_final(fins, f"share_final_l{l}")))


def kernel(x, positions, pre_mix_g, w_in, pool_w, pool_scale, pool_proj, conv_w, conv_b, conv_norm_g, conv_norm_b, conv_proj, sgu_norm_g, sgu_norm_b, sgu_w, sgu_b, sgu_proj, q_norm_g, w_uq, kv_norm_g, w_ukv, attn_proj, w_out, post_mix_g, pre_mlp_g, w_up, w_down, post_mlp_g, loss_target, m_pre_mix_g, m_w_in, m_pool_w, m_pool_scale, m_pool_proj, m_conv_w, m_conv_b, m_conv_norm_g, m_conv_norm_b, m_conv_proj, m_sgu_norm_g, m_sgu_norm_b, m_sgu_w, m_sgu_b, m_sgu_proj, m_q_norm_g, m_w_uq, m_kv_norm_g, m_w_ukv, m_attn_proj, m_w_out, m_post_mix_g, m_pre_mlp_g, m_w_up, m_w_down, m_post_mlp_g, v_pre_mix_g, v_w_in, v_pool_w, v_pool_scale, v_pool_proj, v_conv_w, v_conv_b, v_conv_norm_g, v_conv_norm_b, v_conv_proj, v_sgu_norm_g, v_sgu_norm_b, v_sgu_w, v_sgu_b, v_sgu_proj, v_q_norm_g, v_w_uq, v_kv_norm_g, v_w_ukv, v_attn_proj, v_w_out, v_post_mix_g, v_pre_mlp_g, v_w_up, v_w_down, v_post_mlp_g):
    arg = dict(locals())
    depth = pre_mix_g.shape[0]
    s, d = x.shape[1], x.shape[2]
    nbc = 7 * BR + QK_ROPE
    nz = 4 * d + -(-(7 * BR + 128) // 512) * 512
    g0 = 4 * d // BR
    col = dict(pool=g0, conv=g0 + 1, sgu=g0 + 3, q=g0 + 5, kv=g0 + 6, kr=(4 * d + 7 * BR) // 128)
    my_x, my_y, my_c = _place()
    my_chip = 2 * my_x + my_y
    x2 = x.reshape(s, d)
    target = loss_target.reshape(s, d)
    tabs = _rope_tables(positions.reshape(s))
    how_of = dict(BIG)

    wb = [[arg[n][l].astype(BF16) for n in BIG_NAMES] for l in range(depth)]
    gather = [_ici_start('gather', wb[l], [(N_CHIPS,) + t.shape for t in wb[l]], f"gather_start_l{l}")
              for l in range(depth)]
    started = gather[0][4][0, 0]
    for l in range(1, depth):
        started = started + gather[l][4][0, 0]

    conv_layout, _ = _layout([('conv_w', conv_w.shape)])
    conv_all = _gather_all(_pack(conv_layout, lambda n: arg[n], F32), "gather_conv_w")
    conv_w_full = jnp.concatenate([_unpack(conv_layout, conv_all[4 * (k // 2) + 2 * (k % 2)])['conv_w']
                                   for k in range(4)], axis=-1)
    tril = jnp.tril(jnp.ones((GROUP, GROUP), F32))
    smalls = []
    for l in range(depth):
        sm = {n: arg[n][l][None, :] for n in ('pre_mix_g', 'pool_scale', 'conv_b', 'conv_norm_g', 'conv_norm_b',
                                               'sgu_norm_g', 'sgu_norm_b', 'q_norm_g', 'kv_norm_g', 'post_mix_g',
                                               'pre_mlp_g', 'post_mlp_g')}
        sm['pool_w16'] = pool_w[l].astype(BF16)
        sm['sgu_w16'] = (sgu_w[l] * tril).astype(BF16)
        sm['sgu_bias_full'] = jnp.broadcast_to(sgu_b[l][:, :, None], (BR // GROUP, GROUP, GROUP))
        sm['conv_w'] = conv_w_full[l]
        sm['tril'] = tril
        smalls.append(sm)

    def layer_weights(l, after):
        lands = _ici_wait('gather', gather[l], after, f"gather_wait_l{l}")
        full = dict(zip(BIG_NAMES, _share_weights(wb[l], lands, f"share_weights_l{l}")))
        w_l = {n: (full[n] if how_of[n] == 'col' else full[n].reshape(-1, full[n].shape[2]))
               for n in BIG_NAMES if n not in ('w_in', 'w_uq')}
        w_l['w_in_x'] = _w_in_to_kernel_layout(full['w_in'], nbc, nz)
        w_l['w_uq_x'] = _w_uq_to_kernel_layout(full['w_uq'])
        return w_l

    weights, saved = [], []
    x_cur = x2
    h = _rms_fwd(x2, smalls[0]['pre_mix_g'] + started, "rms_in")
    for l in range(depth):
        weights.append(layer_weights(l, h if l == 0 else saved[l - 1]['f']))
        sv = _layer_forward(l, x_cur, h, weights[l], smalls[l], tabs, col)
        saved.append(sv)
        if l + 1 < depth:
            x_cur, h = _resid_norm_fwd(sv['x1'], sv['f'], smalls[l]['post_mlp_g'], smalls[l + 1]['pre_mix_g'],
                                       f"mlp_out_l{l}")
    d_y, loss_local = _resid_norm_loss(saved[-1]['x1'], saved[-1]['f'], smalls[-1]['post_mlp_g'], target, "loss_head")

    reduce_started = [None] * depth
    g_big = [None] * depth
    grads_small = [dict() for _ in range(depth)]
    d_x2 = d_y
    _, d_f, g_post, _ = _resid_norm_bwd(d_y, None, None, saved[-1]['f'], smalls[-1]['post_mlp_g'], None,
                                        f"mlp_out_bwd_l{depth - 1}")
    grads_small[-1]['post_mlp_g'] = g_post
    grad_x = None
    for l in reversed(range(depth)):
        d_x1, d_h, gb, gs = _layer_backward(l, saved[l], d_f, d_x2, weights[l], smalls[l], tabs, col, nz, nbc)
        grads_small[l].update(gs)
        reduce_started[l] = _reduce_start(l, gb)
        pin = reduce_started[l][4][0, 0]
        if l > 0:
            d_x2, d_f, g_post, g_pre = _resid_norm_bwd(d_x1, d_h, saved[l]['x_in'], saved[l - 1]['f'],
                                                      smalls[l - 1]['post_mlp_g'], smalls[l]['pre_mix_g'] + pin,
                                                      f"mlp_out_bwd_l{l - 1}")
            grads_small[l - 1]['post_mlp_g'] = g_post
            grads_small[l]['pre_mix_g'] = g_pre
        else:
            grad_x, _, _, g_pre = _resid_norm_bwd(d_x1, d_h, saved[0]['x_in'], None, None,
                                                  smalls[0]['pre_mix_g'] + pin, "rms_in_bwd")
            grads_small[0]['pre_mix_g'] = g_pre
        if l + 1 < depth:
            g_big[l + 1] = _reduce_finish(l + 1, reduce_started[l + 1], my_chip, d_h)
    g_big[0] = _reduce_finish(0, reduce_started[0], my_chip, grad_x)

    small_shapes = []
    for n in SMALL_NAMES:
        shp = arg[n].shape[1:]
        if n == 'conv_w':
            shp = (shp[0], shp[1] * 4)
        small_shapes.append((n, shp))
    small_layout, _ = _layout([(f"{n}.{l}", shp) for l in range(depth) for n, shp in small_shapes])

    def small_get(key):
        n, l = key.rsplit('.', 1)
        return grads_small[int(l)][n]

    small_all = _gather_all(_pack(small_layout, small_get, F32), "gather_small_grads")
    g_small = _unpack(small_layout, _sum_slots(small_all, "sum_small_grads"))

    grad, delta, new_m, new_v = {}, {}, {}, {}
    for n in WEIGHT_NAMES:
        if n in BIG_NAMES:
            grad[n], delta[n], new_m[n], new_v[n] = _adamw_big(arg[n], [g_big[l][n] for l in range(depth)],
                                                               arg['m_' + n], arg['v_' + n], f"adamw_{n}")
            continue
        g = jnp.stack([g_small[f"{n}.{l}"] for l in range(depth)])
        if n == 'conv_w':
            wd = conv_w.shape[2]
            g = lax.dynamic_slice_in_dim(g, my_chip * wd, wd, axis=2)
        grad[n] = g
        delta[n], new_m[n], new_v[n] = _adamw(arg[n], g, arg['m_' + n], arg['v_' + n], f"adamw_{n}")

    loss = lax.psum(loss_local, ("x", "y", "c"))
    return (loss, grad_x.reshape(x.shape), *[grad[n] for n in WEIGHT_NAMES], *[delta[n] for n in WEIGHT_NAMES],
            *[new_m[n] for n in WEIGHT_NAMES], *[new_v[n] for n in WEIGHT_NAMES])
```

```python
import math

import jax
import jax.numpy as jnp
from jax import lax
from jax.experimental import pallas as pl
from jax.experimental.pallas import tpu as pltpu

F32 = jnp.float32
BF16 = jnp.bfloat16
MESH = pl.DeviceIdType.MESH

EPS = 1e-6
POOL_WINDOWS = (2, 4, 8, 16)
GROUP = 128
BR = 512
CONV_WIDTH = 31
HEADS = 8
QK_NOPE = 128
QK_ROPE = 64
V_DIM = 128
ROPE_THETA = 10000.0
ATT_SCALE = (QK_NOPE + QK_ROPE) ** -0.5
GELU_C = math.sqrt(2.0 / math.pi)
ADAM_LR, ADAM_B1, ADAM_B2, ADAM_EPS, ADAM_WD, ADAM_STEP = 0.001, 0.9, 0.999, 1e-08, 0.01, 10

VMEM_LIMIT = 48 * 1024 * 1024
PACK_COLS = 1024
PACK_ROW_ALIGN = 16
CONV_HALO = 32
POOL_HALO = 16

TOK_WIDE = 256
TOK_NARROW = 512
ATT_TILE = 512
MM_TILE_M = 1024
MM_TILE_N = 1024
MM_TILE_K = 2048
EW_BLOCK_BYTES = 1536 * 1024
N_CHIPS = 4

WEIGHT_NAMES = ['pre_mix_g', 'w_in', 'pool_w', 'pool_scale', 'pool_proj', 'conv_w', 'conv_b', 'conv_norm_g',
                'conv_norm_b', 'conv_proj', 'sgu_norm_g', 'sgu_norm_b', 'sgu_w', 'sgu_b', 'sgu_proj', 'q_norm_g',
                'w_uq', 'kv_norm_g', 'w_ukv', 'attn_proj', 'w_out', 'post_mix_g', 'pre_mlp_g', 'w_up', 'w_down',
                'post_mlp_g']
BIG = [('w_in', 'col'), ('pool_proj', 'col'), ('conv_proj', 'col'), ('sgu_proj', 'col'), ('w_uq', 'col'),
       ('w_ukv', 'col'), ('attn_proj', 'col'), ('w_out', 'row'), ('w_up', 'col'), ('w_down', 'row')]
BIG_NAMES = [n for n, _ in BIG]
SMALL_NAMES = [n for n in WEIGHT_NAMES if n not in BIG_NAMES]


def _bs(shape, index_map):
    return pl.BlockSpec(shape, index_map)


def _sds(shape, dtype):
    return jax.ShapeDtypeStruct(shape, dtype)


def _tile(n, candidates):
    for t in candidates:
        if n % t == 0:
            return t
    return n


def _div_tile(n, target, align=8):
    t = min(n, target) // align * align
    while t >= align:
        if n % t == 0:
            return t
        t -= align
    return n


def _call(body, *, name, grid, in_specs, out_specs, out_shape, scratch=(), sem=None):
    return pl.pallas_call(
        body, name=name, grid=grid, in_specs=in_specs, out_specs=out_specs, out_shape=out_shape,
        scratch_shapes=list(scratch),
        compiler_params=pltpu.CompilerParams(dimension_semantics=sem, vmem_limit_bytes=VMEM_LIMIT))


def _sigmoid(v):
    return 1.0 / (1.0 + jnp.exp(-v))


def _gelu(v):
    return 0.5 * v * (1.0 + jnp.tanh(GELU_C * (v + 0.044715 * v * v * v)))


def _gelu_grad(v):
    t = jnp.tanh(GELU_C * (v + 0.044715 * v * v * v))
    return 0.5 * (1.0 + t) + 0.5 * v * (1.0 - t * t) * GELU_C * (1.0 + 3.0 * 0.044715 * v * v)


def _rstd(v):
    return lax.rsqrt(jnp.mean(v * v, axis=-1, keepdims=True) + EPS)


def _rms_bwd(v, r, t):
    return r * t - v * (r * r * r) * jnp.mean(v * t, axis=-1, keepdims=True)


def _ln_stats(v):
    mu = jnp.mean(v, axis=-1, keepdims=True)
    d = v - mu
    r = lax.rsqrt(jnp.mean(d * d, axis=-1, keepdims=True) + EPS)
    return d * r, r


def _ln_bwd(xh, r, dxh):
    return r * (dxh - jnp.mean(dxh, axis=-1, keepdims=True) - xh * jnp.mean(dxh * xh, axis=-1, keepdims=True))


def _colsum(v):
    return jnp.sum(v, axis=0, keepdims=True)


def _dot(a, b):
    return jnp.dot(a, b, preferred_element_type=F32)


def _dot_nt(a, b):
    return lax.dot_general(a, b, (((1,), (1,)), ((), ())), preferred_element_type=F32)


def _dot_tn(a, b):
    return lax.dot_general(a, b, (((0,), (0,)), ((), ())), preferred_element_type=F32)


def _mm(a, b, *, name, ta=False, tb=False, b_split=False, out_split=False, out_dtypes=(F32,), epilogue=None,
        extras=()):
    m = a.shape[1] if ta else a.shape[0]
    k = a.shape[0] if ta else a.shape[1]
    shard = b.shape[2] if b_split else None
    b_rows, b_cols = (b.shape[1], N_CHIPS * shard) if b_split else b.shape
    n = b_rows if tb else b_cols
    assert k == (b_cols if tb else b_rows)
    assert not (out_split and extras)
    tm = _div_tile(m, MM_TILE_M, 128)
    tn = _div_tile(n // N_CHIPS if (out_split or (b_split and not tb)) else n, MM_TILE_N, 128)
    tk = _div_tile(shard if (b_split and tb) else k, MM_TILE_K, 128)
    nk = k // tk
    n_extra, n_out = len(extras), len(out_dtypes)
    dims = (((0 if ta else 1,), (1 if tb else 0,)), ((), ()))

    def body(a_ref, b_ref, *rest):
        extra_refs, out_refs = rest[:n_extra], rest[n_extra:n_extra + n_out]
        part = lax.dot_general(a_ref[...].astype(BF16), b_ref[...].astype(BF16), dims, preferred_element_type=F32)

        def finish(res):
            res = (res,) if epilogue is None else epilogue(res, *[e[...] for e in extra_refs])
            for o, r in zip(out_refs, res):
                o[...] = r.astype(o.dtype)

        if nk == 1:
            finish(part)
        else:
            acc, kk = rest[-1], pl.program_id(2)

            @pl.when(kk == 0)
            def _():
                acc[...] = part

            @pl.when(jnp.logical_and(kk > 0, kk < nk - 1))
            def _():
                acc[...] += part

            @pl.when(kk == nk - 1)
            def _():
                finish(acc[...] + part)

    a_spec = _bs((tk, tm), lambda i, j, kk: (kk, i)) if ta else _bs((tm, tk), lambda i, j, kk: (i, kk))
    if not b_split:
        b_spec = _bs((tn, tk), lambda i, j, kk: (j, kk)) if tb else _bs((tk, tn), lambda i, j, kk: (kk, j))
    elif tb:
        kpb = shard // tk
        b_spec = _bs((None, tn, tk), lambda i, j, kk: (kk // kpb, j, kk % kpb))
    else:
        npb = shard // tn
        b_spec = _bs((None, tk, tn), lambda i, j, kk: (j // npb, kk, j % npb))
    e_spec = _bs((tm, tn), lambda i, j, kk: (i, j))
    if out_split:
        npo = (n // N_CHIPS) // tn
        o_spec = _bs((None, tm, tn), lambda i, j, kk: (j // npo, i, j % npo))
        out_shape = [_sds((N_CHIPS, m, n // N_CHIPS), dt) for dt in out_dtypes]
    else:
        o_spec, out_shape = e_spec, [_sds((m, n), dt) for dt in out_dtypes]
    outs = _call(body, name=name, grid=(m // tm, n // tn, nk), in_specs=[a_spec, b_spec] + [e_spec] * n_extra,
                 out_specs=[o_spec] * n_out, out_shape=out_shape,
                 scratch=[pltpu.VMEM((tm, tn), F32)] if nk > 1 else [],
                 sem=("parallel", "parallel", "arbitrary"))(a, b, *extras)
    return outs[0] if n_out == 1 else outs


def _rms_fwd(x, g, name):
    s, d = x.shape
    tt = _tile(s, (TOK_WIDE,))

    def body(x_ref, g_ref, h_ref):
        v = x_ref[...]
        h_ref[...] = (v * _rstd(v) * g_ref[...]).astype(BF16)

    row = _bs((tt, d), lambda i: (i, 0))
    return _call(body, name=name, grid=(s // tt,), in_specs=[row, _bs((1, d), lambda i: (0, 0))],
                 out_specs=row, out_shape=_sds((s, d), BF16), sem=("parallel",))(x, g)


def _resid_norm_fwd(xres, y, g_post, g_next, name):
    s, d = xres.shape
    tt = _tile(s, (TOK_WIDE,))

    def body(xr_ref, y_ref, gp_ref, gn_ref, xn_ref, h_ref):
        yv = y_ref[...]
        xn = xr_ref[...] + yv * _rstd(yv) * gp_ref[...]
        xn_ref[...] = xn
        h_ref[...] = (xn * _rstd(xn) * gn_ref[...]).astype(BF16)

    row, vec = _bs((tt, d), lambda i: (i, 0)), _bs((1, d), lambda i: (0, 0))
    return _call(body, name=name, grid=(s // tt,), in_specs=[row, row, vec, vec], out_specs=[row, row],
                 out_shape=[_sds((s, d), F32), _sds((s, d), BF16)], sem=("parallel",))(xres, y, g_post, g_next)


def _resid_norm_loss(xres, y, g_post, target, name):
    s, d = xres.shape
    tt = _tile(s, (TOK_WIDE,))

    def body(xr_ref, y_ref, gp_ref, t_ref, dy_ref, loss_ref):
        @pl.when(pl.program_id(0) == 0)
        def _():
            loss_ref[...] = jnp.zeros_like(loss_ref)

        yv = y_ref[...]
        err = xr_ref[...] + yv * _rstd(yv) * gp_ref[...] - t_ref[...]
        dy_ref[...] = err * (1.0 / d)
        loss_ref[...] += 0.5 * jnp.sum(jnp.mean(err * err, axis=-1, keepdims=True))

    row, vec = _bs((tt, d), lambda i: (i, 0)), _bs((1, d), lambda i: (0, 0))
    dy, loss = _call(body, name=name, grid=(s // tt,), in_specs=[row, row, vec, row],
                     out_specs=[row, _bs((8, 128), lambda i: (0, 0))],
                     out_shape=[_sds((s, d), F32), _sds((8, 128), F32)], sem=("arbitrary",))(xres, y, g_post, target)
    return dy, loss[0, 0]


def _resid_norm_bwd(d_out, d_h, x_new, y, g_post, g_next, name):
    s, d = d_out.shape
    tt = _tile(s, (TOK_WIDE,))
    has_next, has_y = d_h is not None, y is not None

    def body(*refs):
        it = iter(refs)
        do_ref = next(it)
        if has_next:
            dh_ref, xn_ref, gn_ref = next(it), next(it), next(it)
        if has_y:
            y_ref, gp_ref = next(it), next(it)
        if has_next:
            dx_ref = next(it)
        if has_y:
            dy_ref, dgp_ref = next(it), next(it)
        if has_next:
            dgn_ref = next(it)

        first = pl.program_id(0) == 0
        dx = do_ref[...]
        if has_next:
            xn, dh = xn_ref[...], dh_ref[...]
            r = _rstd(xn)
            dx = dx + _rms_bwd(xn, r, dh * gn_ref[...])
            dx_ref[...] = dx

            @pl.when(first)
            def _():
                dgn_ref[...] = jnp.zeros_like(dgn_ref)

            dgn_ref[...] += _colsum(dh * xn * r)
        if has_y:
            yv = y_ref[...]
            ry = _rstd(yv)
            dy_ref[...] = _rms_bwd(yv, ry, dx * gp_ref[...]).astype(BF16)

            @pl.when(first)
            def _():
                dgp_ref[...] = jnp.zeros_like(dgp_ref)

            dgp_ref[...] += _colsum(dx * yv * ry)

    row, vec = _bs((tt, d), lambda i: (i, 0)), _bs((1, d), lambda i: (0, 0))
    args, in_specs, out_specs, out_shape = [d_out], [row], [], []
    if has_next:
        args += [d_h, x_new, g_next]
        in_specs += [row, row, vec]
    if has_y:
        args += [y, g_post]
        in_specs += [row, vec]
    if has_next:
        out_specs.append(row)
        out_shape.append(_sds((s, d), F32))
    if has_y:
        out_specs += [row, vec]
        out_shape += [_sds((s, d), BF16), _sds((1, d), F32)]
    if has_next:
        out_specs.append(vec)
        out_shape.append(_sds((1, d), F32))
    outs = list(_call(body, name=name, grid=(s // tt,), in_specs=in_specs, out_specs=out_specs, out_shape=out_shape,
                      sem=("arbitrary",))(*args))
    d_x = outs.pop(0) if has_next else None
    d_y, d_gp = (outs.pop(0), outs.pop(0)) if has_y else (None, None)
    d_gn = outs.pop(0) if has_next else None
    return d_x, d_y, d_gp, d_gn


def _pool_counts(t0, tt, w):
    t = t0 + lax.broadcasted_iota(jnp.int32, (tt, 1), 0)
    return jnp.minimum(t + 1, w).astype(F32)


def _pool_pooled(ext, a, t0, tt, g, w):
    cols = pl.ds(g * GROUP, GROUP)
    sm = ext[pl.ds(POOL_HALO, tt), cols]
    for j in range(1, w):
        sm = sm + ext[pl.ds(POOL_HALO - j, tt), cols]
    return sm / _pool_counts(t0, tt, w) - a[:, g * GROUP:(g + 1) * GROUP]


def _pool_fwd(z, cb, pool_w, pool_scale, name):
    s = z.shape[0]
    tt = _tile(s, (TOK_NARROW,))
    hb = tt // POOL_HALO

    def body(zc_ref, zp_ref, pw_ref, sc_ref, out_ref, ext):
        i = pl.program_id(0)
        a = zc_ref[...]
        ext[pl.ds(0, POOL_HALO), :] = jnp.where(i > 0, zp_ref[...], 0.0)
        ext[pl.ds(POOL_HALO, tt), :] = a
        for g, w in enumerate(POOL_WINDOWS):
            pooled = _pool_pooled(ext, a, i * tt, tt, g, w).astype(BF16)
            mixed = _dot(pooled, pw_ref[g])
            out_ref[:, g * GROUP:(g + 1) * GROUP] = (mixed * sc_ref[:, g * GROUP:(g + 1) * GROUP]).astype(BF16)

    return _call(body, name=name, grid=(s // tt,),
                 in_specs=[_bs((tt, BR), lambda i: (i, cb)),
                           _bs((POOL_HALO, BR), lambda i: (jnp.maximum(i * hb - 1, 0), cb)),
                           _bs((len(POOL_WINDOWS), GROUP, GROUP), lambda i: (0, 0, 0)),
                           _bs((1, BR), lambda i: (0, 0))],
                 out_specs=_bs((tt, BR), lambda i: (i, 0)), out_shape=_sds((s, BR), BF16),
                 scratch=[pltpu.VMEM((tt + POOL_HALO, BR), F32)], sem=("parallel",))(z, z, pool_w, pool_scale)


def _pool_bwd(z, cb, d_b, pool_w, pool_scale, name):
    s = z.shape[0]
    tt = _tile(s, (TOK_NARROW,))
    nt, hb, ng = s // tt, tt // POOL_HALO, len(POOL_WINDOWS)

    def body(zc_ref, zp_ref, db_ref, pw_ref, sc_ref, dz_ref, dpw_ref, dsc_ref, ext, ext_e, carry):
        step = pl.program_id(0)
        i = nt - 1 - step

        @pl.when(step == 0)
        def _():
            dpw_ref[...] = jnp.zeros_like(dpw_ref)
            dsc_ref[...] = jnp.zeros_like(dsc_ref)
            carry[...] = jnp.zeros_like(carry)

        a = zc_ref[...]
        ext[pl.ds(0, POOL_HALO), :] = jnp.where(i > 0, zp_ref[...], 0.0)
        ext[pl.ds(POOL_HALO, tt), :] = a
        ext_e[pl.ds(tt, POOL_HALO), :] = carry[...]
        db = db_ref[...]
        for g, w in enumerate(POOL_WINDOWS):
            c0, c1 = g * GROUP, (g + 1) * GROUP
            pooled = _pool_pooled(ext, a, i * tt, tt, g, w).astype(BF16)
            mixed = _dot(pooled, pw_ref[g])
            dsc_ref[:, c0:c1] += _colsum(db[:, c0:c1] * mixed)
            dmixed = (db[:, c0:c1] * sc_ref[:, c0:c1]).astype(BF16)
            dpw_ref[g] += _dot_tn(pooled, dmixed)
            dpooled = _dot_nt(dmixed, pw_ref[g])
            ext_e[pl.ds(0, tt), pl.ds(c0, GROUP)] = dpooled / _pool_counts(i * tt, tt, w)
            acc = -dpooled
            for j in range(w):
                acc = acc + ext_e[pl.ds(j, tt), pl.ds(c0, GROUP)]
            dz_ref[:, c0:c1] = acc.astype(BF16)
        carry[...] = ext_e[pl.ds(0, POOL_HALO), :]

    rev = lambda st: nt - 1 - st
    dz, dpw, dsc = _call(
        body, name=name, grid=(nt,),
        in_specs=[_bs((tt, BR), lambda st: (rev(st), cb)),
                  _bs((POOL_HALO, BR), lambda st: (jnp.maximum(rev(st) * hb - 1, 0), cb)),
                  _bs((tt, BR), lambda st: (rev(st), 0)),
                  _bs((ng, GROUP, GROUP), lambda st: (0, 0, 0)),
                  _bs((1, BR), lambda st: (0, 0))],
        out_specs=[_bs((tt, BR), lambda st: (rev(st), 0)), _bs((ng, GROUP, GROUP), lambda st: (0, 0, 0)),
                   _bs((1, BR), lambda st: (0, 0))],
        out_shape=[_sds((s, BR), BF16), _sds((ng, GROUP, GROUP), F32), _sds((1, BR), F32)],
        scratch=[pltpu.VMEM((tt + POOL_HALO, BR), F32), pltpu.VMEM((tt + POOL_HALO, BR), F32),
                 pltpu.VMEM((POOL_HALO, BR), F32)],
        sem=("arbitrary",))(z, z, d_b, pool_w, pool_scale)
    return dz, dpw, dsc


def _conv_fwd(z, cb, conv_w, conv_b, ng, nb, name):
    s = z.shape[0]
    tt = _tile(s, (TOK_NARROW,))
    hb = tt // CONV_HALO

    def body(a_ref, g_ref, ap_ref, gp_ref, w_ref, b_ref, ng_ref, nb_ref, out_ref, ypre_ref, ext):
        i = pl.program_id(0)
        ext[pl.ds(0, CONV_HALO), :] = jnp.where(i > 0, ap_ref[...] * _sigmoid(gp_ref[...]), 0.0)
        ext[pl.ds(CONV_HALO, tt), :] = a_ref[...] * _sigmoid(g_ref[...])
        acc = jnp.zeros((tt, BR), F32)
        for k in range(CONV_WIDTH):
            acc = acc + w_ref[pl.ds(k, 1), :] * ext[pl.ds(CONV_HALO - (CONV_WIDTH - 1) + k, tt), :]
        ypre = acc + b_ref[...]
        ypre_ref[...] = ypre
        xh, _ = _ln_stats(ypre)
        yl = xh * ng_ref[...] + nb_ref[...]
        out_ref[...] = (yl * _sigmoid(yl)).astype(BF16)

    cur = lambda c: _bs((tt, BR), lambda i: (i, c))
    prev = lambda c: _bs((CONV_HALO, BR), lambda i: (jnp.maximum(i * hb - 1, 0), c))
    vec = _bs((1, BR), lambda i: (0, 0))
    row = _bs((tt, BR), lambda i: (i, 0))
    return _call(body, name=name, grid=(s // tt,),
                 in_specs=[cur(cb), cur(cb + 1), prev(cb), prev(cb + 1),
                           _bs((CONV_WIDTH, BR), lambda i: (0, 0)), vec, vec, vec],
                 out_specs=[row, row], out_shape=[_sds((s, BR), BF16), _sds((s, BR), F32)],
                 scratch=[pltpu.VMEM((tt + CONV_HALO, BR), F32)], sem=("parallel",))(
        z, z, z, z, conv_w, conv_b, ng, nb)


def _conv_bwd(z, cb, d_b, ypre, conv_w, ng, nb, name):
    s = z.shape[0]
    tt = _tile(s, (TOK_NARROW,))
    nt, hb = s // tt, tt // CONV_HALO
    lead = CONV_HALO - (CONV_WIDTH - 1)

    def body(a_ref, g_ref, ap_ref, gp_ref, db_ref, yp_ref, w_ref, ng_ref, nb_ref,
             da_ref, dg_ref, dw_ref, dcb_ref, dng_ref, dnb_ref, ext, ext_d, carry):
        step = pl.program_id(0)
        i = nt - 1 - step

        @pl.when(step == 0)
        def _():
            dw_ref[...] = jnp.zeros_like(dw_ref)
            dcb_ref[...] = jnp.zeros_like(dcb_ref)
            dng_ref[...] = jnp.zeros_like(dng_ref)
            dnb_ref[...] = jnp.zeros_like(dnb_ref)
            carry[...] = jnp.zeros_like(carry)

        xh, r = _ln_stats(yp_ref[...])
        yl = xh * ng_ref[...] + nb_ref[...]
        sg = _sigmoid(yl)
        dyl = db_ref[...] * (sg * (1.0 + yl * (1.0 - sg)))
        dng_ref[...] += _colsum(dyl * xh)
        dnb_ref[...] += _colsum(dyl)
        dypre = _ln_bwd(xh, r, dyl * ng_ref[...])
        dcb_ref[...] += _colsum(dypre)

        a, gate = a_ref[...], g_ref[...]
        sgate = _sigmoid(gate)
        ext[pl.ds(0, CONV_HALO), :] = jnp.where(i > 0, ap_ref[...] * _sigmoid(gp_ref[...]), 0.0)
        ext[pl.ds(CONV_HALO, tt), :] = a * sgate
        ext_d[pl.ds(0, tt), :] = dypre
        ext_d[pl.ds(tt, CONV_HALO), :] = carry[...]
        dglu = jnp.zeros((tt, BR), F32)
        for k in range(CONV_WIDTH):
            dw_ref[pl.ds(k, 1), :] += _colsum(dypre * ext[pl.ds(lead + k, tt), :])
            dglu = dglu + w_ref[pl.ds(k, 1), :] * ext_d[pl.ds(CONV_WIDTH - 1 - k, tt), :]
        carry[...] = ext_d[pl.ds(0, CONV_HALO), :]
        da_ref[...] = (dglu * sgate).astype(BF16)
        dg_ref[...] = (dglu * a * sgate * (1.0 - sgate)).astype(BF16)

    rev = lambda st: nt - 1 - st
    cur = lambda c: _bs((tt, BR), lambda st: (rev(st), c))
    prev = lambda c: _bs((CONV_HALO, BR), lambda st: (jnp.maximum(rev(st) * hb - 1, 0), c))
    vec = _bs((1, BR), lambda st: (0, 0))
    row = _bs((tt, BR), lambda st: (rev(st), 0))
    wsp = _bs((CONV_WIDTH, BR), lambda st: (0, 0))
    return _call(body, name=name, grid=(nt,),
                 in_specs=[cur(cb), cur(cb + 1), prev(cb), prev(cb + 1), row, row, wsp, vec, vec],
                 out_specs=[row, row, wsp, vec, vec, vec],
                 out_shape=[_sds((s, BR), BF16), _sds((s, BR), BF16), _sds((CONV_WIDTH, BR), F32),
                            _sds((1, BR), F32), _sds((1, BR), F32), _sds((1, BR), F32)],
                 scratch=[pltpu.VMEM((tt + CONV_HALO, BR), F32), pltpu.VMEM((tt + CONV_HALO, BR), F32),
                          pltpu.VMEM((CONV_HALO, BR), F32)],
                 sem=("arbitrary",))(z, z, z, z, d_b, ypre, conv_w, ng, nb)


def _sgu_fwd(z, cb, ng, nb, w_masked, bias_full, name):
    s = z.shape[0]
    tt = _tile(s, (TOK_NARROW,))
    ngr = BR // GROUP

    def body(u_ref, v_ref, ng_ref, nb_ref, w_ref, bb_ref, out_ref):
        ua = _gelu(u_ref[...])
        xh, _ = _ln_stats(_gelu(v_ref[...]))
        vn = (xh * ng_ref[...] + nb_ref[...]).astype(BF16)
        for n in range(tt // GROUP):
            for g in range(ngr):
                r0, c0 = n * GROUP, g * GROUP
                sp = _dot(w_ref[g], vn[r0:r0 + GROUP, c0:c0 + GROUP]) + bb_ref[g]
                out_ref[r0:r0 + GROUP, c0:c0 + GROUP] = (ua[r0:r0 + GROUP, c0:c0 + GROUP] * sp).astype(BF16)

    vec = _bs((1, BR), lambda i: (0, 0))
    sq = _bs((ngr, GROUP, GROUP), lambda i: (0, 0, 0))
    return _call(body, name=name, grid=(s // tt,),
                 in_specs=[_bs((tt, BR), lambda i: (i, cb)), _bs((tt, BR), lambda i: (i, cb + 1)), vec, vec, sq, sq],
                 out_specs=_bs((tt, BR), lambda i: (i, 0)), out_shape=_sds((s, BR), BF16),
                 sem=("parallel",))(z, z, ng, nb, w_masked, bias_full)


def _sgu_bwd(z, cb, d_b, ng, nb, w_masked, bias_full, name):
    s = z.shape[0]
    tt = _tile(s, (TOK_NARROW,))
    ngr = BR // GROUP

    def body(u_ref, v_ref, db_ref, ng_ref, nb_ref, w_ref, bb_ref,
             du_ref, dv_ref, dw_ref, dbias_ref, dng_ref, dnb_ref, dvn_s):
        @pl.when(pl.program_id(0) == 0)
        def _():
            dw_ref[...] = jnp.zeros_like(dw_ref)
            dbias_ref[...] = jnp.zeros_like(dbias_ref)
            dng_ref[...] = jnp.zeros_like(dng_ref)
            dnb_ref[...] = jnp.zeros_like(dnb_ref)

        u, v, db = u_ref[...], v_ref[...], db_ref[...]
        ua = _gelu(u)
        xh, r = _ln_stats(_gelu(v))
        vn = (xh * ng_ref[...] + nb_ref[...]).astype(BF16)
        for n in range(tt // GROUP):
            for g in range(ngr):
                rows, cols = slice(n * GROUP, (n + 1) * GROUP), slice(g * GROUP, (g + 1) * GROUP)
                vn_c = vn[rows, cols]
                sp = _dot(w_ref[g], vn_c) + bb_ref[g]
                du_ref[rows, cols] = (db[rows, cols] * sp * _gelu_grad(u[rows, cols])).astype(BF16)
                dsp = db[rows, cols] * ua[rows, cols]
                dbias_ref[g] += dsp
                dsp16 = dsp.astype(BF16)
                dw_ref[g] += _dot_nt(dsp16, vn_c)
                dvn_s[rows, cols] = _dot_tn(w_ref[g], dsp16)
        dvn = dvn_s[...]
        dng_ref[...] += _colsum(dvn * xh)
        dnb_ref[...] += _colsum(dvn)
        dv_ref[...] = (_ln_bwd(xh, r, dvn * ng_ref[...]) * _gelu_grad(v)).astype(BF16)

    vec = _bs((1, BR), lambda i: (0, 0))
    sq = _bs((ngr, GROUP, GROUP), lambda i: (0, 0, 0))
    row = _bs((tt, BR), lambda i: (i, 0))
    return _call(body, name=name, grid=(s // tt,),
                 in_specs=[_bs((tt, BR), lambda i: (i, cb)), _bs((tt, BR), lambda i: (i, cb + 1)), row, vec, vec, sq, sq],
                 out_specs=[row, row, sq, sq, vec, vec],
                 out_shape=[_sds((s, BR), BF16), _sds((s, BR), BF16), _sds((ngr, GROUP, GROUP), F32),
                            _sds((ngr, GROUP, GROUP), F32), _sds((1, BR), F32), _sds((1, BR), F32)],
                 scratch=[pltpu.VMEM((tt, BR), F32)], sem=("arbitrary",))(z, z, d_b, ng, nb, w_masked, bias_full)


def _rope(rv, c_t, s1_t, s2_t):
    return rv * c_t + pltpu.roll(rv, 96, 1) * s1_t + pltpu.roll(rv, 32, 1) * s2_t


def _rope_bwd(gv, c_t, s1_t, s2_t):
    return gv * c_t - pltpu.roll(gv, 96, 1) * s1_t - pltpu.roll(gv, 32, 1) * s2_t


def _mla_prep(z, cb_q, cb_kv, cb_kr, qg, kvg, c_t, s1_t, s2_t, name):
    s = z.shape[0]
    tt = _tile(s, (TOK_NARROW,))

    def body(cq_ref, ckv_ref, kr_ref, qg_ref, kvg_ref, c_ref, s1_ref, s2_ref, qn_ref, kvn_ref, krp_ref):
        cq, ckv = cq_ref[...], ckv_ref[...]
        qn_ref[...] = (cq * _rstd(cq) * qg_ref[...]).astype(BF16)
        kvn_ref[...] = (ckv * _rstd(ckv) * kvg_ref[...]).astype(BF16)
        krp_ref[...] = _rope(kr_ref[...], c_ref[...], s1_ref[...], s2_ref[...]).astype(BF16)

    vec = _bs((1, BR), lambda i: (0, 0))
    rp = _bs((tt, 128), lambda i: (i, 0))
    row = _bs((tt, BR), lambda i: (i, 0))
    return _call(body, name=name, grid=(s // tt,),
                 in_specs=[_bs((tt, BR), lambda i: (i, cb_q)), _bs((tt, BR), lambda i: (i, cb_kv)),
                           _bs((tt, 128), lambda i: (i, cb_kr)), vec, vec, rp, rp, rp],
                 out_specs=[row, row, rp], out_shape=[_sds((s, BR), BF16), _sds((s, BR), BF16), _sds((s, 128), BF16)],
                 sem=("parallel",))(z, z, z, qg, kvg, c_t, s1_t, s2_t)


def _mla_prep_bwd(z, cb_q, cb_kv, d_qn, d_kvn, d_krp, qg, kvg, c_t, s1_t, s2_t, name):
    s = z.shape[0]
    tt = _tile(s, (TOK_NARROW,))

    def body(cq_ref, ckv_ref, dqn_ref, dkvn_ref, dkr_ref, qg_ref, kvg_ref, c_ref, s1_ref, s2_ref,
             dcq_ref, dckv_ref, dkro_ref, dqg_ref, dkvg_ref):
        @pl.when(pl.program_id(0) == 0)
        def _():
            dqg_ref[...] = jnp.zeros_like(dqg_ref)
            dkvg_ref[...] = jnp.zeros_like(dkvg_ref)

        cq, ckv, dqn, dkvn = cq_ref[...], ckv_ref[...], dqn_ref[...], dkvn_ref[...]
        rq, rkv = _rstd(cq), _rstd(ckv)
        dcq_ref[...] = _rms_bwd(cq, rq, dqn * qg_ref[...]).astype(BF16)
        dckv_ref[...] = _rms_bwd(ckv, rkv, dkvn * kvg_ref[...]).astype(BF16)
        dqg_ref[...] += _colsum(dqn * cq * rq)
        dkvg_ref[...] += _colsum(dkvn * ckv * rkv)
        dkro_ref[...] = _rope_bwd(dkr_ref[...], c_ref[...], s1_ref[...], s2_ref[...]).astype(BF16)

    vec = _bs((1, BR), lambda i: (0, 0))
    rp = _bs((tt, 128), lambda i: (i, 0))
    row = _bs((tt, BR), lambda i: (i, 0))
    return _call(body, name=name, grid=(s // tt,),
                 in_specs=[_bs((tt, BR), lambda i: (i, cb_q)), _bs((tt, BR), lambda i: (i, cb_kv)),
                           row, row, rp, vec, vec, rp, rp, rp],
                 out_specs=[row, row, rp, vec, vec],
                 out_shape=[_sds((s, BR), BF16), _sds((s, BR), BF16), _sds((s, 128), BF16),
                            _sds((1, BR), F32), _sds((1, BR), F32)],
                 sem=("arbitrary",))(z, z, d_qn, d_kvn, d_krp, qg, kvg, c_t, s1_t, s2_t)


def _q_rope(q, c_t, s1_t, s2_t, name):
    s, n = q.shape
    tt = _tile(s, (TOK_WIDE,))

    def body(q_ref, c_ref, s1_ref, s2_ref, out_ref):
        c_v, s1_v, s2_v = c_ref[...], s1_ref[...], s2_ref[...]
        for h in range(HEADS):
            b0 = h * 256
            out_ref[:, b0:b0 + 128] = q_ref[:, b0:b0 + 128].astype(BF16)
            out_ref[:, b0 + 128:b0 + 256] = _rope(q_ref[:, b0 + 128:b0 + 256], c_v, s1_v, s2_v).astype(BF16)

    rp = _bs((tt, 128), lambda i: (i, 0))
    row = _bs((tt, n), lambda i: (i, 0))
    return _call(body, name=name, grid=(s // tt,), in_specs=[row, rp, rp, rp], out_specs=row,
                 out_shape=_sds((s, n), BF16), sem=("parallel",))(q, c_t, s1_t, s2_t)


def _att_scores(q_ref, kn_ref, kr_ref, diagonal, t):
    q = q_ref[...]
    sc = (_dot_nt(q[:, :128], kn_ref[...]) + _dot_nt(q[:, 128:], kr_ref[...])) * ATT_SCALE
    if not diagonal:
        return sc
    row = lax.broadcasted_iota(jnp.int32, (t, t), 0)
    col = lax.broadcasted_iota(jnp.int32, (t, t), 1)
    return jnp.where(col <= row, sc, -1e30)


def _on_causal_pairs(q_tile, k_tile, step):
    @pl.when(k_tile < q_tile)
    def _():
        step(False)

    @pl.when(k_tile == q_tile)
    def _():
        step(True)


def _flash_fwd(qb, kv, krp, name):
    s = qb.shape[0]
    t = _tile(s, (ATT_TILE,))
    nq = s // t

    def body(q_ref, kn_ref, v_ref, kr_ref, o_ref, lse_ref, m_s, l_s, acc):
        i, j = pl.program_id(1), pl.program_id(2)

        @pl.when(j == 0)
        def _():
            m_s[...] = jnp.full_like(m_s, -1e30)
            l_s[...] = jnp.zeros_like(l_s)
            acc[...] = jnp.zeros_like(acc)

        def step(diagonal):
            sc = _att_scores(q_ref, kn_ref, kr_ref, diagonal, t)
            m_new = jnp.maximum(m_s[...], jnp.max(sc, axis=-1, keepdims=True))
            p = jnp.exp(sc - m_new)
            alpha = jnp.exp(m_s[...] - m_new)
            l_s[...] = alpha * l_s[...] + jnp.sum(p, axis=-1, keepdims=True)
            acc[...] = alpha * acc[...] + _dot(p.astype(BF16), v_ref[...])
            m_s[...] = m_new

        _on_causal_pairs(i, j, step)

        @pl.when(j == i)
        def _():
            o_ref[...] = (acc[...] / l_s[...]).astype(BF16)
            lse_ref[...] = m_s[...] + jnp.log(l_s[...])

    kmap = lambda off: (lambda h, i, j: (jnp.minimum(j, i), 2 * h + off))
    return _call(body, name=name, grid=(HEADS, nq, nq),
                 in_specs=[_bs((t, 256), lambda h, i, j: (i, h)), _bs((t, 128), kmap(0)), _bs((t, 128), kmap(1)),
                           _bs((t, 128), lambda h, i, j: (jnp.minimum(j, i), 0))],
                 out_specs=[_bs((t, 128), lambda h, i, j: (i, h)), _bs((None, t, 1), lambda h, i, j: (h, i, 0))],
                 out_shape=[_sds((s, HEADS * V_DIM), BF16), _sds((HEADS, s, 1), F32)],
                 scratch=[pltpu.VMEM((t, 1), F32), pltpu.VMEM((t, 1), F32), pltpu.VMEM((t, 128), F32)],
                 sem=("parallel", "parallel", "arbitrary"))(qb, kv, kv, krp)


def _flash_bwd_dq(qb, kv, krp, o, d_o, lse, c_t, s1_t, s2_t, name):
    s = qb.shape[0]
    t = _tile(s, (ATT_TILE,))
    nq = s // t

    def body(q_ref, kn_ref, v_ref, kr_ref, o_ref, do_ref, lse_ref, c_ref, s1_ref, s2_ref, dq_ref, dqn_s, dqr_s, dl_s):
        i, j = pl.program_id(1), pl.program_id(2)

        @pl.when(j == 0)
        def _():
            dqn_s[...] = jnp.zeros_like(dqn_s)
            dqr_s[...] = jnp.zeros_like(dqr_s)
            dl_s[...] = jnp.sum(do_ref[...].astype(F32) * o_ref[...].astype(F32), axis=-1, keepdims=True)

        def step(diagonal):
            p = jnp.exp(_att_scores(q_ref, kn_ref, kr_ref, diagonal, t) - lse_ref[...])
            dp = _dot_nt(do_ref[...], v_ref[...])
            ds = (p * (dp - dl_s[...]) * ATT_SCALE).astype(BF16)
            dqn_s[...] += _dot(ds, kn_ref[...])
            dqr_s[...] += _dot(ds, kr_ref[...])

        _on_causal_pairs(i, j, step)

        @pl.when(j == i)
        def _():
            dq_ref[:, :128] = dqn_s[...].astype(BF16)
            dq_ref[:, 128:] = _rope_bwd(dqr_s[...], c_ref[...], s1_ref[...], s2_ref[...]).astype(BF16)

    kmap = lambda off: (lambda h, i, j: (jnp.minimum(j, i), 2 * h + off))
    qrow = _bs((t, 128), lambda h, i, j: (i, h))
    rp = _bs((t, 128), lambda h, i, j: (i, 0))
    return _call(body, name=name, grid=(HEADS, nq, nq),
                 in_specs=[_bs((t, 256), lambda h, i, j: (i, h)), _bs((t, 128), kmap(0)), _bs((t, 128), kmap(1)),
                           _bs((t, 128), lambda h, i, j: (jnp.minimum(j, i), 0)), qrow, qrow,
                           _bs((None, t, 1), lambda h, i, j: (h, i, 0)), rp, rp, rp],
                 out_specs=_bs((t, 256), lambda h, i, j: (i, h)), out_shape=_sds((s, HEADS * 256), BF16),
                 scratch=[pltpu.VMEM((t, 128), F32), pltpu.VMEM((t, 128), F32), pltpu.VMEM((t, 1), F32)],
                 sem=("parallel", "parallel", "arbitrary"))(qb, kv, kv, krp, o, d_o, lse, c_t, s1_t, s2_t)


def _flash_bwd_dkv(qb, kv, krp, o, d_o, lse, name):
    s = qb.shape[0]
    t = _tile(s, (ATT_TILE,))
    nq = s // t

    def body(q_ref, kn_ref, v_ref, kr_ref, o_ref, do_ref, lse_ref, dkv_ref, dkr_ref, dk_s, dv_s, dkr_s):
        j, h, i = pl.program_id(0), pl.program_id(1), pl.program_id(2)

        @pl.when(i == j)
        def _():
            dk_s[...] = jnp.zeros_like(dk_s)
            dv_s[...] = jnp.zeros_like(dv_s)

        @pl.when(jnp.logical_and(i == j, h == 0))
        def _():
            dkr_s[...] = jnp.zeros_like(dkr_s)

        def step(diagonal):
            q, do = q_ref[...], do_ref[...]
            p = jnp.exp(_att_scores(q_ref, kn_ref, kr_ref, diagonal, t) - lse_ref[...])
            delta = jnp.sum(do.astype(F32) * o_ref[...].astype(F32), axis=-1, keepdims=True)
            dv_s[...] += _dot_tn(p.astype(BF16), do)
            ds = (p * (_dot_nt(do, v_ref[...]) - delta) * ATT_SCALE).astype(BF16)
            dk_s[...] += _dot_tn(ds, q[:, :128])
            dkr_s[...] += _dot_tn(ds, q[:, 128:])

        _on_causal_pairs(i, j, step)

        @pl.when(i == nq - 1)
        def _():
            dkv_ref[:, :128] = dk_s[...].astype(BF16)
            dkv_ref[:, 128:] = dv_s[...].astype(BF16)

        @pl.when(jnp.logical_and(i == nq - 1, h == HEADS - 1))
        def _():
            dkr_ref[...] = dkr_s[...]

    qi = lambda j, h, i: jnp.maximum(i, j)
    qrow = _bs((t, 128), lambda j, h, i: (qi(j, h, i), h))
    return _call(body, name=name, grid=(nq, HEADS, nq),
                 in_specs=[_bs((t, 256), lambda j, h, i: (qi(j, h, i), h)),
                           _bs((t, 128), lambda j, h, i: (j, 2 * h)), _bs((t, 128), lambda j, h, i: (j, 2 * h + 1)),
                           _bs((t, 128), lambda j, h, i: (j, 0)), qrow, qrow,
                           _bs((None, t, 1), lambda j, h, i: (h, qi(j, h, i), 0))],
                 out_specs=[_bs((t, 256), lambda j, h, i: (j, h)), _bs((t, 128), lambda j, h, i: (j, 0))],
                 out_shape=[_sds((s, HEADS * 256), BF16), _sds((s, 128), F32)],
                 scratch=[pltpu.VMEM((t, 128), F32), pltpu.VMEM((t, 128), F32), pltpu.VMEM((t, 128), F32)],
                 sem=("parallel", "arbitrary", "arbitrary"))(qb, kv, kv, krp, o, d_o, lse)


def _merge_tiles(s, projs):
    return _tile(s, (TOK_WIDE,)), _div_tile(projs[0].shape[2], 512, 128)


def _merge_specs(s, d, tt, tn):
    nb, npb = d // tn, d // N_CHIPS // tn
    br = lambda w: _bs((tt, w), lambda i, n: (i, 0))
    pw = lambda k: _bs((None, k, tn), lambda i, n: (n // npb, 0, n % npb))
    gate = lambda g: _bs((tt, tn), lambda i, n: (i, g * nb + n))
    return [br(BR), br(BR), br(BR), br(HEADS * V_DIM), pw(BR), pw(BR), pw(BR), pw(HEADS * V_DIM)] + \
           [gate(g) for g in range(4)]


def _merge_fwd(z, branches, projs, name):
    s, d = z.shape[0], N_CHIPS * projs[0].shape[2]
    tt, tn = _merge_tiles(s, projs)

    def body(*refs):
        b_refs, p_refs, g_refs, out_ref = refs[0:4], refs[4:8], refs[8:12], refs[12]
        acc = None
        for b_ref, p_ref, g_ref in zip(b_refs, p_refs, g_refs):
            term = _sigmoid(g_ref[...]) * _dot(b_ref[...], p_ref[...])
            acc = term if acc is None else acc + term
        out_ref[...] = acc.astype(BF16)

    return _call(body, name=name, grid=(s // tt, d // tn), in_specs=_merge_specs(s, d, tt, tn),
                 out_specs=_bs((tt, tn), lambda i, n: (i, n)), out_shape=_sds((s, d), BF16),
                 sem=("parallel", "parallel"))(*branches, *projs, z, z, z, z)


def _merge_bwd(z, branches, projs, d_merged, name):
    s, d = z.shape[0], N_CHIPS * projs[0].shape[2]
    tt, tn = _merge_tiles(s, projs)

    def body(*refs):
        b_refs, p_refs, g_refs, dm_ref = refs[0:4], refs[4:8], refs[8:12], refs[12]
        dy_refs, dg_refs = refs[13:17], refs[17:21]
        dm = dm_ref[...]
        for b_ref, p_ref, g_ref, dy_ref, dg_ref in zip(b_refs, p_refs, g_refs, dy_refs, dg_refs):
            sg = _sigmoid(g_ref[...])
            dy_ref[...] = (dm * sg).astype(BF16)
            dg_ref[...] = (dm * _dot(b_ref[...], p_ref[...]) * sg * (1.0 - sg)).astype(BF16)

    tile = _bs((tt, tn), lambda i, n: (i, n))
    outs = _call(body, name=name, grid=(s // tt, d // tn), in_specs=_merge_specs(s, d, tt, tn) + [tile],
                 out_specs=[tile] * 8, out_shape=[_sds((s, d), BF16)] * 8,
                 sem=("parallel", "parallel"))(*branches, *projs, z, z, z, z, d_merged)
    return outs[:4], outs[4:]


def _ew_rows(rows, cols, align=16):
    lanes = -(-cols // 128) * 128
    return _div_tile(rows, max(EW_BLOCK_BYTES // (lanes * 4), align), align)


def _add_pair(a, b, name):
    n, r, c = a.shape
    tr = _ew_rows(r, c)

    def body(a_ref, b_ref, o_ref):
        o_ref[...] = (a_ref[...].astype(F32) + b_ref[...].astype(F32)).astype(BF16)

    blk = _bs((None, tr, c), lambda k, i: (k, i, 0))
    return _call(body, name=name, grid=(n, r // tr), in_specs=[blk, blk], out_specs=blk,
                 out_shape=_sds((n, r, c), BF16), sem=("parallel", "parallel"))(a, b)


def _sum_chips(land, own, name):
    n, r, c = land.shape
    tr = _ew_rows(r, c)

    def body(l_ref, o_ref, out_ref):
        acc = l_ref[0].astype(F32)
        for k in range(1, n):
            acc = acc + l_ref[k].astype(F32)
        out_ref[...] = acc + o_ref[...].astype(F32)

    blk = _bs((tr, c), lambda i: (i, 0))
    return _call(body, name=name, grid=(r // tr,), in_specs=[_bs((n, tr, c), lambda i: (0, i, 0)), blk],
                 out_specs=blk, out_shape=_sds((r, c), F32), sem=("parallel",))(land, own)


def _sum_slots(buf, name):
    n, r, c = buf.shape
    tr = _div_tile(r, max(EW_BLOCK_BYTES // (c * 4 * n), 8), 8)

    def body(b_ref, o_ref):
        acc = b_ref[0].astype(F32)
        for k in range(1, n):
            acc = acc + b_ref[k].astype(F32)
        o_ref[...] = acc

    return _call(body, name=name, grid=(r // tr,), in_specs=[_bs((n, tr, c), lambda i: (0, i, 0))],
                 out_specs=_bs((tr, c), lambda i: (i, 0)), out_shape=_sds((r, c), F32), sem=("parallel",))(buf)


def _adam_update(w, g, m, v):
    nm = ADAM_B1 * m + (1.0 - ADAM_B1) * g
    nv = ADAM_B2 * v + (1.0 - ADAM_B2) * jnp.square(g)
    m_hat = nm / (1.0 - ADAM_B1 ** ADAM_STEP)
    v_hat = nv / (1.0 - ADAM_B2 ** ADAM_STEP)
    return -ADAM_LR * (m_hat / (jnp.sqrt(v_hat) + ADAM_EPS) + ADAM_WD * w), nm, nv


def _adamw(w, g, m, v, name):
    shape = w.shape
    cols = shape[-1]
    rows = 1
    for dim in shape[:-1]:
        rows *= dim
    w2, g2, m2, v2 = (t.reshape(rows, cols) for t in (w, g, m, v))
    tr = _ew_rows(rows, cols, align=8)

    def body(w_ref, g_ref, m_ref, v_ref, d_ref, nm_ref, nv_ref):
        d_ref[...], nm_ref[...], nv_ref[...] = _adam_update(w_ref[...], g_ref[...], m_ref[...], v_ref[...])

    blk = _bs((tr, cols), lambda i: (i, 0))
    outs = _call(body, name=name, grid=(rows // tr,), in_specs=[blk] * 4, out_specs=[blk] * 3,
                 out_shape=[_sds((rows, cols), F32)] * 3, sem=("parallel",))(w2, g2, m2, v2)
    return [o.reshape(shape) for o in outs]


def _adamw_big(w, mine, other, m, v, name):
    depth, r, c = w.shape
    tr = _ew_rows(r // 2, c, align=8)
    hb = (r // 2) // tr

    def body(w_ref, m_ref, v_ref, *rest):
        mine_refs, other_refs = rest[:depth], rest[depth:2 * depth]
        g_out, d_ref, nm_ref, nv_ref = rest[2 * depth:]
        layer, blk_i = pl.program_id(0), pl.program_id(1)
        is_mine = (blk_i // hb) == lax.axis_index("c")
        gv = jnp.where(is_mine, mine_refs[0][...], other_refs[0][...])
        for k in range(1, depth):
            gv = jnp.where(layer == k, jnp.where(is_mine, mine_refs[k][...], other_refs[k][...]), gv)
        g_out[...] = gv
        d_ref[...], nm_ref[...], nv_ref[...] = _adam_update(w_ref[...], gv, m_ref[...], v_ref[...])

    blk = _bs((None, tr, c), lambda l, i: (l, i, 0))
    g_spec = lambda k: _bs((tr, c), lambda l, i: (jnp.where(l == k, i % hb, jnp.where(l < k, 0, hb - 1)), 0))
    return _call(body, name=name, grid=(depth, 2 * hb),
                 in_specs=[blk] * 3 + [g_spec(k) for k in range(depth)] * 2,
                 out_specs=[blk] * 4, out_shape=[_sds((depth, r, c), F32)] * 4,
                 sem=("arbitrary", "arbitrary"))(w, m, v, *mine, *other)


ANY = pl.BlockSpec(memory_space=pl.ANY)
HBM = pl.BlockSpec(memory_space=pltpu.HBM)
SEM = pl.BlockSpec(memory_space=pltpu.SEMAPHORE)
DATAFLOW = pltpu.SideEffectType.DATAFLOW_SIDE_EFFECTING
DMA_SEMS = pltpu.SemaphoreType.DMA


def _place():
    return lax.axis_index("x"), lax.axis_index("y"), lax.axis_index("c")


def _other_chips(x, y):
    return [(1 - x, y), (x, 1 - y), (1 - x, 1 - y)]


def _half_rows(rows, c):
    half = rows // 2
    assert half % 16 == 0
    return pl.ds(pl.multiple_of(c * half, 16), half)


def _in_hbm(a):
    return pltpu.with_memory_space_constraint(a, pltpu.HBM)


def _ici_copies(mode, src, land, send_sems, recv_sems):
    x, y, c = _place()
    my = 2 * x + y
    out = []
    for i in range(len(src)):
        for j, chip in enumerate(_other_chips(x, y)):
            peer = 2 * chip[0] + chip[1]
            if mode == 'gather':
                rows = _half_rows(src[i].shape[0], c)
                s_ref, d_send, d_recv = src[i].at[rows], land[i].at[my, rows], land[i].at[peer, rows]
            else:
                s_ref, d_send, d_recv = src[i].at[peer], land[i].at[j], land[i].at[j]
            pair = [pltpu.make_async_remote_copy(src_ref=s_ref, dst_ref=dst, send_sem=send_sems.at[3 * i + j],
                                                 recv_sem=recv_sems.at[3 * i + j], device_id=(chip[0], chip[1], c),
                                                 device_id_type=MESH) for dst in (d_send, d_recv)]
            out.append(pair)
    return out


def _ici_start(mode, srcs, land_shapes, name):
    n = len(srcs)
    lands = [_in_hbm(lax.empty(shp, s.dtype)) for shp, s in zip(land_shapes, srcs)]

    def body(*refs):
        src, land, send_sems, recv_sems, token = refs[:n], refs[n:2 * n], refs[2 * n], refs[2 * n + 1], refs[-1]
        for send, _ in _ici_copies(mode, src, land, send_sems, recv_sems):
            send.start()
        token[...] = jnp.zeros_like(token)

    outs = pl.pallas_call(
        body, name=name,
        out_shape=(DMA_SEMS((3 * n,)), DMA_SEMS((3 * n,)), *[pltpu.HBM(s.shape, s.dtype) for s in srcs],
                   *[pltpu.HBM(l.shape, l.dtype) for l in lands], _sds((8, 128), F32)),
        in_specs=[HBM] * (2 * n), out_specs=(SEM, SEM, *[HBM] * (2 * n), pl.BlockSpec(memory_space=pltpu.VMEM)),
        input_output_aliases={i: 2 + i for i in range(2 * n)},
        compiler_params=pltpu.CompilerParams(has_side_effects=DATAFLOW))(*[_in_hbm(s) for s in srcs], *lands)
    return outs[0], outs[1], list(outs[2:2 + n]), list(outs[2 + n:2 + 2 * n]), outs[-1]


def _ici_wait(mode, started, after, name):
    send_sems, recv_sems, src_thru, land_thru, _ = started
    n = len(src_thru)

    def body(*refs):
        src, land, send_s, recv_s = refs[:n], refs[n:2 * n], refs[2 * n], refs[2 * n + 1]
        for send, recv in _ici_copies(mode, src, land, send_s, recv_s):
            send.wait_send()
            recv.wait_recv()

    outs = pl.pallas_call(
        body, name=name, out_shape=tuple(pltpu.HBM(t.shape, t.dtype) for t in src_thru + land_thru),
        in_specs=[HBM] * (2 * n) + [SEM, SEM, ANY], out_specs=(HBM,) * (2 * n),
        input_output_aliases={i: i for i in range(2 * n)},
        compiler_params=pltpu.CompilerParams(has_side_effects=DATAFLOW))(*src_thru, *land_thru, send_sems, recv_sems,
                                                                        after)
    return list(outs[n:])


def _sibling_call(body, name, inputs, out_shape, n_copies, n_local, aliases=None):
    return pl.pallas_call(body, name=name, out_shape=out_shape, in_specs=[ANY] * len(inputs),
                          out_specs=[ANY] * len(out_shape), input_output_aliases=aliases or {},
                          scratch_shapes=[DMA_SEMS((n_copies,)), DMA_SEMS((n_copies,)), DMA_SEMS((n_local,))])(*inputs)


def _share_weights(srcs, lands, name):
    n = len(srcs)

    def body(*refs):
        src, land_in, land = refs[:n], refs[n:2 * n], refs[2 * n:3 * n]
        send_sems, recv_sems, _ = refs[3 * n:3 * n + 3]
        x, y, c = _place()
        my, sib = 2 * x + y, (x, y, 1 - c)

        def d2d(k, from_ref, to_ref):
            return pltpu.make_async_remote_copy(src_ref=from_ref, dst_ref=to_ref, send_sem=send_sems.at[k],
                                                recv_sem=recv_sems.at[k], device_id=sib, device_id_type=MESH)

        sends, arrivals = [], []
        for i in range(n):
            mine, theirs = _half_rows(src[i].shape[0], c), _half_rows(src[i].shape[0], 1 - c)
            for j, chip in enumerate(_other_chips(x, y)):
                peer = 2 * chip[0] + chip[1]
                sends.append(d2d(4 * i + j, land_in[i].at[peer, mine], land[i].at[peer, mine]))
                arrivals.append(d2d(4 * i + j, land_in[i].at[peer, theirs], land[i].at[peer, theirs]))
            sends.append(d2d(4 * i + 3, src[i], land[i].at[my]))
            arrivals.append(d2d(4 * i + 3, src[i], land[i].at[my]))
        for cp in sends:
            cp.start()
        for cp in arrivals:
            cp.wait_recv()
        for cp in sends:
            cp.wait_send()

    return list(_sibling_call(body, name, list(srcs) + list(lands), [_sds(l.shape, l.dtype) for l in lands], 4 * n, 1,
                              aliases={n + i: i for i in range(n)}))


def _swap_grad_halves(gs, name):
    n = len(gs)

    def body(*refs):
        g, got = refs[:n], refs[n:2 * n]
        send_sems, recv_sems, _ = refs[2 * n:2 * n + 3]
        x, y, c = _place()
        sends = [pltpu.make_async_remote_copy(src_ref=g[i].at[:, _half_rows(g[i].shape[1], 1 - c)], dst_ref=got[i],
                                              send_sem=send_sems.at[i], recv_sem=recv_sems.at[i],
                                              device_id=(x, y, 1 - c), device_id_type=MESH) for i in range(n)]
        for cp in sends:
            cp.start()
        for cp in sends:
            cp.wait()

    half = [_sds((g.shape[0], g.shape[1] // 2, g.shape[2]), g.dtype) for g in gs]
    return list(_sibling_call(body, name, list(gs), half, n, 1))


def _share_final(fins, name):
    n = len(fins)

    def body(*refs):
        fin, other = refs[:n], refs[n:2 * n]
        send_sems, recv_sems, _ = refs[2 * n:2 * n + 3]
        x, y, c = _place()
        sends = [pltpu.make_async_remote_copy(src_ref=fin[i], dst_ref=other[i], send_sem=send_sems.at[i],
                                              recv_sem=recv_sems.at[i], device_id=(x, y, 1 - c), device_id_type=MESH)
                 for i in range(n)]
        for cp in sends:
            cp.start()
        for cp in sends:
            cp.wait()

    return list(_sibling_call(body, name, list(fins), [_sds(f.shape, f.dtype) for f in fins], n, 1))


def _gather_all(buf, name):
    flips = [(fx, fy, fc) for fx in (0, 1) for fy in (0, 1) for fc in (0, 1) if (fx, fy, fc) != (0, 0, 0)]

    def body(src, dst, send_sems, recv_sems, loc_sem):
        x, y, c = _place()
        me = 4 * x + 2 * y + c
        loc = pltpu.make_async_copy(src, dst.at[me], loc_sem.at[0])
        loc.start()
        peers = [(1 - x if fx else x, 1 - y if fy else y, 1 - c if fc else c) for fx, fy, fc in flips]

        def cp(j, peer, slot):
            return pltpu.make_async_remote_copy(src_ref=src, dst_ref=dst.at[slot], send_sem=send_sems.at[j],
                                                recv_sem=recv_sems.at[j], device_id=peer, device_id_type=MESH)

        sends = [cp(j, peer, me) for j, peer in enumerate(peers)]
        for s_ in sends:
            s_.start()
        for j, peer in enumerate(peers):
            cp(j, peer, 4 * peer[0] + 2 * peer[1] + peer[2]).wait_recv()
        for s_ in sends:
            s_.wait_send()
        loc.wait()

    return _sibling_call(body, name, [buf], [_sds((8,) + buf.shape, buf.dtype)], 7, 1)[0]


def _layout(shapes):
    out, r0 = [], 0
    for name, shape in shapes:
        n = 1
        for dim in shape:
            n *= dim
        nr = -(-n // PACK_COLS)
        nr = -(-nr // PACK_ROW_ALIGN) * PACK_ROW_ALIGN
        out.append((name, tuple(shape), r0, nr))
        r0 += nr
    return out, r0


def _pack(layout, get, dtype):
    parts = []
    for name, shape, _, nr in layout:
        flat = get(name).astype(dtype).reshape(-1)
        pad = nr * PACK_COLS - flat.shape[0]
        if pad:
            flat = jnp.pad(flat, (0, pad))
        parts.append(flat.reshape(nr, PACK_COLS))
    return jnp.concatenate(parts, axis=0)


def _unpack(layout, buf):
    out = {}
    for name, shape, r0, nr in layout:
        n = 1
        for dim in shape:
            n *= dim
        out[name] = buf[r0:r0 + nr].reshape(-1)[:n].reshape(shape)
    return out


def _rope_tables(positions):
    inv_freq = ROPE_THETA ** (-jnp.arange(0, QK_ROPE, 2, dtype=F32) / QK_ROPE)
    ang = positions.astype(F32)[:, None] * inv_freq
    cos, sin, zero = jnp.cos(ang), jnp.sin(ang), jnp.zeros_like(ang)
    c_t = jnp.concatenate([cos, cos, zero, zero], axis=1)
    s1_t = jnp.concatenate([-sin, zero, zero, zero], axis=1)
    s2_t = jnp.concatenate([zero, sin, zero, zero], axis=1)
    return c_t, s1_t, s2_t


def _shard_cols(shards, lo, hi):
    ws = shards.shape[2]
    out = []
    for k in range(shards.shape[0]):
        a, b = max(lo, k * ws), min(hi, (k + 1) * ws)
        if a < b:
            out.append(shards[k][:, a - k * ws:b - k * ws])
    return out


def _w_in_to_kernel_layout(shards, nbc, nz):
    d, n_in = shards.shape[1], N_CHIPS * shards.shape[2]
    return jnp.concatenate(_shard_cols(shards, nbc, n_in) + _shard_cols(shards, 0, nbc)
                           + [jnp.zeros((d, nz - n_in), shards.dtype)], axis=1)


def _w_in_grad_to_shards(gx, nbc, n_in):
    ws, n_gate = n_in // N_CHIPS, n_in - nbc
    out = []
    for k in range(N_CHIPS):
        lo, hi, parts = k * ws, (k + 1) * ws, []
        if lo < nbc:
            parts.append(gx[:, n_gate + lo:n_gate + min(hi, nbc)])
        if hi > nbc:
            parts.append(gx[:, max(lo, nbc) - nbc:hi - nbc])
        out.append(parts[0] if len(parts) == 1 else jnp.concatenate(parts, axis=1))
    return jnp.stack(out)


def _w_uq_to_kernel_layout(shards):
    full = jnp.concatenate([shards[k] for k in range(N_CHIPS)], axis=1).reshape(-1, HEADS, QK_NOPE + QK_ROPE)
    pad = jnp.zeros((full.shape[0], HEADS, 256 - QK_NOPE - QK_ROPE), full.dtype)
    return jnp.concatenate([full, pad], axis=2).reshape(-1, HEADS * 256)


def _w_uq_grad_to_shards(gx):
    full = gx.reshape(-1, HEADS, 256)[:, :, :QK_NOPE + QK_ROPE].reshape(gx.shape[0], -1)
    ws = full.shape[1] // N_CHIPS
    return jnp.stack([full[:, k * ws:(k + 1) * ws] for k in range(N_CHIPS)])


def _layer_forward(l, x_in, h, w_in_x, other_weights, sm, tabs, col):
    c_t, s1_t, s2_t = tabs
    tag = f"l{l}"
    z = _mm(h, w_in_x, name=f"z_{tag}")
    z, w = other_weights(z)
    w = dict(w, w_in_x=w_in_x)
    b_pool = _pool_fwd(z, col['pool'], sm['pool_w16'], sm['pool_scale'], f"pool_fwd_{tag}")
    b_conv, ypre = _conv_fwd(z, col['conv'], sm['conv_w'], sm['conv_b'], sm['conv_norm_g'], sm['conv_norm_b'],
                             f"conv_fwd_{tag}")
    b_sgu = _sgu_fwd(z, col['sgu'], sm['sgu_norm_g'], sm['sgu_norm_b'], sm['sgu_w16'], sm['sgu_bias_full'],
                     f"sgu_fwd_{tag}")
    qn, kvn, krp = _mla_prep(z, col['q'], col['kv'], col['kr'], sm['q_norm_g'], sm['kv_norm_g'], c_t, s1_t, s2_t,
                             f"mla_prep_{tag}")
    q = _mm(qn, w['w_uq_x'], name=f"q_{tag}")
    qb = _q_rope(q, c_t, s1_t, s2_t, f"q_rope_{tag}")
    kv = _mm(kvn, w['w_ukv'], name=f"kv_{tag}", b_split=True, out_dtypes=(BF16,))
    o, lse = _flash_fwd(qb, kv, krp, f"flash_fwd_{tag}")
    branches = (b_pool, b_conv, b_sgu, o)
    projs = (w['pool_proj'], w['conv_proj'], w['sgu_proj'], w['attn_proj'])
    merged = _merge_fwd(z, branches, projs, f"merge_fwd_{tag}")
    o2 = _mm(merged, w['w_out'], name=f"o2_{tag}")
    x1, h2 = _resid_norm_fwd(x_in, o2, sm['post_mix_g'], sm['pre_mlp_g'], f"mix_out_{tag}")
    u, a = _mm(h2, w['w_up'], name=f"up_{tag}", b_split=True, out_dtypes=(F32, BF16),
               epilogue=lambda acc: (acc, jnp.square(jnp.maximum(acc, 0.0))))
    f = _mm(a, w['w_down'], name=f"down_{tag}")
    return dict(x_in=x_in, h=h, z=z, ypre=ypre, branches=branches, projs=projs, qn=qn, kvn=kvn, krp=krp, qb=qb, kv=kv,
                o=o, lse=lse, merged=merged, o2=o2, x1=x1, h2=h2, u=u, a=a, f=f), w


def _layer_backward(l, sv, d_f, d_x2, w, sm, tabs, col, nz, nbc, early_reduce):
    c_t, s1_t, s2_t = tabs
    tag = f"l{l}"
    s, d = sv['x_in'].shape
    gb, gs = {}, {}
    row_shards = lambda g: g.reshape(N_CHIPS, g.shape[0] // N_CHIPS, g.shape[1])
    d_u = _mm(d_f, w['w_down'], name=f"d_u_{tag}", tb=True, out_dtypes=(BF16,), extras=(sv['u'],),
              epilogue=lambda acc, u: (acc * (2.0 * jnp.maximum(u, 0.0)),))
    gb['w_down'] = row_shards(_mm(sv['a'], d_f, name=f"g_down_{tag}", ta=True, out_dtypes=(BF16,)))
    d_h2 = _mm(d_u, w['w_up'], name=f"d_h2_{tag}", tb=True, b_split=True)
    gb['w_up'] = _mm(sv['h2'], d_u, name=f"g_up_{tag}", ta=True, out_split=True, out_dtypes=(BF16,))
    d_x1, d_o2, gs['post_mix_g'], gs['pre_mlp_g'] = _resid_norm_bwd(
        d_x2, d_h2, sv['x1'], sv['o2'], sm['post_mix_g'], sm['pre_mlp_g'], f"mix_out_bwd_{tag}")
    d_merged = _mm(d_o2, w['w_out'], name=f"d_merged_{tag}", tb=True)
    gb['w_out'] = row_shards(_mm(sv['merged'], d_o2, name=f"g_out_{tag}", ta=True, out_dtypes=(BF16,)))
    d_ys, d_gates = _merge_bwd(sv['z'], sv['branches'], sv['projs'], d_merged, f"merge_bwd_{tag}")
    d_br = []
    for k, pname in enumerate(('pool_proj', 'conv_proj', 'sgu_proj', 'attn_proj')):
        last = pname == 'attn_proj'
        d_br.append(_mm(d_ys[k], w[pname], name=f"d_{pname}_in_{tag}", tb=True, b_split=True,
                        out_dtypes=(BF16 if last else F32,)))
        gb[pname] = _mm(sv['branches'][k], d_ys[k], name=f"g_{pname}_{tag}", ta=True, out_split=True,
                        out_dtypes=(BF16,))
    dz_pool, gs['pool_w'], gs['pool_scale'] = _pool_bwd(sv['z'], col['pool'], d_br[0], sm['pool_w16'],
                                                        sm['pool_scale'], f"pool_bwd_{tag}")
    dz_ca, dz_cg, gs['conv_w'], gs['conv_b'], gs['conv_norm_g'], gs['conv_norm_b'] = _conv_bwd(
        sv['z'], col['conv'], d_br[1], sv['ypre'], sm['conv_w'], sm['conv_norm_g'], sm['conv_norm_b'],
        f"conv_bwd_{tag}")
    dz_su, dz_sv, g_sgu_w, g_sgu_bfull, gs['sgu_norm_g'], gs['sgu_norm_b'] = _sgu_bwd(
        sv['z'], col['sgu'], d_br[2], sm['sgu_norm_g'], sm['sgu_norm_b'], sm['sgu_w16'], sm['sgu_bias_full'],
        f"sgu_bwd_{tag}")
    gs['sgu_w'] = g_sgu_w * sm['tril']
    gs['sgu_b'] = jnp.sum(g_sgu_bfull, axis=-1)
    d_qb = _flash_bwd_dq(sv['qb'], sv['kv'], sv['krp'], sv['o'], d_br[3], sv['lse'], c_t, s1_t, s2_t,
                         f"flash_dq_{tag}")
    d_kv, d_krp = _flash_bwd_dkv(sv['qb'], sv['kv'], sv['krp'], sv['o'], d_br[3], sv['lse'], f"flash_dkv_{tag}")
    d_qn = _mm(d_qb, w['w_uq_x'], name=f"d_qn_{tag}", tb=True)
    gb['w_uq'] = _w_uq_grad_to_shards(_mm(sv['qn'], d_qb, name=f"g_uq_{tag}", ta=True, out_dtypes=(BF16,)))
    d_kvn = _mm(d_kv, w['w_ukv'], name=f"d_kvn_{tag}", tb=True, b_split=True)
    gb['w_ukv'] = _mm(sv['kvn'], d_kv, name=f"g_ukv_{tag}", ta=True, out_split=True, out_dtypes=(BF16,))
    dz_q, dz_kv, dz_kr, gs['q_norm_g'], gs['kv_norm_g'] = _mla_prep_bwd(
        sv['z'], col['q'], col['kv'], d_qn, d_kvn, d_krp, sm['q_norm_g'] + early_reduce(gb), sm['kv_norm_g'],
        c_t, s1_t, s2_t, f"mla_prep_bwd_{tag}")
    used = 4 * d + 7 * BR + 128
    dz = jnp.concatenate(list(d_gates) + [dz_pool, dz_ca, dz_cg, dz_su, dz_sv, dz_q, dz_kv, dz_kr,
                                          jnp.zeros((s, nz - used), BF16)], axis=1)
    d_h = _mm(dz, w['w_in_x'], name=f"d_h_{tag}", tb=True)
    gb['w_in'] = _w_in_grad_to_shards(_mm(sv['h'], dz, name=f"g_in_{tag}", ta=True, out_dtypes=(BF16,)),
                                      nbc, 4 * d + nbc)
    return d_x1, d_h, gb, gs


def _reduce_start(tag, names, gb, my_c):
    gs = [gb[n] for n in names]
    sib4 = _swap_grad_halves(gs, f"swap_grads_{tag}")
    own4 = [lax.dynamic_slice_in_dim(g, my_c * (g.shape[1] // 2), g.shape[1] // 2, axis=1) for g in gs]
    partial = [_add_pair(o, g, f"add_sibling_{n}_{tag}") for n, o, g in zip(names, own4, sib4)]
    return _ici_start('scatter', partial, [(3,) + p.shape[1:] for p in partial], f"scatter_start_{tag}")


def _reduce_finish(tag, names, started, my_chip, after):
    own = [lax.dynamic_index_in_dim(p, my_chip, 0, keepdims=False) for p in started[2]]
    lands = _ici_wait('scatter', started, after, f"scatter_wait_{tag}")
    fins = [_sum_chips(ld, ow, f"sum_chips_{n}_{tag}") for n, ld, ow in zip(names, lands, own)]
    return dict(zip(names, fins)), dict(zip(names, _share_final(fins, f"share_final_{tag}")))


def kernel(x, positions, pre_mix_g, w_in, pool_w, pool_scale, pool_proj, conv_w, conv_b, conv_norm_g, conv_norm_b, conv_proj, sgu_norm_g, sgu_norm_b, sgu_w, sgu_b, sgu_proj, q_norm_g, w_uq, kv_norm_g, w_ukv, attn_proj, w_out, post_mix_g, pre_mlp_g, w_up, w_down, post_mlp_g, loss_target, m_pre_mix_g, m_w_in, m_pool_w, m_pool_scale, m_pool_proj, m_conv_w, m_conv_b, m_conv_norm_g, m_conv_norm_b, m_conv_proj, m_sgu_norm_g, m_sgu_norm_b, m_sgu_w, m_sgu_b, m_sgu_proj, m_q_norm_g, m_w_uq, m_kv_norm_g, m_w_ukv, m_attn_proj, m_w_out, m_post_mix_g, m_pre_mlp_g, m_w_up, m_w_down, m_post_mlp_g, v_pre_mix_g, v_w_in, v_pool_w, v_pool_scale, v_pool_proj, v_conv_w, v_conv_b, v_conv_norm_g, v_conv_norm_b, v_conv_proj, v_sgu_norm_g, v_sgu_norm_b, v_sgu_w, v_sgu_b, v_sgu_proj, v_q_norm_g, v_w_uq, v_kv_norm_g, v_w_ukv, v_attn_proj, v_w_out, v_post_mix_g, v_pre_mlp_g, v_w_up, v_w_down, v_post_mlp_g):
    arg = dict(locals())
    depth = pre_mix_g.shape[0]
    s, d = x.shape[1], x.shape[2]
    nbc = 7 * BR + QK_ROPE
    nz = 4 * d + -(-(7 * BR + 128) // 512) * 512
    g0 = 4 * d // BR
    col = dict(pool=g0, conv=g0 + 1, sgu=g0 + 3, q=g0 + 5, kv=g0 + 6, kr=(4 * d + 7 * BR) // 128)
    my_x, my_y, my_c = _place()
    my_chip = 2 * my_x + my_y
    x2 = x.reshape(s, d)
    target = loss_target.reshape(s, d)
    tabs = _rope_tables(positions.reshape(s))
    how_of = dict(BIG)

    rest_names = [n for n in BIG_NAMES if n != 'w_in']

    def gather_start(tag, names, l, gate):
        srcs = [arg[n][l].astype(BF16) for n in names]
        if gate is not None:
            srcs, _ = lax.optimization_barrier((srcs, gate))
        return names, srcs, _ici_start('gather', srcs, [(N_CHIPS,) + t.shape for t in srcs], f"gather_start_{tag}")

    def gather_finish(tag, group, after):
        names, srcs, started = group
        lands = _ici_wait('gather', started, after, f"gather_wait_{tag}")
        full = dict(zip(names, _share_weights(srcs, lands, f"share_weights_{tag}")))
        w_l = {n: (full[n] if how_of[n] == 'col' else full[n].reshape(-1, full[n].shape[2]))
               for n in names if n not in ('w_in', 'w_uq')}
        if 'w_in' in full:
            w_l['w_in_x'] = _w_in_to_kernel_layout(full['w_in'], nbc, nz)
        if 'w_uq' in full:
            w_l['w_uq_x'] = _w_uq_to_kernel_layout(full['w_uq'])
        return w_l

    def behind(value, group):
        return lax.optimization_barrier((value, group[2][4]))[0]

    first_in = gather_start("l0_in", ['w_in'], 0, None)

    conv_layout, _ = _layout([('conv_w', conv_w.shape)])
    conv_all = _gather_all(_pack(conv_layout, lambda n: arg[n], F32), "gather_conv_w")
    conv_w_full = jnp.concatenate([_unpack(conv_layout, conv_all[4 * (k // 2) + 2 * (k % 2)])['conv_w']
                                   for k in range(4)], axis=-1)
    tril = jnp.tril(jnp.ones((GROUP, GROUP), F32))
    smalls = []
    for l in range(depth):
        sm = {n: arg[n][l][None, :] for n in ('pre_mix_g', 'pool_scale', 'conv_b', 'conv_norm_g', 'conv_norm_b',
                                               'sgu_norm_g', 'sgu_norm_b', 'q_norm_g', 'kv_norm_g', 'post_mix_g',
                                               'pre_mlp_g', 'post_mlp_g')}
        sm['pool_w16'] = pool_w[l].astype(BF16)
        sm['sgu_w16'] = (sgu_w[l] * tril).astype(BF16)
        sm['sgu_bias_full'] = jnp.broadcast_to(sgu_b[l][:, :, None], (BR // GROUP, GROUP, GROUP))
        sm['conv_w'] = conv_w_full[l]
        sm['tril'] = tril
        smalls.append(sm)

    weights, saved, groups = [], [], {}
    x_cur = x2
    h = _rms_fwd(x2, smalls[0]['pre_mix_g'] + first_in[2][4][0, 0], "rms_in")
    w_in_x = gather_finish("l0_in", first_in, h)['w_in_x']
    groups[0] = gather_start("l0_rest", rest_names, 0, w_in_x)
    h = behind(h, groups[0])

    def start_next(l, z, gate):
        if l + 1 < depth:
            groups[l + 1] = gather_start(f"l{l + 1}", BIG_NAMES, l + 1, gate)
            z = behind(z, groups[l + 1])
        return z

    for l in range(depth):
        if l == 0:
            def other_weights(z):
                w_l = gather_finish("l0_rest", groups[0], z)
                return start_next(0, z, w_l['w_down']), w_l
        else:
            w_l = gather_finish(f"l{l}", groups[l], saved[l - 1]['f'])
            w_in_x = w_l.pop('w_in_x')

            def other_weights(z, w_l=w_l, l=l):
                return start_next(l, z, w_l['w_down']), w_l
        sv, w_full = _layer_forward(l, x_cur, h, w_in_x, other_weights, smalls[l], tabs, col)
        weights.append(w_full)
        saved.append(sv)
        if l + 1 < depth:
            x_cur, h = _resid_norm_fwd(sv['x1'], sv['f'], smalls[l]['post_mlp_g'], smalls[l + 1]['pre_mix_g'],
                                       f"mlp_out_l{l}")
    d_y, loss_local = _resid_norm_loss(saved[-1]['x1'], saved[-1]['f'], smalls[-1]['post_mlp_g'], target, "loss_head")

    reduce_groups = (("a", rest_names), ("b", ['w_in']))
    reduce_started = [None] * depth
    fin_mine, fin_other = [dict() for _ in range(depth)], [dict() for _ in range(depth)]
    grads_small = [dict() for _ in range(depth)]

    def finish_layer(l, after):
        for key, names in reduce_groups:
            mine, other = _reduce_finish(f"{key}_l{l}", names, reduce_started[l][key], my_chip, after)
            fin_mine[l].update(mine)
            fin_other[l].update(other)

    d_x2 = d_y
    _, d_f, g_post, _ = _resid_norm_bwd(d_y, None, None, saved[-1]['f'], smalls[-1]['post_mlp_g'], None,
                                        f"mlp_out_bwd_l{depth - 1}")
    grads_small[-1]['post_mlp_g'] = g_post
    grad_x = None
    for l in reversed(range(depth)):
        started_l = {}

        def early_reduce(gb, l=l, started_l=started_l):
            started_l["a"] = _reduce_start(f"a_l{l}", reduce_groups[0][1], gb, my_c)
            return started_l["a"][4][0, 0]

        d_x1, d_h, gb, gs = _layer_backward(l, saved[l], d_f, d_x2, weights[l], smalls[l], tabs, col, nz, nbc,
                                            early_reduce)
        grads_small[l].update(gs)
        started_l["b"] = _reduce_start(f"b_l{l}", reduce_groups[1][1], gb, my_c)
        reduce_started[l] = started_l
        pin = started_l["b"][4][0, 0]
        if l > 0:
            d_x2, d_f, g_post, g_pre = _resid_norm_bwd(d_x1, d_h, saved[l]['x_in'], saved[l - 1]['f'],
                                                      smalls[l - 1]['post_mlp_g'], smalls[l]['pre_mix_g'] + pin,
                                                      f"mlp_out_bwd_l{l - 1}")
            grads_small[l - 1]['post_mlp_g'] = g_post
            grads_small[l]['pre_mix_g'] = g_pre
        else:
            grad_x, _, _, g_pre = _resid_norm_bwd(d_x1, d_h, saved[0]['x_in'], None, None,
                                                  smalls[0]['pre_mix_g'] + pin, "rms_in_bwd")
            grads_small[0]['pre_mix_g'] = g_pre
        if l + 1 < depth:
            finish_layer(l + 1, d_h)

    small_shapes = []
    for n in SMALL_NAMES:
        shp = arg[n].shape[1:]
        if n == 'conv_w':
            shp = (shp[0], shp[1] * 4)
        small_shapes.append((n, shp))
    small_layout, _ = _layout([(f"{n}.{l}", shp) for l in range(depth) for n, shp in small_shapes])

    def small_get(key):
        n, l = key.rsplit('.', 1)
        return grads_small[int(l)][n]

    small_all = _gather_all(_pack(small_layout, small_get, F32), "gather_small_grads")
    g_small = _unpack(small_layout, _sum_slots(small_all, "sum_small_grads"))
    finish_layer(0, small_all)

    grad, delta, new_m, new_v = {}, {}, {}, {}
    for n in WEIGHT_NAMES:
        if n in BIG_NAMES:
            grad[n], delta[n], new_m[n], new_v[n] = _adamw_big(
                arg[n], [fin_mine[l][n] for l in range(depth)], [fin_other[l][n] for l in range(depth)],
                arg['m_' + n], arg['v_' + n], f"adamw_{n}")
            continue
        g = jnp.stack([g_small[f"{n}.{l}"] for l in range(depth)])
        if n == 'conv_w':
            wd = conv_w.shape[2]
            g = lax.dynamic_slice_in_dim(g, my_chip * wd, wd, axis=2)
        grad[n] = g
        delta[n], new_m[n], new_v[n] = _adamw(arg[n], g, arg['m_' + n], arg['v_' + n], f"adamw_{n}")

    loss = lax.psum(loss_local, ("x", "y", "c"))
    return (loss, grad_x.reshape(x.shape), *[grad[n] for n in WEIGHT_NAMES], *[delta[n] for n in WEIGHT_NAMES],
            *[new_m[n] for n in WEIGHT_NAMES], *[new_v[n] for n in WEIGHT_NAMES])
```

```python
import math

import jax
import jax.numpy as jnp
from jax import lax
from jax.experimental import pallas as pl
from jax.experimental.pallas import tpu as pltpu

F32 = jnp.float32
BF16 = jnp.bfloat16
MESH = pl.DeviceIdType.MESH

EPS = 1e-6
POOL_WINDOWS = (2, 4, 8, 16)
GROUP = 128
BR = 512
CONV_WIDTH = 31
HEADS = 8
QK_NOPE = 128
QK_ROPE = 64
V_DIM = 128
ROPE_THETA = 10000.0
ATT_SCALE = (QK_NOPE + QK_ROPE) ** -0.5
GELU_C = math.sqrt(2.0 / math.pi)
ADAM_LR, ADAM_B1, ADAM_B2, ADAM_EPS, ADAM_WD, ADAM_STEP = 0.001, 0.9, 0.999, 1e-08, 0.01, 10

VMEM_LIMIT = 48 * 1024 * 1024
PACK_COLS = 1024
PACK_ROW_ALIGN = 16
CONV_HALO = 32
POOL_HALO = 16

TOK_WIDE = 256
TOK_NARROW = 512
ATT_TILE = 1024
MM_TILE_M = 1024
MM_TILE_N = 1024
MM_TILE_K = 2048
EW_BLOCK_BYTES = 1536 * 1024
N_CHIPS = 4

WEIGHT_NAMES = ['pre_mix_g', 'w_in', 'pool_w', 'pool_scale', 'pool_proj', 'conv_w', 'conv_b', 'conv_norm_g',
                'conv_norm_b', 'conv_proj', 'sgu_norm_g', 'sgu_norm_b', 'sgu_w', 'sgu_b', 'sgu_proj', 'q_norm_g',
                'w_uq', 'kv_norm_g', 'w_ukv', 'attn_proj', 'w_out', 'post_mix_g', 'pre_mlp_g', 'w_up', 'w_down',
                'post_mlp_g']
BIG = [('w_in', 'col'), ('pool_proj', 'col'), ('conv_proj', 'col'), ('sgu_proj', 'col'), ('w_uq', 'col'),
       ('w_ukv', 'col'), ('attn_proj', 'col'), ('w_out', 'row'), ('w_up', 'col'), ('w_down', 'row')]
BIG_NAMES = [n for n, _ in BIG]
SMALL_NAMES = [n for n in WEIGHT_NAMES if n not in BIG_NAMES]


def _bs(shape, index_map):
    return pl.BlockSpec(shape, index_map)


def _sds(shape, dtype):
    return jax.ShapeDtypeStruct(shape, dtype)


def _tile(n, candidates):
    for t in candidates:
        if n % t == 0:
            return t
    return n


def _div_tile(n, target, align=8):
    t = min(n, target) // align * align
    while t >= align:
        if n % t == 0:
            return t
        t -= align
    return n


def _call(body, *, name, grid, in_specs, out_specs, out_shape, scratch=(), sem=None):
    return pl.pallas_call(
        body, name=name, grid=grid, in_specs=in_specs, out_specs=out_specs, out_shape=out_shape,
        scratch_shapes=list(scratch),
        compiler_params=pltpu.CompilerParams(dimension_semantics=sem, vmem_limit_bytes=VMEM_LIMIT))


def _sigmoid(v):
    return 1.0 / (1.0 + jnp.exp(-v))


def _gelu(v):
    return 0.5 * v * (1.0 + jnp.tanh(GELU_C * (v + 0.044715 * v * v * v)))


def _gelu_grad(v):
    t = jnp.tanh(GELU_C * (v + 0.044715 * v * v * v))
    return 0.5 * (1.0 + t) + 0.5 * v * (1.0 - t * t) * GELU_C * (1.0 + 3.0 * 0.044715 * v * v)


def _rstd(v):
    return lax.rsqrt(jnp.mean(v * v, axis=-1, keepdims=True) + EPS)


def _rms_bwd(v, r, t):
    return r * t - v * (r * r * r) * jnp.mean(v * t, axis=-1, keepdims=True)


def _ln_stats(v):
    mu = jnp.mean(v, axis=-1, keepdims=True)
    d = v - mu
    r = lax.rsqrt(jnp.mean(d * d, axis=-1, keepdims=True) + EPS)
    return d * r, r


def _ln_bwd(xh, r, dxh):
    return r * (dxh - jnp.mean(dxh, axis=-1, keepdims=True) - xh * jnp.mean(dxh * xh, axis=-1, keepdims=True))


def _colsum(v):
    return jnp.sum(v, axis=0, keepdims=True)


def _dot(a, b):
    return jnp.dot(a, b, preferred_element_type=F32)


def _dot_nt(a, b):
    return lax.dot_general(a, b, (((1,), (1,)), ((), ())), preferred_element_type=F32)


def _dot_tn(a, b):
    return lax.dot_general(a, b, (((0,), (0,)), ((), ())), preferred_element_type=F32)


def _mm(a, b, *, name, ta=False, tb=False, b_split=False, out_split=False, out_dtypes=(F32,), epilogue=None,
        extras=()):
    m = a.shape[1] if ta else a.shape[0]
    k = a.shape[0] if ta else a.shape[1]
    shard = b.shape[2] if b_split else None
    b_rows, b_cols = (b.shape[1], N_CHIPS * shard) if b_split else b.shape
    n = b_rows if tb else b_cols
    assert k == (b_cols if tb else b_rows)
    assert not (out_split and extras)
    tm = _div_tile(m, MM_TILE_M, 128)
    tn = _div_tile(n // N_CHIPS if (out_split or (b_split and not tb)) else n, MM_TILE_N, 128)
    tk = _div_tile(shard if (b_split and tb) else k, MM_TILE_K, 128)
    nk = k // tk
    n_extra, n_out = len(extras), len(out_dtypes)
    dims = (((0 if ta else 1,), (1 if tb else 0,)), ((), ()))

    def body(a_ref, b_ref, *rest):
        extra_refs, out_refs = rest[:n_extra], rest[n_extra:n_extra + n_out]
        part = lax.dot_general(a_ref[...].astype(BF16), b_ref[...].astype(BF16), dims, preferred_element_type=F32)

        def finish(res):
            res = (res,) if epilogue is None else epilogue(res, *[e[...] for e in extra_refs])
            for o, r in zip(out_refs, res):
                o[...] = r.astype(o.dtype)

        if nk == 1:
            finish(part)
        else:
            acc, kk = rest[-1], pl.program_id(2)

            @pl.when(kk == 0)
            def _():
                acc[...] = part

            @pl.when(jnp.logical_and(kk > 0, kk < nk - 1))
            def _():
                acc[...] += part

            @pl.when(kk == nk - 1)
            def _():
                finish(acc[...] + part)

    a_spec = _bs((tk, tm), lambda i, j, kk: (kk, i)) if ta else _bs((tm, tk), lambda i, j, kk: (i, kk))
    if not b_split:
        b_spec = _bs((tn, tk), lambda i, j, kk: (j, kk)) if tb else _bs((tk, tn), lambda i, j, kk: (kk, j))
    elif tb:
        kpb = shard // tk
        b_spec = _bs((None, tn, tk), lambda i, j, kk: (kk // kpb, j, kk % kpb))
    else:
        npb = shard // tn
        b_spec = _bs((None, tk, tn), lambda i, j, kk: (j // npb, kk, j % npb))
    e_spec = _bs((tm, tn), lambda i, j, kk: (i, j))
    if out_split:
        npo = (n // N_CHIPS) // tn
        o_spec = _bs((None, tm, tn), lambda i, j, kk: (j // npo, i, j % npo))
        out_shape = [_sds((N_CHIPS, m, n // N_CHIPS), dt) for dt in out_dtypes]
    else:
        o_spec, out_shape = e_spec, [_sds((m, n), dt) for dt in out_dtypes]
    outs = _call(body, name=name, grid=(m // tm, n // tn, nk), in_specs=[a_spec, b_spec] + [e_spec] * n_extra,
                 out_specs=[o_spec] * n_out, out_shape=out_shape,
                 scratch=[pltpu.VMEM((tm, tn), F32)] if nk > 1 else [],
                 sem=("parallel", "parallel", "arbitrary"))(a, b, *extras)
    return outs[0] if n_out == 1 else outs


def _rms_fwd(x, g, name):
    s, d = x.shape
    tt = _tile(s, (TOK_WIDE,))

    def body(x_ref, g_ref, h_ref):
        v = x_ref[...]
        h_ref[...] = (v * _rstd(v) * g_ref[...]).astype(BF16)

    row = _bs((tt, d), lambda i: (i, 0))
    return _call(body, name=name, grid=(s // tt,), in_specs=[row, _bs((1, d), lambda i: (0, 0))],
                 out_specs=row, out_shape=_sds((s, d), BF16), sem=("parallel",))(x, g)


def _resid_norm_fwd(xres, y, g_post, g_next, name):
    s, d = xres.shape
    tt = _tile(s, (TOK_WIDE,))

    def body(xr_ref, y_ref, gp_ref, gn_ref, xn_ref, h_ref):
        yv = y_ref[...]
        xn = xr_ref[...] + yv * _rstd(yv) * gp_ref[...]
        xn_ref[...] = xn
        h_ref[...] = (xn * _rstd(xn) * gn_ref[...]).astype(BF16)

    row, vec = _bs((tt, d), lambda i: (i, 0)), _bs((1, d), lambda i: (0, 0))
    return _call(body, name=name, grid=(s // tt,), in_specs=[row, row, vec, vec], out_specs=[row, row],
                 out_shape=[_sds((s, d), F32), _sds((s, d), BF16)], sem=("parallel",))(xres, y, g_post, g_next)


def _resid_norm_loss(xres, y, g_post, target, name):
    s, d = xres.shape
    tt = _tile(s, (TOK_WIDE,))

    def body(xr_ref, y_ref, gp_ref, t_ref, dy_ref, loss_ref):
        @pl.when(pl.program_id(0) == 0)
        def _():
            loss_ref[...] = jnp.zeros_like(loss_ref)

        yv = y_ref[...]
        err = xr_ref[...] + yv * _rstd(yv) * gp_ref[...] - t_ref[...]
        dy_ref[...] = err * (1.0 / d)
        loss_ref[...] += 0.5 * jnp.sum(jnp.mean(err * err, axis=-1, keepdims=True))

    row, vec = _bs((tt, d), lambda i: (i, 0)), _bs((1, d), lambda i: (0, 0))
    dy, loss = _call(body, name=name, grid=(s // tt,), in_specs=[row, row, vec, row],
                     out_specs=[row, _bs((8, 128), lambda i: (0, 0))],
                     out_shape=[_sds((s, d), F32), _sds((8, 128), F32)], sem=("arbitrary",))(xres, y, g_post, target)
    return dy, loss[0, 0]


def _resid_norm_bwd(d_out, d_h, x_new, y, g_post, g_next, name):
    s, d = d_out.shape
    tt = _tile(s, (TOK_WIDE,))
    has_next, has_y = d_h is not None, y is not None

    def body(*refs):
        it = iter(refs)
        do_ref = next(it)
        if has_next:
            dh_ref, xn_ref, gn_ref = next(it), next(it), next(it)
        if has_y:
            y_ref, gp_ref = next(it), next(it)
        if has_next:
            dx_ref = next(it)
        if has_y:
            dy_ref, dgp_ref = next(it), next(it)
        if has_next:
            dgn_ref = next(it)

        first = pl.program_id(0) == 0
        dx = do_ref[...]
        if has_next:
            xn, dh = xn_ref[...], dh_ref[...]
            r = _rstd(xn)
            dx = dx + _rms_bwd(xn, r, dh * gn_ref[...])
            dx_ref[...] = dx

            @pl.when(first)
            def _():
                dgn_ref[...] = jnp.zeros_like(dgn_ref)

            dgn_ref[...] += _colsum(dh * xn * r)
        if has_y:
            yv = y_ref[...]
            ry = _rstd(yv)
            dy_ref[...] = _rms_bwd(yv, ry, dx * gp_ref[...]).astype(BF16)

            @pl.when(first)
            def _():
                dgp_ref[...] = jnp.zeros_like(dgp_ref)

            dgp_ref[...] += _colsum(dx * yv * ry)

    row, vec = _bs((tt, d), lambda i: (i, 0)), _bs((1, d), lambda i: (0, 0))
    args, in_specs, out_specs, out_shape = [d_out], [row], [], []
    if has_next:
        args += [d_h, x_new, g_next]
        in_specs += [row, row, vec]
    if has_y:
        args += [y, g_post]
        in_specs += [row, vec]
    if has_next:
        out_specs.append(row)
        out_shape.append(_sds((s, d), F32))
    if has_y:
        out_specs += [row, vec]
        out_shape += [_sds((s, d), BF16), _sds((1, d), F32)]
    if has_next:
        out_specs.append(vec)
        out_shape.append(_sds((1, d), F32))
    outs = list(_call(body, name=name, grid=(s // tt,), in_specs=in_specs, out_specs=out_specs, out_shape=out_shape,
                      sem=("arbitrary",))(*args))
    d_x = outs.pop(0) if has_next else None
    d_y, d_gp = (outs.pop(0), outs.pop(0)) if has_y else (None, None)
    d_gn = outs.pop(0) if has_next else None
    return d_x, d_y, d_gp, d_gn


def _pool_counts(t0, tt, w):
    t = t0 + lax.broadcasted_iota(jnp.int32, (tt, 1), 0)
    return jnp.minimum(t + 1, w).astype(F32)


def _pool_pooled(ext, a, t0, tt, g, w):
    cols = pl.ds(g * GROUP, GROUP)
    sm = ext[pl.ds(POOL_HALO, tt), cols]
    for j in range(1, w):
        sm = sm + ext[pl.ds(POOL_HALO - j, tt), cols]
    return sm / _pool_counts(t0, tt, w) - a[:, g * GROUP:(g + 1) * GROUP]


def _pool_fwd(z, cb, pool_w, pool_scale, name):
    s = z.shape[0]
    tt = _tile(s, (TOK_NARROW,))
    hb = tt // POOL_HALO

    def body(zc_ref, zp_ref, pw_ref, sc_ref, out_ref, ext):
        i = pl.program_id(0)
        a = zc_ref[...]
        ext[pl.ds(0, POOL_HALO), :] = jnp.where(i > 0, zp_ref[...], 0.0)
        ext[pl.ds(POOL_HALO, tt), :] = a
        for g, w in enumerate(POOL_WINDOWS):
            pooled = _pool_pooled(ext, a, i * tt, tt, g, w).astype(BF16)
            mixed = _dot(pooled, pw_ref[g])
            out_ref[:, g * GROUP:(g + 1) * GROUP] = (mixed * sc_ref[:, g * GROUP:(g + 1) * GROUP]).astype(BF16)

    return _call(body, name=name, grid=(s // tt,),
                 in_specs=[_bs((tt, BR), lambda i: (i, cb)),
                           _bs((POOL_HALO, BR), lambda i: (jnp.maximum(i * hb - 1, 0), cb)),
                           _bs((len(POOL_WINDOWS), GROUP, GROUP), lambda i: (0, 0, 0)),
                           _bs((1, BR), lambda i: (0, 0))],
                 out_specs=_bs((tt, BR), lambda i: (i, 0)), out_shape=_sds((s, BR), BF16),
                 scratch=[pltpu.VMEM((tt + POOL_HALO, BR), F32)], sem=("parallel",))(z, z, pool_w, pool_scale)


def _pool_bwd(z, cb, d_b, pool_w, pool_scale, name):
    s = z.shape[0]
    tt = _tile(s, (TOK_NARROW,))
    nt, hb, ng = s // tt, tt // POOL_HALO, len(POOL_WINDOWS)

    def body(zc_ref, zp_ref, db_ref, pw_ref, sc_ref, dz_ref, dpw_ref, dsc_ref, ext, ext_e, carry):
        step = pl.program_id(0)
        i = nt - 1 - step

        @pl.when(step == 0)
        def _():
            dpw_ref[...] = jnp.zeros_like(dpw_ref)
            dsc_ref[...] = jnp.zeros_like(dsc_ref)
            carry[...] = jnp.zeros_like(carry)

        a = zc_ref[...]
        ext[pl.ds(0, POOL_HALO), :] = jnp.where(i > 0, zp_ref[...], 0.0)
        ext[pl.ds(POOL_HALO, tt), :] = a
        ext_e[pl.ds(tt, POOL_HALO), :] = carry[...]
        db = db_ref[...]
        for g, w in enumerate(POOL_WINDOWS):
            c0, c1 = g * GROUP, (g + 1) * GROUP
            pooled = _pool_pooled(ext, a, i * tt, tt, g, w).astype(BF16)
            mixed = _dot(pooled, pw_ref[g])
            dsc_ref[:, c0:c1] += _colsum(db[:, c0:c1] * mixed)
            dmixed = (db[:, c0:c1] * sc_ref[:, c0:c1]).astype(BF16)
            dpw_ref[g] += _dot_tn(pooled, dmixed)
            dpooled = _dot_nt(dmixed, pw_ref[g])
            ext_e[pl.ds(0, tt), pl.ds(c0, GROUP)] = dpooled / _pool_counts(i * tt, tt, w)
            acc = -dpooled
            for j in range(w):
                acc = acc + ext_e[pl.ds(j, tt), pl.ds(c0, GROUP)]
            dz_ref[:, c0:c1] = acc.astype(BF16)
        carry[...] = ext_e[pl.ds(0, POOL_HALO), :]

    rev = lambda st: nt - 1 - st
    dz, dpw, dsc = _call(
        body, name=name, grid=(nt,),
        in_specs=[_bs((tt, BR), lambda st: (rev(st), cb)),
                  _bs((POOL_HALO, BR), lambda st: (jnp.maximum(rev(st) * hb - 1, 0), cb)),
                  _bs((tt, BR), lambda st: (rev(st), 0)),
                  _bs((ng, GROUP, GROUP), lambda st: (0, 0, 0)),
                  _bs((1, BR), lambda st: (0, 0))],
        out_specs=[_bs((tt, BR), lambda st: (rev(st), 0)), _bs((ng, GROUP, GROUP), lambda st: (0, 0, 0)),
                   _bs((1, BR), lambda st: (0, 0))],
        out_shape=[_sds((s, BR), BF16), _sds((ng, GROUP, GROUP), F32), _sds((1, BR), F32)],
        scratch=[pltpu.VMEM((tt + POOL_HALO, BR), F32), pltpu.VMEM((tt + POOL_HALO, BR), F32),
                 pltpu.VMEM((POOL_HALO, BR), F32)],
        sem=("arbitrary",))(z, z, d_b, pool_w, pool_scale)
    return dz, dpw, dsc


def _conv_fwd(z, cb, conv_w, conv_b, ng, nb, name):
    s = z.shape[0]
    tt = _tile(s, (TOK_NARROW,))
    hb = tt // CONV_HALO

    def body(a_ref, g_ref, ap_ref, gp_ref, w_ref, b_ref, ng_ref, nb_ref, out_ref, ypre_ref, ext):
        i = pl.program_id(0)
        ext[pl.ds(0, CONV_HALO), :] = jnp.where(i > 0, ap_ref[...] * _sigmoid(gp_ref[...]), 0.0)
        ext[pl.ds(CONV_HALO, tt), :] = a_ref[...] * _sigmoid(g_ref[...])
        acc = jnp.zeros((tt, BR), F32)
        for k in range(CONV_WIDTH):
            acc = acc + w_ref[pl.ds(k, 1), :] * ext[pl.ds(CONV_HALO - (CONV_WIDTH - 1) + k, tt), :]
        ypre = acc + b_ref[...]
        ypre_ref[...] = ypre
        xh, _ = _ln_stats(ypre)
        yl = xh * ng_ref[...] + nb_ref[...]
        out_ref[...] = (yl * _sigmoid(yl)).astype(BF16)

    cur = lambda c: _bs((tt, BR), lambda i: (i, c))
    prev = lambda c: _bs((CONV_HALO, BR), lambda i: (jnp.maximum(i * hb - 1, 0), c))
    vec = _bs((1, BR), lambda i: (0, 0))
    row = _bs((tt, BR), lambda i: (i, 0))
    return _call(body, name=name, grid=(s // tt,),
                 in_specs=[cur(cb), cur(cb + 1), prev(cb), prev(cb + 1),
                           _bs((CONV_WIDTH, BR), lambda i: (0, 0)), vec, vec, vec],
                 out_specs=[row, row], out_shape=[_sds((s, BR), BF16), _sds((s, BR), F32)],
                 scratch=[pltpu.VMEM((tt + CONV_HALO, BR), F32)], sem=("parallel",))(
        z, z, z, z, conv_w, conv_b, ng, nb)


def _conv_bwd(z, cb, d_b, ypre, conv_w, ng, nb, name):
    s = z.shape[0]
    tt = _tile(s, (TOK_NARROW,))
    nt, hb = s // tt, tt // CONV_HALO
    lead = CONV_HALO - (CONV_WIDTH - 1)

    def body(a_ref, g_ref, ap_ref, gp_ref, db_ref, yp_ref, w_ref, ng_ref, nb_ref,
             da_ref, dg_ref, dw_ref, dcb_ref, dng_ref, dnb_ref, ext, ext_d, carry):
        step = pl.program_id(0)
        i = nt - 1 - step

        @pl.when(step == 0)
        def _():
            dw_ref[...] = jnp.zeros_like(dw_ref)
            dcb_ref[...] = jnp.zeros_like(dcb_ref)
            dng_ref[...] = jnp.zeros_like(dng_ref)
            dnb_ref[...] = jnp.zeros_like(dnb_ref)
            carry[...] = jnp.zeros_like(carry)

        xh, r = _ln_stats(yp_ref[...])
        yl = xh * ng_ref[...] + nb_ref[...]
        sg = _sigmoid(yl)
        dyl = db_ref[...] * (sg * (1.0 + yl * (1.0 - sg)))
        dng_ref[...] += _colsum(dyl * xh)
        dnb_ref[...] += _colsum(dyl)
        dypre = _ln_bwd(xh, r, dyl * ng_ref[...])
        dcb_ref[...] += _colsum(dypre)

        a, gate = a_ref[...], g_ref[...]
        sgate = _sigmoid(gate)
        ext[pl.ds(0, CONV_HALO), :] = jnp.where(i > 0, ap_ref[...] * _sigmoid(gp_ref[...]), 0.0)
        ext[pl.ds(CONV_HALO, tt), :] = a * sgate
        ext_d[pl.ds(0, tt), :] = dypre
        ext_d[pl.ds(tt, CONV_HALO), :] = carry[...]
        dglu = jnp.zeros((tt, BR), F32)
        for k in range(CONV_WIDTH):
            dw_ref[pl.ds(k, 1), :] += _colsum(dypre * ext[pl.ds(lead + k, tt), :])
            dglu = dglu + w_ref[pl.ds(k, 1), :] * ext_d[pl.ds(CONV_WIDTH - 1 - k, tt), :]
        carry[...] = ext_d[pl.ds(0, CONV_HALO), :]
        da_ref[...] = (dglu * sgate).astype(BF16)
        dg_ref[...] = (dglu * a * sgate * (1.0 - sgate)).astype(BF16)

    rev = lambda st: nt - 1 - st
    cur = lambda c: _bs((tt, BR), lambda st: (rev(st), c))
    prev = lambda c: _bs((CONV_HALO, BR), lambda st: (jnp.maximum(rev(st) * hb - 1, 0), c))
    vec = _bs((1, BR), lambda st: (0, 0))
    row = _bs((tt, BR), lambda st: (rev(st), 0))
    wsp = _bs((CONV_WIDTH, BR), lambda st: (0, 0))
    return _call(body, name=name, grid=(nt,),
                 in_specs=[cur(cb), cur(cb + 1), prev(cb), prev(cb + 1), row, row, wsp, vec, vec],
                 out_specs=[row, row, wsp, vec, vec, vec],
                 out_shape=[_sds((s, BR), BF16), _sds((s, BR), BF16), _sds((CONV_WIDTH, BR), F32),
                            _sds((1, BR), F32), _sds((1, BR), F32), _sds((1, BR), F32)],
                 scratch=[pltpu.VMEM((tt + CONV_HALO, BR), F32), pltpu.VMEM((tt + CONV_HALO, BR), F32),
                          pltpu.VMEM((CONV_HALO, BR), F32)],
                 sem=("arbitrary",))(z, z, z, z, d_b, ypre, conv_w, ng, nb)


def _sgu_fwd(z, cb, ng, nb, w_masked, bias_full, name):
    s = z.shape[0]
    tt = _tile(s, (TOK_NARROW,))
    ngr = BR // GROUP

    def body(u_ref, v_ref, ng_ref, nb_ref, w_ref, bb_ref, out_ref):
        ua = _gelu(u_ref[...])
        xh, _ = _ln_stats(_gelu(v_ref[...]))
        vn = (xh * ng_ref[...] + nb_ref[...]).astype(BF16)
        for n in range(tt // GROUP):
            for g in range(ngr):
                r0, c0 = n * GROUP, g * GROUP
                sp = _dot(w_ref[g], vn[r0:r0 + GROUP, c0:c0 + GROUP]) + bb_ref[g]
                out_ref[r0:r0 + GROUP, c0:c0 + GROUP] = (ua[r0:r0 + GROUP, c0:c0 + GROUP] * sp).astype(BF16)

    vec = _bs((1, BR), lambda i: (0, 0))
    sq = _bs((ngr, GROUP, GROUP), lambda i: (0, 0, 0))
    return _call(body, name=name, grid=(s // tt,),
                 in_specs=[_bs((tt, BR), lambda i: (i, cb)), _bs((tt, BR), lambda i: (i, cb + 1)), vec, vec, sq, sq],
                 out_specs=_bs((tt, BR), lambda i: (i, 0)), out_shape=_sds((s, BR), BF16),
                 sem=("parallel",))(z, z, ng, nb, w_masked, bias_full)


def _sgu_bwd(z, cb, d_b, ng, nb, w_masked, bias_full, name):
    s = z.shape[0]
    tt = _tile(s, (TOK_NARROW,))
    ngr = BR // GROUP

    def body(u_ref, v_ref, db_ref, ng_ref, nb_ref, w_ref, bb_ref,
             du_ref, dv_ref, dw_ref, dbias_ref, dng_ref, dnb_ref, dvn_s):
        @pl.when(pl.program_id(0) == 0)
        def _():
            dw_ref[...] = jnp.zeros_like(dw_ref)
            dbias_ref[...] = jnp.zeros_like(dbias_ref)
            dng_ref[...] = jnp.zeros_like(dng_ref)
            dnb_ref[...] = jnp.zeros_like(dnb_ref)

        u, v, db = u_ref[...], v_ref[...], db_ref[...]
        ua = _gelu(u)
        xh, r = _ln_stats(_gelu(v))
        vn = (xh * ng_ref[...] + nb_ref[...]).astype(BF16)
        for n in range(tt // GROUP):
            for g in range(ngr):
                rows, cols = slice(n * GROUP, (n + 1) * GROUP), slice(g * GROUP, (g + 1) * GROUP)
                vn_c = vn[rows, cols]
                sp = _dot(w_ref[g], vn_c) + bb_ref[g]
                du_ref[rows, cols] = (db[rows, cols] * sp * _gelu_grad(u[rows, cols])).astype(BF16)
                dsp = db[rows, cols] * ua[rows, cols]
                dbias_ref[g] += dsp
                dsp16 = dsp.astype(BF16)
                dw_ref[g] += _dot_nt(dsp16, vn_c)
                dvn_s[rows, cols] = _dot_tn(w_ref[g], dsp16)
        dvn = dvn_s[...]
        dng_ref[...] += _colsum(dvn * xh)
        dnb_ref[...] += _colsum(dvn)
        dv_ref[...] = (_ln_bwd(xh, r, dvn * ng_ref[...]) * _gelu_grad(v)).astype(BF16)

    vec = _bs((1, BR), lambda i: (0, 0))
    sq = _bs((ngr, GROUP, GROUP), lambda i: (0, 0, 0))
    row = _bs((tt, BR), lambda i: (i, 0))
    return _call(body, name=name, grid=(s // tt,),
                 in_specs=[_bs((tt, BR), lambda i: (i, cb)), _bs((tt, BR), lambda i: (i, cb + 1)), row, vec, vec, sq, sq],
                 out_specs=[row, row, sq, sq, vec, vec],
                 out_shape=[_sds((s, BR), BF16), _sds((s, BR), BF16), _sds((ngr, GROUP, GROUP), F32),
                            _sds((ngr, GROUP, GROUP), F32), _sds((1, BR), F32), _sds((1, BR), F32)],
                 scratch=[pltpu.VMEM((tt, BR), F32)], sem=("arbitrary",))(z, z, d_b, ng, nb, w_masked, bias_full)


def _rope(rv, c_t, s1_t, s2_t):
    return rv * c_t + pltpu.roll(rv, 96, 1) * s1_t + pltpu.roll(rv, 32, 1) * s2_t


def _rope_bwd(gv, c_t, s1_t, s2_t):
    return gv * c_t - pltpu.roll(gv, 96, 1) * s1_t - pltpu.roll(gv, 32, 1) * s2_t


def _mla_prep(z, cb_q, cb_kv, cb_kr, qg, kvg, c_t, s1_t, s2_t, name):
    s = z.shape[0]
    tt = _tile(s, (TOK_NARROW,))

    def body(cq_ref, ckv_ref, kr_ref, qg_ref, kvg_ref, c_ref, s1_ref, s2_ref, qn_ref, kvn_ref, krp_ref):
        cq, ckv = cq_ref[...], ckv_ref[...]
        qn_ref[...] = (cq * _rstd(cq) * qg_ref[...]).astype(BF16)
        kvn_ref[...] = (ckv * _rstd(ckv) * kvg_ref[...]).astype(BF16)
        krp_ref[...] = _rope(kr_ref[...], c_ref[...], s1_ref[...], s2_ref[...]).astype(BF16)

    vec = _bs((1, BR), lambda i: (0, 0))
    rp = _bs((tt, 128), lambda i: (i, 0))
    row = _bs((tt, BR), lambda i: (i, 0))
    return _call(body, name=name, grid=(s // tt,),
                 in_specs=[_bs((tt, BR), lambda i: (i, cb_q)), _bs((tt, BR), lambda i: (i, cb_kv)),
                           _bs((tt, 128), lambda i: (i, cb_kr)), vec, vec, rp, rp, rp],
                 out_specs=[row, row, rp], out_shape=[_sds((s, BR), BF16), _sds((s, BR), BF16), _sds((s, 128), BF16)],
                 sem=("parallel",))(z, z, z, qg, kvg, c_t, s1_t, s2_t)


def _mla_prep_bwd(z, cb_q, cb_kv, d_qn, d_kvn, d_krp, qg, kvg, c_t, s1_t, s2_t, name):
    s = z.shape[0]
    tt = _tile(s, (TOK_NARROW,))

    def body(cq_ref, ckv_ref, dqn_ref, dkvn_ref, dkr_ref, qg_ref, kvg_ref, c_ref, s1_ref, s2_ref,
             dcq_ref, dckv_ref, dkro_ref, dqg_ref, dkvg_ref):
        @pl.when(pl.program_id(0) == 0)
        def _():
            dqg_ref[...] = jnp.zeros_like(dqg_ref)
            dkvg_ref[...] = jnp.zeros_like(dkvg_ref)

        cq, ckv, dqn, dkvn = cq_ref[...], ckv_ref[...], dqn_ref[...], dkvn_ref[...]
        rq, rkv = _rstd(cq), _rstd(ckv)
        dcq_ref[...] = _rms_bwd(cq, rq, dqn * qg_ref[...]).astype(BF16)
        dckv_ref[...] = _rms_bwd(ckv, rkv, dkvn * kvg_ref[...]).astype(BF16)
        dqg_ref[...] += _colsum(dqn * cq * rq)
        dkvg_ref[...] += _colsum(dkvn * ckv * rkv)
        dkro_ref[...] = _rope_bwd(dkr_ref[...], c_ref[...], s1_ref[...], s2_ref[...]).astype(BF16)

    vec = _bs((1, BR), lambda i: (0, 0))
    rp = _bs((tt, 128), lambda i: (i, 0))
    row = _bs((tt, BR), lambda i: (i, 0))
    return _call(body, name=name, grid=(s // tt,),
                 in_specs=[_bs((tt, BR), lambda i: (i, cb_q)), _bs((tt, BR), lambda i: (i, cb_kv)),
                           row, row, rp, vec, vec, rp, rp, rp],
                 out_specs=[row, row, rp, vec, vec],
                 out_shape=[_sds((s, BR), BF16), _sds((s, BR), BF16), _sds((s, 128), BF16),
                            _sds((1, BR), F32), _sds((1, BR), F32)],
                 sem=("arbitrary",))(z, z, d_qn, d_kvn, d_krp, qg, kvg, c_t, s1_t, s2_t)


def _q_rope(q, c_t, s1_t, s2_t, name):
    s, n = q.shape
    tt = _tile(s, (TOK_WIDE,))

    def body(q_ref, c_ref, s1_ref, s2_ref, out_ref):
        c_v, s1_v, s2_v = c_ref[...], s1_ref[...], s2_ref[...]
        for h in range(HEADS):
            b0 = h * 256
            out_ref[:, b0:b0 + 128] = q_ref[:, b0:b0 + 128].astype(BF16)
            out_ref[:, b0 + 128:b0 + 256] = _rope(q_ref[:, b0 + 128:b0 + 256], c_v, s1_v, s2_v).astype(BF16)

    rp = _bs((tt, 128), lambda i: (i, 0))
    row = _bs((tt, n), lambda i: (i, 0))
    return _call(body, name=name, grid=(s // tt,), in_specs=[row, rp, rp, rp], out_specs=row,
                 out_shape=_sds((s, n), BF16), sem=("parallel",))(q, c_t, s1_t, s2_t)


def _att_scores(q_ref, kn_ref, kr_ref, diagonal, t):
    q = q_ref[...]
    sc = (_dot_nt(q[:, :128], kn_ref[...]) + _dot_nt(q[:, 128:], kr_ref[...])) * ATT_SCALE
    if not diagonal:
        return sc
    row = lax.broadcasted_iota(jnp.int32, (t, t), 0)
    col = lax.broadcasted_iota(jnp.int32, (t, t), 1)
    return jnp.where(col <= row, sc, -1e30)


def _on_causal_pairs(q_tile, k_tile, step):
    @pl.when(k_tile < q_tile)
    def _():
        step(False)

    @pl.when(k_tile == q_tile)
    def _():
        step(True)


def _flash_fwd(qb, kv, krp, name):
    s = qb.shape[0]
    t = _tile(s, (ATT_TILE,))
    nq = s // t

    def body(q_ref, kn_ref, v_ref, kr_ref, o_ref, lse_ref, m_s, l_s, acc):
        i, j = pl.program_id(1), pl.program_id(2)

        @pl.when(j == 0)
        def _():
            m_s[...] = jnp.full_like(m_s, -1e30)
            l_s[...] = jnp.zeros_like(l_s)
            acc[...] = jnp.zeros_like(acc)

        def step(diagonal):
            sc = _att_scores(q_ref, kn_ref, kr_ref, diagonal, t)
            m_new = jnp.maximum(m_s[...], jnp.max(sc, axis=-1, keepdims=True))
            p = jnp.exp(sc - m_new)
            alpha = jnp.exp(m_s[...] - m_new)
            l_s[...] = alpha * l_s[...] + jnp.sum(p, axis=-1, keepdims=True)
            acc[...] = alpha * acc[...] + _dot(p.astype(BF16), v_ref[...])
            m_s[...] = m_new

        _on_causal_pairs(i, j, step)

        @pl.when(j == i)
        def _():
            o_ref[...] = (acc[...] / l_s[...]).astype(BF16)
            lse_ref[...] = m_s[...] + jnp.log(l_s[...])

    kmap = lambda off: (lambda h, i, j: (jnp.minimum(j, i), 2 * h + off))
    return _call(body, name=name, grid=(HEADS, nq, nq),
                 in_specs=[_bs((t, 256), lambda h, i, j: (i, h)), _bs((t, 128), kmap(0)), _bs((t, 128), kmap(1)),
                           _bs((t, 128), lambda h, i, j: (jnp.minimum(j, i), 0))],
                 out_specs=[_bs((t, 128), lambda h, i, j: (i, h)), _bs((None, t, 1), lambda h, i, j: (h, i, 0))],
                 out_shape=[_sds((s, HEADS * V_DIM), BF16), _sds((HEADS, s, 1), F32)],
                 scratch=[pltpu.VMEM((t, 1), F32), pltpu.VMEM((t, 1), F32), pltpu.VMEM((t, 128), F32)],
                 sem=("parallel", "parallel", "arbitrary"))(qb, kv, kv, krp)


def _flash_bwd_dq(qb, kv, krp, o, d_o, lse, c_t, s1_t, s2_t, name):
    s = qb.shape[0]
    t = _tile(s, (ATT_TILE,))
    nq = s // t

    def body(q_ref, kn_ref, v_ref, kr_ref, o_ref, do_ref, lse_ref, c_ref, s1_ref, s2_ref, dq_ref, dqn_s, dqr_s, dl_s):
        i, j = pl.program_id(1), pl.program_id(2)

        @pl.when(j == 0)
        def _():
            dqn_s[...] = jnp.zeros_like(dqn_s)
            dqr_s[...] = jnp.zeros_like(dqr_s)
            dl_s[...] = jnp.sum(do_ref[...].astype(F32) * o_ref[...].astype(F32), axis=-1, keepdims=True)

        def step(diagonal):
            p = jnp.exp(_att_scores(q_ref, kn_ref, kr_ref, diagonal, t) - lse_ref[...])
            dp = _dot_nt(do_ref[...], v_ref[...])
            ds = (p * (dp - dl_s[...]) * ATT_SCALE).astype(BF16)
            dqn_s[...] += _dot(ds, kn_ref[...])
            dqr_s[...] += _dot(ds, kr_ref[...])

        _on_causal_pairs(i, j, step)

        @pl.when(j == i)
        def _():
            dq_ref[:, :128] = dqn_s[...].astype(BF16)
            dq_ref[:, 128:] = _rope_bwd(dqr_s[...], c_ref[...], s1_ref[...], s2_ref[...]).astype(BF16)

    kmap = lambda off: (lambda h, i, j: (jnp.minimum(j, i), 2 * h + off))
    qrow = _bs((t, 128), lambda h, i, j: (i, h))
    rp = _bs((t, 128), lambda h, i, j: (i, 0))
    return _call(body, name=name, grid=(HEADS, nq, nq),
                 in_specs=[_bs((t, 256), lambda h, i, j: (i, h)), _bs((t, 128), kmap(0)), _bs((t, 128), kmap(1)),
                           _bs((t, 128), lambda h, i, j: (jnp.minimum(j, i), 0)), qrow, qrow,
                           _bs((None, t, 1), lambda h, i, j: (h, i, 0)), rp, rp, rp],
                 out_specs=_bs((t, 256), lambda h, i, j: (i, h)), out_shape=_sds((s, HEADS * 256), BF16),
                 scratch=[pltpu.VMEM((t, 128), F32), pltpu.VMEM((t, 128), F32), pltpu.VMEM((t, 1), F32)],
                 sem=("parallel", "parallel", "arbitrary"))(qb, kv, kv, krp, o, d_o, lse, c_t, s1_t, s2_t)


def _flash_bwd_dkv(qb, kv, krp, o, d_o, lse, name):
    s = qb.shape[0]
    t = _tile(s, (ATT_TILE,))
    nq = s // t

    def body(q_ref, kn_ref, v_ref, kr_ref, o_ref, do_ref, lse_ref, dkv_ref, dkr_ref, dk_s, dv_s, dkr_s):
        j, h, i = pl.program_id(0), pl.program_id(1), pl.program_id(2)

        @pl.when(i == j)
        def _():
            dk_s[...] = jnp.zeros_like(dk_s)
            dv_s[...] = jnp.zeros_like(dv_s)

        @pl.when(jnp.logical_and(i == j, h == 0))
        def _():
            dkr_s[...] = jnp.zeros_like(dkr_s)

        def step(diagonal):
            q, do = q_ref[...], do_ref[...]
            p = jnp.exp(_att_scores(q_ref, kn_ref, kr_ref, diagonal, t) - lse_ref[...])
            delta = jnp.sum(do.astype(F32) * o_ref[...].astype(F32), axis=-1, keepdims=True)
            dv_s[...] += _dot_tn(p.astype(BF16), do)
            ds = (p * (_dot_nt(do, v_ref[...]) - delta) * ATT_SCALE).astype(BF16)
            dk_s[...] += _dot_tn(ds, q[:, :128])
            dkr_s[...] += _dot_tn(ds, q[:, 128:])

        _on_causal_pairs(i, j, step)

        @pl.when(i == nq - 1)
        def _():
            dkv_ref[:, :128] = dk_s[...].astype(BF16)
            dkv_ref[:, 128:] = dv_s[...].astype(BF16)

        @pl.when(jnp.logical_and(i == nq - 1, h == HEADS - 1))
        def _():
            dkr_ref[...] = dkr_s[...]

    qi = lambda j, h, i: jnp.maximum(i, j)
    qrow = _bs((t, 128), lambda j, h, i: (qi(j, h, i), h))
    return _call(body, name=name, grid=(nq, HEADS, nq),
                 in_specs=[_bs((t, 256), lambda j, h, i: (qi(j, h, i), h)),
                           _bs((t, 128), lambda j, h, i: (j, 2 * h)), _bs((t, 128), lambda j, h, i: (j, 2 * h + 1)),
                           _bs((t, 128), lambda j, h, i: (j, 0)), qrow, qrow,
                           _bs((None, t, 1), lambda j, h, i: (h, qi(j, h, i), 0))],
                 out_specs=[_bs((t, 256), lambda j, h, i: (j, h)), _bs((t, 128), lambda j, h, i: (j, 0))],
                 out_shape=[_sds((s, HEADS * 256), BF16), _sds((s, 128), F32)],
                 scratch=[pltpu.VMEM((t, 128), F32), pltpu.VMEM((t, 128), F32), pltpu.VMEM((t, 128), F32)],
                 sem=("parallel", "arbitrary", "arbitrary"))(qb, kv, kv, krp, o, d_o, lse)


def _merge_tiles(s, projs):
    return _tile(s, (TOK_WIDE,)), _div_tile(projs[0].shape[2], 512, 128)


def _merge_specs(s, d, tt, tn):
    nb, npb = d // tn, d // N_CHIPS // tn
    br = lambda w: _bs((tt, w), lambda i, n: (i, 0))
    pw = lambda k: _bs((None, k, tn), lambda i, n: (n // npb, 0, n % npb))
    gate = lambda g: _bs((tt, tn), lambda i, n: (i, g * nb + n))
    return [br(BR), br(BR), br(BR), br(HEADS * V_DIM), pw(BR), pw(BR), pw(BR), pw(HEADS * V_DIM)] + \
           [gate(g) for g in range(4)]


def _merge_fwd(z, branches, projs, name):
    s, d = z.shape[0], N_CHIPS * projs[0].shape[2]
    tt, tn = _merge_tiles(s, projs)

    def body(*refs):
        b_refs, p_refs, g_refs, out_ref = refs[0:4], refs[4:8], refs[8:12], refs[12]
        acc = None
        for b_ref, p_ref, g_ref in zip(b_refs, p_refs, g_refs):
            term = _sigmoid(g_ref[...]) * _dot(b_ref[...], p_ref[...])
            acc = term if acc is None else acc + term
        out_ref[...] = acc.astype(BF16)

    return _call(body, name=name, grid=(s // tt, d // tn), in_specs=_merge_specs(s, d, tt, tn),
                 out_specs=_bs((tt, tn), lambda i, n: (i, n)), out_shape=_sds((s, d), BF16),
                 sem=("parallel", "parallel"))(*branches, *projs, z, z, z, z)


def _merge_bwd(z, branches, projs, d_merged, name):
    s, d = z.shape[0], N_CHIPS * projs[0].shape[2]
    tt, tn = _merge_tiles(s, projs)

    def body(*refs):
        b_refs, p_refs, g_refs, dm_ref = refs[0:4], refs[4:8], refs[8:12], refs[12]
        dy_refs, dg_refs = refs[13:17], refs[17:21]
        dm = dm_ref[...]
        for b_ref, p_ref, g_ref, dy_ref, dg_ref in zip(b_refs, p_refs, g_refs, dy_refs, dg_refs):
            sg = _sigmoid(g_ref[...])
            dy_ref[...] = (dm * sg).astype(BF16)
            dg_ref[...] = (dm * _dot(b_ref[...], p_ref[...]) * sg * (1.0 - sg)).astype(BF16)

    tile = _bs((tt, tn), lambda i, n: (i, n))
    outs = _call(body, name=name, grid=(s // tt, d // tn), in_specs=_merge_specs(s, d, tt, tn) + [tile],
                 out_specs=[tile] * 8, out_shape=[_sds((s, d), BF16)] * 8,
                 sem=("parallel", "parallel"))(*branches, *projs, z, z, z, z, d_merged)
    return outs[:4], outs[4:]


def _ew_rows(rows, cols, align=16):
    lanes = -(-cols // 128) * 128
    return _div_tile(rows, max(EW_BLOCK_BYTES // (lanes * 4), align), align)


def _add_pair(a, b, name):
    n, r, c = a.shape
    tr = _ew_rows(r, c)

    def body(a_ref, b_ref, o_ref):
        o_ref[...] = (a_ref[...].astype(F32) + b_ref[...].astype(F32)).astype(BF16)

    blk = _bs((None, tr, c), lambda k, i: (k, i, 0))
    return _call(body, name=name, grid=(n, r // tr), in_specs=[blk, blk], out_specs=blk,
                 out_shape=_sds((n, r, c), BF16), sem=("parallel", "parallel"))(a, b)


def _sum_chips(land, own, name):
    n, r, c = land.shape
    tr = _ew_rows(r, c)

    def body(l_ref, o_ref, out_ref):
        acc = l_ref[0].astype(F32)
        for k in range(1, n):
            acc = acc + l_ref[k].astype(F32)
        out_ref[...] = acc + o_ref[...].astype(F32)

    blk = _bs((tr, c), lambda i: (i, 0))
    return _call(body, name=name, grid=(r // tr,), in_specs=[_bs((n, tr, c), lambda i: (0, i, 0)), blk],
                 out_specs=blk, out_shape=_sds((r, c), F32), sem=("parallel",))(land, own)


def _sum_slots(buf, name):
    n, r, c = buf.shape
    tr = _div_tile(r, max(EW_BLOCK_BYTES // (c * 4 * n), 8), 8)

    def body(b_ref, o_ref):
        acc = b_ref[0].astype(F32)
        for k in range(1, n):
            acc = acc + b_ref[k].astype(F32)
        o_ref[...] = acc

    return _call(body, name=name, grid=(r // tr,), in_specs=[_bs((n, tr, c), lambda i: (0, i, 0))],
                 out_specs=_bs((tr, c), lambda i: (i, 0)), out_shape=_sds((r, c), F32), sem=("parallel",))(buf)


def _adam_update(w, g, m, v):
    nm = ADAM_B1 * m + (1.0 - ADAM_B1) * g
    nv = ADAM_B2 * v + (1.0 - ADAM_B2) * jnp.square(g)
    m_hat = nm / (1.0 - ADAM_B1 ** ADAM_STEP)
    v_hat = nv / (1.0 - ADAM_B2 ** ADAM_STEP)
    return -ADAM_LR * (m_hat / (jnp.sqrt(v_hat) + ADAM_EPS) + ADAM_WD * w), nm, nv


def _adamw(w, g, m, v, name):
    shape = w.shape
    cols = shape[-1]
    rows = 1
    for dim in shape[:-1]:
        rows *= dim
    w2, g2, m2, v2 = (t.reshape(rows, cols) for t in (w, g, m, v))
    tr = _ew_rows(rows, cols, align=8)

    def body(w_ref, g_ref, m_ref, v_ref, d_ref, nm_ref, nv_ref):
        d_ref[...], nm_ref[...], nv_ref[...] = _adam_update(w_ref[...], g_ref[...], m_ref[...], v_ref[...])

    blk = _bs((tr, cols), lambda i: (i, 0))
    outs = _call(body, name=name, grid=(rows // tr,), in_specs=[blk] * 4, out_specs=[blk] * 3,
                 out_shape=[_sds((rows, cols), F32)] * 3, sem=("parallel",))(w2, g2, m2, v2)
    return [o.reshape(shape) for o in outs]


def _adamw_big(w, mine, other, m, v, name):
    depth, r, c = w.shape
    tr = _ew_rows(r // 2, c, align=8)
    hb = (r // 2) // tr

    def body(w_ref, m_ref, v_ref, *rest):
        mine_refs, other_refs = rest[:depth], rest[depth:2 * depth]
        g_out, d_ref, nm_ref, nv_ref = rest[2 * depth:]
        layer, blk_i = pl.program_id(0), pl.program_id(1)
        is_mine = (blk_i // hb) == lax.axis_index("c")
        gv = jnp.where(is_mine, mine_refs[0][...], other_refs[0][...])
        for k in range(1, depth):
            gv = jnp.where(layer == k, jnp.where(is_mine, mine_refs[k][...], other_refs[k][...]), gv)
        g_out[...] = gv
        d_ref[...], nm_ref[...], nv_ref[...] = _adam_update(w_ref[...], gv, m_ref[...], v_ref[...])

    blk = _bs((None, tr, c), lambda l, i: (l, i, 0))
    g_spec = lambda k: _bs((tr, c), lambda l, i: (jnp.where(l == k, i % hb, jnp.where(l < k, 0, hb - 1)), 0))
    return _call(body, name=name, grid=(depth, 2 * hb),
                 in_specs=[blk] * 3 + [g_spec(k) for k in range(depth)] * 2,
                 out_specs=[blk] * 4, out_shape=[_sds((depth, r, c), F32)] * 4,
                 sem=("arbitrary", "arbitrary"))(w, m, v, *mine, *other)


ANY = pl.BlockSpec(memory_space=pl.ANY)
HBM = pl.BlockSpec(memory_space=pltpu.HBM)
SEM = pl.BlockSpec(memory_space=pltpu.SEMAPHORE)
DATAFLOW = pltpu.SideEffectType.DATAFLOW_SIDE_EFFECTING
DMA_SEMS = pltpu.SemaphoreType.DMA


def _place():
    return lax.axis_index("x"), lax.axis_index("y"), lax.axis_index("c")


def _other_chips(x, y):
    return [(1 - x, y), (x, 1 - y), (1 - x, 1 - y)]


def _half_rows(rows, c):
    half = rows // 2
    assert half % 16 == 0
    return pl.ds(pl.multiple_of(c * half, 16), half)


def _in_hbm(a):
    return pltpu.with_memory_space_constraint(a, pltpu.HBM)


def _ici_copies(mode, src, land, send_sems, recv_sems):
    x, y, c = _place()
    my = 2 * x + y
    out = []
    for i in range(len(src)):
        for j, chip in enumerate(_other_chips(x, y)):
            peer = 2 * chip[0] + chip[1]
            if mode == 'gather':
                rows = _half_rows(src[i].shape[0], c)
                s_ref, d_send, d_recv = src[i].at[rows], land[i].at[my, rows], land[i].at[peer, rows]
            else:
                s_ref, d_send, d_recv = src[i].at[peer], land[i].at[j], land[i].at[j]
            pair = [pltpu.make_async_remote_copy(src_ref=s_ref, dst_ref=dst, send_sem=send_sems.at[3 * i + j],
                                                 recv_sem=recv_sems.at[3 * i + j], device_id=(chip[0], chip[1], c),
                                                 device_id_type=MESH) for dst in (d_send, d_recv)]
            out.append(pair)
    return out


def _ici_start(mode, srcs, land_shapes, name):
    n = len(srcs)
    lands = [_in_hbm(lax.empty(shp, s.dtype)) for shp, s in zip(land_shapes, srcs)]

    def body(*refs):
        src, land, send_sems, recv_sems, token = refs[:n], refs[n:2 * n], refs[2 * n], refs[2 * n + 1], refs[-1]
        for send, _ in _ici_copies(mode, src, land, send_sems, recv_sems):
            send.start()
        token[...] = jnp.zeros_like(token)

    outs = pl.pallas_call(
        body, name=name,
        out_shape=(DMA_SEMS((3 * n,)), DMA_SEMS((3 * n,)), *[pltpu.HBM(s.shape, s.dtype) for s in srcs],
                   *[pltpu.HBM(l.shape, l.dtype) for l in lands], _sds((8, 128), F32)),
        in_specs=[HBM] * (2 * n), out_specs=(SEM, SEM, *[HBM] * (2 * n), pl.BlockSpec(memory_space=pltpu.VMEM)),
        input_output_aliases={i: 2 + i for i in range(2 * n)},
        compiler_params=pltpu.CompilerParams(has_side_effects=DATAFLOW))(*[_in_hbm(s) for s in srcs], *lands)
    return outs[0], outs[1], list(outs[2:2 + n]), list(outs[2 + n:2 + 2 * n]), outs[-1]


def _ici_wait(mode, started, after, name):
    send_sems, recv_sems, src_thru, land_thru, _ = started
    n = len(src_thru)

    def body(*refs):
        src, land, send_s, recv_s = refs[:n], refs[n:2 * n], refs[2 * n], refs[2 * n + 1]
        for send, recv in _ici_copies(mode, src, land, send_s, recv_s):
            send.wait_send()
            recv.wait_recv()

    outs = pl.pallas_call(
        body, name=name, out_shape=tuple(pltpu.HBM(t.shape, t.dtype) for t in src_thru + land_thru),
        in_specs=[HBM] * (2 * n) + [SEM, SEM, ANY], out_specs=(HBM,) * (2 * n),
        input_output_aliases={i: i for i in range(2 * n)},
        compiler_params=pltpu.CompilerParams(has_side_effects=DATAFLOW))(*src_thru, *land_thru, send_sems, recv_sems,
                                                                        after)
    return list(outs[n:])


def _sibling_call(body, name, inputs, out_shape, n_copies, n_local, aliases=None):
    return pl.pallas_call(body, name=name, out_shape=out_shape, in_specs=[ANY] * len(inputs),
                          out_specs=[ANY] * len(out_shape), input_output_aliases=aliases or {},
                          scratch_shapes=[DMA_SEMS((n_copies,)), DMA_SEMS((n_copies,)), DMA_SEMS((n_local,))])(*inputs)


def _share_weights(srcs, lands, name):
    n = len(srcs)

    def body(*refs):
        src, land_in, land = refs[:n], refs[n:2 * n], refs[2 * n:3 * n]
        send_sems, recv_sems, _ = refs[3 * n:3 * n + 3]
        x, y, c = _place()
        my, sib = 2 * x + y, (x, y, 1 - c)

        def d2d(k, from_ref, to_ref):
            return pltpu.make_async_remote_copy(src_ref=from_ref, dst_ref=to_ref, send_sem=send_sems.at[k],
                                                recv_sem=recv_sems.at[k], device_id=sib, device_id_type=MESH)

        sends, arrivals = [], []
        for i in range(n):
            mine, theirs = _half_rows(src[i].shape[0], c), _half_rows(src[i].shape[0], 1 - c)
            for j, chip in enumerate(_other_chips(x, y)):
                peer = 2 * chip[0] + chip[1]
                sends.append(d2d(4 * i + j, land_in[i].at[peer, mine], land[i].at[peer, mine]))
                arrivals.append(d2d(4 * i + j, land_in[i].at[peer, theirs], land[i].at[peer, theirs]))
            sends.append(d2d(4 * i + 3, src[i], land[i].at[my]))
            arrivals.append(d2d(4 * i + 3, src[i], land[i].at[my]))
        for cp in sends:
            cp.start()
        for cp in arrivals:
            cp.wait_recv()
        for cp in sends:
            cp.wait_send()

    return list(_sibling_call(body, name, list(srcs) + list(lands), [_sds(l.shape, l.dtype) for l in lands], 4 * n, 1,
                              aliases={n + i: i for i in range(n)}))


def _swap_grad_halves(gs, name):
    n = len(gs)

    def body(*refs):
        g, got = refs[:n], refs[n:2 * n]
        send_sems, recv_sems, _ = refs[2 * n:2 * n + 3]
        x, y, c = _place()
        sends = [pltpu.make_async_remote_copy(src_ref=g[i].at[:, _half_rows(g[i].shape[1], 1 - c)], dst_ref=got[i],
                                              send_sem=send_sems.at[i], recv_sem=recv_sems.at[i],
                                              device_id=(x, y, 1 - c), device_id_type=MESH) for i in range(n)]
        for cp in sends:
            cp.start()
        for cp in sends:
            cp.wait()

    half = [_sds((g.shape[0], g.shape[1] // 2, g.shape[2]), g.dtype) for g in gs]
    return list(_sibling_call(body, name, list(gs), half, n, 1))


def _share_final(fins, name):
    n = len(fins)

    def body(*refs):
        fin, other = refs[:n], refs[n:2 * n]
        send_sems, recv_sems, _ = refs[2 * n:2 * n + 3]
        x, y, c = _place()
        sends = [pltpu.make_async_remote_copy(src_ref=fin[i], dst_ref=other[i], send_sem=send_sems.at[i],
                                              recv_sem=recv_sems.at[i], device_id=(x, y, 1 - c), device_id_type=MESH)
                 for i in range(n)]
        for cp in sends:
            cp.start()
        for cp in sends:
            cp.wait()

    return list(_sibling_call(body, name, list(fins), [_sds(f.shape, f.dtype) for f in fins], n, 1))


def _gather_all(buf, name):
    flips = [(fx, fy, fc) for fx in (0, 1) for fy in (0, 1) for fc in (0, 1) if (fx, fy, fc) != (0, 0, 0)]

    def body(src, dst, send_sems, recv_sems, loc_sem):
        x, y, c = _place()
        me = 4 * x + 2 * y + c
        loc = pltpu.make_async_copy(src, dst.at[me], loc_sem.at[0])
        loc.start()
        peers = [(1 - x if fx else x, 1 - y if fy else y, 1 - c if fc else c) for fx, fy, fc in flips]

        def cp(j, peer, slot):
            return pltpu.make_async_remote_copy(src_ref=src, dst_ref=dst.at[slot], send_sem=send_sems.at[j],
                                                recv_sem=recv_sems.at[j], device_id=peer, device_id_type=MESH)

        sends = [cp(j, peer, me) for j, peer in enumerate(peers)]
        for s_ in sends:
            s_.start()
        for j, peer in enumerate(peers):
            cp(j, peer, 4 * peer[0] + 2 * peer[1] + peer[2]).wait_recv()
        for s_ in sends:
            s_.wait_send()
        loc.wait()

    return _sibling_call(body, name, [buf], [_sds((8,) + buf.shape, buf.dtype)], 7, 1)[0]


def _layout(shapes):
    out, r0 = [], 0
    for name, shape in shapes:
        n = 1
        for dim in shape:
            n *= dim
        nr = -(-n // PACK_COLS)
        nr = -(-nr // PACK_ROW_ALIGN) * PACK_ROW_ALIGN
        out.append((name, tuple(shape), r0, nr))
        r0 += nr
    return out, r0


def _pack(layout, get, dtype):
    parts = []
    for name, shape, _, nr in layout:
        flat = get(name).astype(dtype).reshape(-1)
        pad = nr * PACK_COLS - flat.shape[0]
        if pad:
            flat = jnp.pad(flat, (0, pad))
        parts.append(flat.reshape(nr, PACK_COLS))
    return jnp.concatenate(parts, axis=0)


def _unpack(layout, buf):
    out = {}
    for name, shape, r0, nr in layout:
        n = 1
        for dim in shape:
            n *= dim
        out[name] = buf[r0:r0 + nr].reshape(-1)[:n].reshape(shape)
    return out


def _rope_tables(positions):
    inv_freq = ROPE_THETA ** (-jnp.arange(0, QK_ROPE, 2, dtype=F32) / QK_ROPE)
    ang = positions.astype(F32)[:, None] * inv_freq
    cos, sin, zero = jnp.cos(ang), jnp.sin(ang), jnp.zeros_like(ang)
    c_t = jnp.concatenate([cos, cos, zero, zero], axis=1)
    s1_t = jnp.concatenate([-sin, zero, zero, zero], axis=1)
    s2_t = jnp.concatenate([zero, sin, zero, zero], axis=1)
    return c_t, s1_t, s2_t


def _shard_cols(shards, lo, hi):
    ws = shards.shape[2]
    out = []
    for k in range(shards.shape[0]):
        a, b = max(lo, k * ws), min(hi, (k + 1) * ws)
        if a < b:
            out.append(shards[k][:, a - k * ws:b - k * ws])
    return out


def _w_in_to_kernel_layout(shards, nbc, nz):
    d, n_in = shards.shape[1], N_CHIPS * shards.shape[2]
    return jnp.concatenate(_shard_cols(shards, nbc, n_in) + _shard_cols(shards, 0, nbc)
                           + [jnp.zeros((d, nz - n_in), shards.dtype)], axis=1)


def _w_in_grad_to_shards(gx, nbc, n_in):
    ws, n_gate = n_in // N_CHIPS, n_in - nbc
    out = []
    for k in range(N_CHIPS):
        lo, hi, parts = k * ws, (k + 1) * ws, []
        if lo < nbc:
            parts.append(gx[:, n_gate + lo:n_gate + min(hi, nbc)])
        if hi > nbc:
            parts.append(gx[:, max(lo, nbc) - nbc:hi - nbc])
        out.append(parts[0] if len(parts) == 1 else jnp.concatenate(parts, axis=1))
    return jnp.stack(out)


def _w_uq_to_kernel_layout(shards):
    full = jnp.concatenate([shards[k] for k in range(N_CHIPS)], axis=1).reshape(-1, HEADS, QK_NOPE + QK_ROPE)
    pad = jnp.zeros((full.shape[0], HEADS, 256 - QK_NOPE - QK_ROPE), full.dtype)
    return jnp.concatenate([full, pad], axis=2).reshape(-1, HEADS * 256)


def _w_uq_grad_to_shards(gx):
    full = gx.reshape(-1, HEADS, 256)[:, :, :QK_NOPE + QK_ROPE].reshape(gx.shape[0], -1)
    ws = full.shape[1] // N_CHIPS
    return jnp.stack([full[:, k * ws:(k + 1) * ws] for k in range(N_CHIPS)])


def _layer_forward(l, x_in, h, w_in_x, other_weights, sm, tabs, col):
    c_t, s1_t, s2_t = tabs
    tag = f"l{l}"
    z = _mm(h, w_in_x, name=f"z_{tag}")
    z, w = other_weights(z)
    w = dict(w, w_in_x=w_in_x)
    b_pool = _pool_fwd(z, col['pool'], sm['pool_w16'], sm['pool_scale'], f"pool_fwd_{tag}")
    b_conv, ypre = _conv_fwd(z, col['conv'], sm['conv_w'], sm['conv_b'], sm['conv_norm_g'], sm['conv_norm_b'],
                             f"conv_fwd_{tag}")
    b_sgu = _sgu_fwd(z, col['sgu'], sm['sgu_norm_g'], sm['sgu_norm_b'], sm['sgu_w16'], sm['sgu_bias_full'],
                     f"sgu_fwd_{tag}")
    qn, kvn, krp = _mla_prep(z, col['q'], col['kv'], col['kr'], sm['q_norm_g'], sm['kv_norm_g'], c_t, s1_t, s2_t,
                             f"mla_prep_{tag}")
    q = _mm(qn, w['w_uq_x'], name=f"q_{tag}")
    qb = _q_rope(q, c_t, s1_t, s2_t, f"q_rope_{tag}")
    kv = _mm(kvn, w['w_ukv'], name=f"kv_{tag}", b_split=True, out_dtypes=(BF16,))
    o, lse = _flash_fwd(qb, kv, krp, f"flash_fwd_{tag}")
    branches = (b_pool, b_conv, b_sgu, o)
    projs = (w['pool_proj'], w['conv_proj'], w['sgu_proj'], w['attn_proj'])
    merged = _merge_fwd(z, branches, projs, f"merge_fwd_{tag}")
    o2 = _mm(merged, w['w_out'], name=f"o2_{tag}")
    x1, h2 = _resid_norm_fwd(x_in, o2, sm['post_mix_g'], sm['pre_mlp_g'], f"mix_out_{tag}")
    u, a = _mm(h2, w['w_up'], name=f"up_{tag}", b_split=True, out_dtypes=(F32, BF16),
               epilogue=lambda acc: (acc, jnp.square(jnp.maximum(acc, 0.0))))
    f = _mm(a, w['w_down'], name=f"down_{tag}")
    return dict(x_in=x_in, h=h, z=z, ypre=ypre, branches=branches, projs=projs, qn=qn, kvn=kvn, krp=krp, qb=qb, kv=kv,
                o=o, lse=lse, merged=merged, o2=o2, x1=x1, h2=h2, u=u, a=a, f=f), w


def _layer_backward(l, sv, d_f, d_x2, w, sm, tabs, col, nz, nbc, early_reduce):
    c_t, s1_t, s2_t = tabs
    tag = f"l{l}"
    s, d = sv['x_in'].shape
    gb, gs = {}, {}
    row_shards = lambda g: g.reshape(N_CHIPS, g.shape[0] // N_CHIPS, g.shape[1])
    d_u = _mm(d_f, w['w_down'], name=f"d_u_{tag}", tb=True, out_dtypes=(BF16,), extras=(sv['u'],),
              epilogue=lambda acc, u: (acc * (2.0 * jnp.maximum(u, 0.0)),))
    gb['w_down'] = row_shards(_mm(sv['a'], d_f, name=f"g_down_{tag}", ta=True, out_dtypes=(BF16,)))
    d_h2 = _mm(d_u, w['w_up'], name=f"d_h2_{tag}", tb=True, b_split=True)
    gb['w_up'] = _mm(sv['h2'], d_u, name=f"g_up_{tag}", ta=True, out_split=True, out_dtypes=(BF16,))
    d_x1, d_o2, gs['post_mix_g'], gs['pre_mlp_g'] = _resid_norm_bwd(
        d_x2, d_h2, sv['x1'], sv['o2'], sm['post_mix_g'], sm['pre_mlp_g'], f"mix_out_bwd_{tag}")
    d_merged = _mm(d_o2, w['w_out'], name=f"d_merged_{tag}", tb=True)
    gb['w_out'] = row_shards(_mm(sv['merged'], d_o2, name=f"g_out_{tag}", ta=True, out_dtypes=(BF16,)))
    d_ys, d_gates = _merge_bwd(sv['z'], sv['branches'], sv['projs'], d_merged, f"merge_bwd_{tag}")
    d_br = []
    for k, pname in enumerate(('pool_proj', 'conv_proj', 'sgu_proj', 'attn_proj')):
        last = pname == 'attn_proj'
        d_br.append(_mm(d_ys[k], w[pname], name=f"d_{pname}_in_{tag}", tb=True, b_split=True,
                        out_dtypes=(BF16 if last else F32,)))
        gb[pname] = _mm(sv['branches'][k], d_ys[k], name=f"g_{pname}_{tag}", ta=True, out_split=True,
                        out_dtypes=(BF16,))
    dz_pool, gs['pool_w'], gs['pool_scale'] = _pool_bwd(sv['z'], col['pool'], d_br[0], sm['pool_w16'],
                                                        sm['pool_scale'], f"pool_bwd_{tag}")
    dz_ca, dz_cg, gs['conv_w'], gs['conv_b'], gs['conv_norm_g'], gs['conv_norm_b'] = _conv_bwd(
        sv['z'], col['conv'], d_br[1], sv['ypre'], sm['conv_w'], sm['conv_norm_g'], sm['conv_norm_b'],
        f"conv_bwd_{tag}")
    dz_su, dz_sv, g_sgu_w, g_sgu_bfull, gs['sgu_norm_g'], gs['sgu_norm_b'] = _sgu_bwd(
        sv['z'], col['sgu'], d_br[2], sm['sgu_norm_g'], sm['sgu_norm_b'], sm['sgu_w16'], sm['sgu_bias_full'],
        f"sgu_bwd_{tag}")
    gs['sgu_w'] = g_sgu_w * sm['tril']
    gs['sgu_b'] = jnp.sum(g_sgu_bfull, axis=-1)
    d_qb = _flash_bwd_dq(sv['qb'], sv['kv'], sv['krp'], sv['o'], d_br[3], sv['lse'], c_t, s1_t, s2_t,
                         f"flash_dq_{tag}")
    d_kv, d_krp = _flash_bwd_dkv(sv['qb'], sv['kv'], sv['krp'], sv['o'], d_br[3], sv['lse'], f"flash_dkv_{tag}")
    d_qn = _mm(d_qb, w['w_uq_x'], name=f"d_qn_{tag}", tb=True)
    gb['w_uq'] = _w_uq_grad_to_shards(_mm(sv['qn'], d_qb, name=f"g_uq_{tag}", ta=True, out_dtypes=(BF16,)))
    d_kvn = _mm(d_kv, w['w_ukv'], name=f"d_kvn_{tag}", tb=True, b_split=True)
    gb['w_ukv'] = _mm(sv['kvn'], d_kv, name=f"g_ukv_{tag}", ta=True, out_split=True, out_dtypes=(BF16,))
    dz_q, dz_kv, dz_kr, gs['q_norm_g'], gs['kv_norm_g'] = _mla_prep_bwd(
        sv['z'], col['q'], col['kv'], d_qn, d_kvn, d_krp, sm['q_norm_g'] + early_reduce(gb), sm['kv_norm_g'],
        c_t, s1_t, s2_t, f"mla_prep_bwd_{tag}")
    used = 4 * d + 7 * BR + 128
    dz = jnp.concatenate(list(d_gates) + [dz_pool, dz_ca, dz_cg, dz_su, dz_sv, dz_q, dz_kv, dz_kr,
                                          jnp.zeros((s, nz - used), BF16)], axis=1)
    d_h = _mm(dz, w['w_in_x'], name=f"d_h_{tag}", tb=True)
    gb['w_in'] = _w_in_grad_to_shards(_mm(sv['h'], dz, name=f"g_in_{tag}", ta=True, out_dtypes=(BF16,)),
                                      nbc, 4 * d + nbc)
    return d_x1, d_h, gb, gs


def _reduce_start(tag, names, gb, my_c):
    gs = [gb[n] for n in names]
    sib4 = _swap_grad_halves(gs, f"swap_grads_{tag}")
    own4 = [lax.dynamic_slice_in_dim(g, my_c * (g.shape[1] // 2), g.shape[1] // 2, axis=1) for g in gs]
    partial = [_add_pair(o, g, f"add_sibling_{n}_{tag}") for n, o, g in zip(names, own4, sib4)]
    return _ici_start('scatter', partial, [(3,) + p.shape[1:] for p in partial], f"scatter_start_{tag}")


def _reduce_finish(tag, names, started, my_chip, after):
    own = [lax.dynamic_index_in_dim(p, my_chip, 0, keepdims=False) for p in started[2]]
    lands = _ici_wait('scatter', started, after, f"scatter_wait_{tag}")
    fins = [_sum_chips(ld, ow, f"sum_chips_{n}_{tag}") for n, ld, ow in zip(names, lands, own)]
    return dict(zip(names, fins)), dict(zip(names, _share_final(fins, f"share_final_{tag}")))


def kernel(x, positions, pre_mix_g, w_in, pool_w, pool_scale, pool_proj, conv_w, conv_b, conv_norm_g, conv_norm_b, conv_proj, sgu_norm_g, sgu_norm_b, sgu_w, sgu_b, sgu_proj, q_norm_g, w_uq, kv_norm_g, w_ukv, attn_proj, w_out, post_mix_g, pre_mlp_g, w_up, w_down, post_mlp_g, loss_target, m_pre_mix_g, m_w_in, m_pool_w, m_pool_scale, m_pool_proj, m_conv_w, m_conv_b, m_conv_norm_g, m_conv_norm_b, m_conv_proj, m_sgu_norm_g, m_sgu_norm_b, m_sgu_w, m_sgu_b, m_sgu_proj, m_q_norm_g, m_w_uq, m_kv_norm_g, m_w_ukv, m_attn_proj, m_w_out, m_post_mix_g, m_pre_mlp_g, m_w_up, m_w_down, m_post_mlp_g, v_pre_mix_g, v_w_in, v_pool_w, v_pool_scale, v_pool_proj, v_conv_w, v_conv_b, v_conv_norm_g, v_conv_norm_b, v_conv_proj, v_sgu_norm_g, v_sgu_norm_b, v_sgu_w, v_sgu_b, v_sgu_proj, v_q_norm_g, v_w_uq, v_kv_norm_g, v_w_ukv, v_attn_proj, v_w_out, v_post_mix_g, v_pre_mlp_g, v_w_up, v_w_down, v_post_mlp_g):
    arg = dict(locals())
    depth = pre_mix_g.shape[0]
    s, d = x.shape[1], x.shape[2]
    nbc = 7 * BR + QK_ROPE
    nz = 4 * d + -(-(7 * BR + 128) // 512) * 512
    g0 = 4 * d // BR
    col = dict(pool=g0, conv=g0 + 1, sgu=g0 + 3, q=g0 + 5, kv=g0 + 6, kr=(4 * d + 7 * BR) // 128)
    my_x, my_y, my_c = _place()
    my_chip = 2 * my_x + my_y
    x2 = x.reshape(s, d)
    target = loss_target.reshape(s, d)
    tabs = _rope_tables(positions.reshape(s))
    how_of = dict(BIG)

    rest_names = [n for n in BIG_NAMES if n != 'w_in']

    def gather_start(tag, names, l, gate):
        srcs = [arg[n][l].astype(BF16) for n in names]
        if gate is not None:
            srcs, _ = lax.optimization_barrier((srcs, gate))
        return names, srcs, _ici_start('gather', srcs, [(N_CHIPS,) + t.shape for t in srcs], f"gather_start_{tag}")

    def gather_finish(tag, group, after):
        names, srcs, started = group
        lands = _ici_wait('gather', started, after, f"gather_wait_{tag}")
        full = dict(zip(names, _share_weights(srcs, lands, f"share_weights_{tag}")))
        w_l = {n: (full[n] if how_of[n] == 'col' else full[n].reshape(-1, full[n].shape[2]))
               for n in names if n not in ('w_in', 'w_uq')}
        if 'w_in' in full:
            w_l['w_in_x'] = _w_in_to_kernel_layout(full['w_in'], nbc, nz)
        if 'w_uq' in full:
            w_l['w_uq_x'] = _w_uq_to_kernel_layout(full['w_uq'])
        return w_l

    def behind(value, group):
        return lax.optimization_barrier((value, group[2][4]))[0]

    first_in = gather_start("l0_in", ['w_in'], 0, None)

    conv_layout, _ = _layout([('conv_w', conv_w.shape)])
    conv_all = _gather_all(_pack(conv_layout, lambda n: arg[n], F32), "gather_conv_w")
    conv_w_full = jnp.concatenate([_unpack(conv_layout, conv_all[4 * (k // 2) + 2 * (k % 2)])['conv_w']
                                   for k in range(4)], axis=-1)
    tril = jnp.tril(jnp.ones((GROUP, GROUP), F32))
    smalls = []
    for l in range(depth):
        sm = {n: arg[n][l][None, :] for n in ('pre_mix_g', 'pool_scale', 'conv_b', 'conv_norm_g', 'conv_norm_b',
                                               'sgu_norm_g', 'sgu_norm_b', 'q_norm_g', 'kv_norm_g', 'post_mix_g',
                                               'pre_mlp_g', 'post_mlp_g')}
        sm['pool_w16'] = pool_w[l].astype(BF16)
        sm['sgu_w16'] = (sgu_w[l] * tril).astype(BF16)
        sm['sgu_bias_full'] = jnp.broadcast_to(sgu_b[l][:, :, None], (BR // GROUP, GROUP, GROUP))
        sm['conv_w'] = conv_w_full[l]
        sm['tril'] = tril
        smalls.append(sm)

    weights, saved, groups = [], [], {}
    x_cur = x2
    h = _rms_fwd(x2, smalls[0]['pre_mix_g'] + first_in[2][4][0, 0], "rms_in")
    w_in_x = gather_finish("l0_in", first_in, h)['w_in_x']
    groups[0] = gather_start("l0_rest", rest_names, 0, w_in_x)
    h = behind(h, groups[0])

    def start_next(l, z, gate):
        if l + 1 < depth:
            groups[l + 1] = gather_start(f"l{l + 1}", BIG_NAMES, l + 1, gate)
            z = behind(z, groups[l + 1])
        return z

    for l in range(depth):
        if l == 0:
            def other_weights(z):
                w_l = gather_finish("l0_rest", groups[0], z)
                return start_next(0, z, w_l['w_down']), w_l
        else:
            w_l = gather_finish(f"l{l}", groups[l], saved[l - 1]['f'])
            w_in_x = w_l.pop('w_in_x')

            def other_weights(z, w_l=w_l, l=l):
                return start_next(l, z, w_l['w_down']), w_l
        sv, w_full = _layer_forward(l, x_cur, h, w_in_x, other_weights, smalls[l], tabs, col)
        weights.append(w_full)
        saved.append(sv)
        if l + 1 < depth:
            x_cur, h = _resid_norm_fwd(sv['x1'], sv['f'], smalls[l]['post_mlp_g'], smalls[l + 1]['pre_mix_g'],
                                       f"mlp_out_l{l}")
    d_y, loss_local = _resid_norm_loss(saved[-1]['x1'], saved[-1]['f'], smalls[-1]['post_mlp_g'], target, "loss_head")

    reduce_groups = (("a", rest_names), ("b", ['w_in']))
    reduce_started = [None] * depth
    fin_mine, fin_other = [dict() for _ in range(depth)], [dict() for _ in range(depth)]
    grads_small = [dict() for _ in range(depth)]

    def finish_layer(l, after):
        for key, names in reduce_groups:
            mine, other = _reduce_finish(f"{key}_l{l}", names, reduce_started[l][key], my_chip, after)
            fin_mine[l].update(mine)
            fin_other[l].update(other)

    d_x2 = d_y
    _, d_f, g_post, _ = _resid_norm_bwd(d_y, None, None, saved[-1]['f'], smalls[-1]['post_mlp_g'], None,
                                        f"mlp_out_bwd_l{depth - 1}")
    grads_small[-1]['post_mlp_g'] = g_post
    grad_x = None
    for l in reversed(range(depth)):
        started_l = {}

        def early_reduce(gb, l=l, started_l=started_l):
            started_l["a"] = _reduce_start(f"a_l{l}", reduce_groups[0][1], gb, my_c)
            return started_l["a"][4][0, 0]

        d_x1, d_h, gb, gs = _layer_backward(l, saved[l], d_f, d_x2, weights[l], smalls[l], tabs, col, nz, nbc,
                                            early_reduce)
        grads_small[l].update(gs)
        started_l["b"] = _reduce_start(f"b_l{l}", reduce_groups[1][1], gb, my_c)
        reduce_started[l] = started_l
        pin = started_l["b"][4][0, 0]
        if l > 0:
            d_x2, d_f, g_post, g_pre = _resid_norm_bwd(d_x1, d_h, saved[l]['x_in'], saved[l - 1]['f'],
                                                      smalls[l - 1]['post_mlp_g'], smalls[l]['pre_mix_g'] + pin,
                                                      f"mlp_out_bwd_l{l - 1}")
            grads_small[l - 1]['post_mlp_g'] = g_post
            grads_small[l]['pre_mix_g'] = g_pre
        else:
            grad_x, _, _, g_pre = _resid_norm_bwd(d_x1, d_h, saved[0]['x_in'], None, None,
                                                  smalls[0]['pre_mix_g'] + pin, "rms_in_bwd")
            grads_small[0]['pre_mix_g'] = g_pre
        if l + 1 < depth:
            finish_layer(l + 1, d_h)

    small_shapes = []
    for n in SMALL_NAMES:
        shp = arg[n].shape[1:]
        if n == 'conv_w':
            shp = (shp[0], shp[1] * 4)
        small_shapes.append((n, shp))
    small_layout, _ = _layout([(f"{n}.{l}", shp) for l in range(depth) for n, shp in small_shapes])

    def small_get(key):
        n, l = key.rsplit('.', 1)
        return grads_small[int(l)][n]

    small_all = _gather_all(_pack(small_layout, small_get, F32), "gather_small_grads")
    g_small = _unpack(small_layout, _sum_slots(small_all, "sum_small_grads"))
    finish_layer(0, small_all)

    grad, delta, new_m, new_v = {}, {}, {}, {}
    for n in WEIGHT_NAMES:
        if n in BIG_NAMES:
            grad[n], delta[n], new_m[n], new_v[n] = _adamw_big(
                arg[n], [fin_mine[l][n] for l in range(depth)], [fin_other[l][n] for l in range(depth)],
                arg['m_' + n], arg['v_' + n], f"adamw_{n}")
            continue
        g = jnp.stack([g_small[f"{n}.{l}"] for l in range(depth)])
        if n == 'conv_w':
            wd = conv_w.shape[2]
            g = lax.dynamic_slice_in_dim(g, my_chip * wd, wd, axis=2)
        grad[n] = g
        delta[n], new_m[n], new_v[n] = _adamw(arg[n], g, arg['m_' + n], arg['v_' + n], f"adamw_{n}")

    loss = lax.psum(loss_local, ("x", "y", "c"))
    return (loss, grad_x.reshape(x.shape), *[grad[n] for n in WEIGHT_NAMES], *[delta[n] for n in WEIGHT_NAMES],
            *[new_m[n] for n in WEIGHT_NAMES], *[new_v[n] for n in WEIGHT_NAMES])
```

```python
import math

import jax
import jax.numpy as jnp
from jax import lax
from jax.experimental import pallas as pl
from jax.experimental.pallas import tpu as pltpu

F32 = jnp.float32
BF16 = jnp.bfloat16
MESH = pl.DeviceIdType.MESH

EPS = 1e-6
POOL_WINDOWS = (2, 4, 8, 16)
GROUP = 128
BR = 512
CONV_WIDTH = 31
HEADS = 8
QK_NOPE = 128
QK_ROPE = 64
V_DIM = 128
ROPE_THETA = 10000.0
ATT_SCALE = (QK_NOPE + QK_ROPE) ** -0.5
GELU_C = math.sqrt(2.0 / math.pi)
ADAM_LR, ADAM_B1, ADAM_B2, ADAM_EPS, ADAM_WD, ADAM_STEP = 0.001, 0.9, 0.999, 1e-08, 0.01, 10

VMEM_LIMIT = 48 * 1024 * 1024
PACK_COLS = 1024
PACK_ROW_ALIGN = 16
CONV_HALO = 32
POOL_HALO = 16

TOK_WIDE = 256
TOK_NARROW = 512
ATT_TILE = 1024
MM_TILE_M = 1024
MM_TILE_N = 1024
MM_TILE_K = 2048
EW_BLOCK_BYTES = 1536 * 1024
N_CHIPS = 4

WEIGHT_NAMES = ['pre_mix_g', 'w_in', 'pool_w', 'pool_scale', 'pool_proj', 'conv_w', 'conv_b', 'conv_norm_g',
                'conv_norm_b', 'conv_proj', 'sgu_norm_g', 'sgu_norm_b', 'sgu_w', 'sgu_b', 'sgu_proj', 'q_norm_g',
                'w_uq', 'kv_norm_g', 'w_ukv', 'attn_proj', 'w_out', 'post_mix_g', 'pre_mlp_g', 'w_up', 'w_down',
                'post_mlp_g']
BIG = [('w_in', 'col'), ('pool_proj', 'col'), ('conv_proj', 'col'), ('sgu_proj', 'col'), ('w_uq', 'col'),
       ('w_ukv', 'col'), ('attn_proj', 'col'), ('w_out', 'row'), ('w_up', 'col'), ('w_down', 'row')]
BIG_NAMES = [n for n, _ in BIG]
SMALL_NAMES = [n for n in WEIGHT_NAMES if n not in BIG_NAMES]


def _bs(shape, index_map):
    return pl.BlockSpec(shape, index_map)


def _sds(shape, dtype):
    return jax.ShapeDtypeStruct(shape, dtype)


def _tile(n, candidates):
    for t in candidates:
        if n % t == 0:
            return t
    return n


def _div_tile(n, target, align=8):
    t = min(n, target) // align * align
    while t >= align:
        if n % t == 0:
            return t
        t -= align
    return n


def _call(body, *, name, grid, in_specs, out_specs, out_shape, scratch=(), sem=None):
    return pl.pallas_call(
        body, name=name, grid=grid, in_specs=in_specs, out_specs=out_specs, out_shape=out_shape,
        scratch_shapes=list(scratch),
        compiler_params=pltpu.CompilerParams(dimension_semantics=sem, vmem_limit_bytes=VMEM_LIMIT))


def _sigmoid(v):
    return 1.0 / (1.0 + jnp.exp(-v))


def _gelu(v):
    return 0.5 * v * (1.0 + jnp.tanh(GELU_C * (v + 0.044715 * v * v * v)))


def _gelu_grad(v):
    t = jnp.tanh(GELU_C * (v + 0.044715 * v * v * v))
    return 0.5 * (1.0 + t) + 0.5 * v * (1.0 - t * t) * GELU_C * (1.0 + 3.0 * 0.044715 * v * v)


def _rstd(v):
    return lax.rsqrt(jnp.mean(v * v, axis=-1, keepdims=True) + EPS)


def _rms_bwd(v, r, t):
    return r * t - v * (r * r * r) * jnp.mean(v * t, axis=-1, keepdims=True)


def _ln_stats(v):
    mu = jnp.mean(v, axis=-1, keepdims=True)
    d = v - mu
    r = lax.rsqrt(jnp.mean(d * d, axis=-1, keepdims=True) + EPS)
    return d * r, r


def _ln_bwd(xh, r, dxh):
    return r * (dxh - jnp.mean(dxh, axis=-1, keepdims=True) - xh * jnp.mean(dxh * xh, axis=-1, keepdims=True))


def _colsum(v):
    return jnp.sum(v, axis=0, keepdims=True)


def _dot(a, b):
    return jnp.dot(a, b, preferred_element_type=F32)


def _dot_nt(a, b):
    return lax.dot_general(a, b, (((1,), (1,)), ((), ())), preferred_element_type=F32)


def _dot_tn(a, b):
    return lax.dot_general(a, b, (((0,), (0,)), ((), ())), preferred_element_type=F32)


def _mm(a, b, *, name, ta=False, tb=False, b_split=False, out_split=False, out_dtypes=(F32,), epilogue=None,
        extras=()):
    m = a.shape[1] if ta else a.shape[0]
    k = a.shape[0] if ta else a.shape[1]
    shard = b.shape[2] if b_split else None
    b_rows, b_cols = (b.shape[1], N_CHIPS * shard) if b_split else b.shape
    n = b_rows if tb else b_cols
    assert k == (b_cols if tb else b_rows)
    assert not (out_split and extras)
    tm = _div_tile(m, MM_TILE_M, 128)
    tn = _div_tile(n // N_CHIPS if (out_split or (b_split and not tb)) else n, MM_TILE_N, 128)
    tk = _div_tile(shard if (b_split and tb) else k, MM_TILE_K, 128)
    nk = k // tk
    n_extra, n_out = len(extras), len(out_dtypes)
    dims = (((0 if ta else 1,), (1 if tb else 0,)), ((), ()))

    def body(a_ref, b_ref, *rest):
        extra_refs, out_refs = rest[:n_extra], rest[n_extra:n_extra + n_out]
        part = lax.dot_general(a_ref[...].astype(BF16), b_ref[...].astype(BF16), dims, preferred_element_type=F32)

        def finish(res):
            res = (res,) if epilogue is None else epilogue(res, *[e[...] for e in extra_refs])
            for o, r in zip(out_refs, res):
                o[...] = r.astype(o.dtype)

        if nk == 1:
            finish(part)
        else:
            acc, kk = rest[-1], pl.program_id(2)

            @pl.when(kk == 0)
            def _():
                acc[...] = part

            @pl.when(jnp.logical_and(kk > 0, kk < nk - 1))
            def _():
                acc[...] += part

            @pl.when(kk == nk - 1)
            def _():
                finish(acc[...] + part)

    a_spec = _bs((tk, tm), lambda i, j, kk: (kk, i)) if ta else _bs((tm, tk), lambda i, j, kk: (i, kk))
    if not b_split:
        b_spec = _bs((tn, tk), lambda i, j, kk: (j, kk)) if tb else _bs((tk, tn), lambda i, j, kk: (kk, j))
    elif tb:
        kpb = shard // tk
        b_spec = _bs((None, tn, tk), lambda i, j, kk: (kk // kpb, j, kk % kpb))
    else:
        npb = shard // tn
        b_spec = _bs((None, tk, tn), lambda i, j, kk: (j // npb, kk, j % npb))
    e_spec = _bs((tm, tn), lambda i, j, kk: (i, j))
    if out_split:
        npo = (n // N_CHIPS) // tn
        o_spec = _bs((None, tm, tn), lambda i, j, kk: (j // npo, i, j % npo))
        out_shape = [_sds((N_CHIPS, m, n // N_CHIPS), dt) for dt in out_dtypes]
    else:
        o_spec, out_shape = e_spec, [_sds((m, n), dt) for dt in out_dtypes]
    outs = _call(body, name=name, grid=(m // tm, n // tn, nk), in_specs=[a_spec, b_spec] + [e_spec] * n_extra,
                 out_specs=[o_spec] * n_out, out_shape=out_shape,
                 scratch=[pltpu.VMEM((tm, tn), F32)] if nk > 1 else [],
                 sem=("parallel", "parallel", "arbitrary"))(a, b, *extras)
    return outs[0] if n_out == 1 else outs


def _rms_fwd(x, g, name):
    s, d = x.shape
    tt = _tile(s, (TOK_WIDE,))

    def body(x_ref, g_ref, h_ref):
        v = x_ref[...]
        h_ref[...] = (v * _rstd(v) * g_ref[...]).astype(BF16)

    row = _bs((tt, d), lambda i: (i, 0))
    return _call(body, name=name, grid=(s // tt,), in_specs=[row, _bs((1, d), lambda i: (0, 0))],
                 out_specs=row, out_shape=_sds((s, d), BF16), sem=("parallel",))(x, g)


def _resid_norm_fwd(xres, y, g_post, g_next, name):
    s, d = xres.shape
    tt = _tile(s, (TOK_WIDE,))

    def body(xr_ref, y_ref, gp_ref, gn_ref, xn_ref, h_ref):
        yv = y_ref[...]
        xn = xr_ref[...] + yv * _rstd(yv) * gp_ref[...]
        xn_ref[...] = xn
        h_ref[...] = (xn * _rstd(xn) * gn_ref[...]).astype(BF16)

    row, vec = _bs((tt, d), lambda i: (i, 0)), _bs((1, d), lambda i: (0, 0))
    return _call(body, name=name, grid=(s // tt,), in_specs=[row, row, vec, vec], out_specs=[row, row],
                 out_shape=[_sds((s, d), F32), _sds((s, d), BF16)], sem=("parallel",))(xres, y, g_post, g_next)


def _resid_norm_loss(xres, y, g_post, target, name):
    s, d = xres.shape
    tt = _tile(s, (TOK_WIDE,))

    def body(xr_ref, y_ref, gp_ref, t_ref, dy_ref, loss_ref):
        @pl.when(pl.program_id(0) == 0)
        def _():
            loss_ref[...] = jnp.zeros_like(loss_ref)

        yv = y_ref[...]
        err = xr_ref[...] + yv * _rstd(yv) * gp_ref[...] - t_ref[...]
        dy_ref[...] = err * (1.0 / d)
        loss_ref[...] += 0.5 * jnp.sum(jnp.mean(err * err, axis=-1, keepdims=True))

    row, vec = _bs((tt, d), lambda i: (i, 0)), _bs((1, d), lambda i: (0, 0))
    dy, loss = _call(body, name=name, grid=(s // tt,), in_specs=[row, row, vec, row],
                     out_specs=[row, _bs((8, 128), lambda i: (0, 0))],
                     out_shape=[_sds((s, d), F32), _sds((8, 128), F32)], sem=("arbitrary",))(xres, y, g_post, target)
    return dy, loss[0, 0]


def _resid_norm_bwd(d_out, d_h, x_new, y, g_post, g_next, name):
    s, d = d_out.shape
    tt = _tile(s, (TOK_WIDE,))
    has_next, has_y = d_h is not None, y is not None

    def body(*refs):
        it = iter(refs)
        do_ref = next(it)
        if has_next:
            dh_ref, xn_ref, gn_ref = next(it), next(it), next(it)
        if has_y:
            y_ref, gp_ref = next(it), next(it)
        if has_next:
            dx_ref = next(it)
        if has_y:
            dy_ref, dgp_ref = next(it), next(it)
        if has_next:
            dgn_ref = next(it)

        first = pl.program_id(0) == 0
        dx = do_ref[...]
        if has_next:
            xn, dh = xn_ref[...], dh_ref[...]
            r = _rstd(xn)
            dx = dx + _rms_bwd(xn, r, dh * gn_ref[...])
            dx_ref[...] = dx

            @pl.when(first)
            def _():
                dgn_ref[...] = jnp.zeros_like(dgn_ref)

            dgn_ref[...] += _colsum(dh * xn * r)
        if has_y:
            yv = y_ref[...]
            ry = _rstd(yv)
            dy_ref[...] = _rms_bwd(yv, ry, dx * gp_ref[...]).astype(BF16)

            @pl.when(first)
            def _():
                dgp_ref[...] = jnp.zeros_like(dgp_ref)

            dgp_ref[...] += _colsum(dx * yv * ry)

    row, vec = _bs((tt, d), lambda i: (i, 0)), _bs((1, d), lambda i: (0, 0))
    args, in_specs, out_specs, out_shape = [d_out], [row], [], []
    if has_next:
        args += [d_h, x_new, g_next]
        in_specs += [row, row, vec]
    if has_y:
        args += [y, g_post]
        in_specs += [row, vec]
    if has_next:
        out_specs.append(row)
        out_shape.append(_sds((s, d), F32))
    if has_y:
        out_specs += [row, vec]
        out_shape += [_sds((s, d), BF16), _sds((1, d), F32)]
    if has_next:
        out_specs.append(vec)
        out_shape.append(_sds((1, d), F32))
    outs = list(_call(body, name=name, grid=(s // tt,), in_specs=in_specs, out_specs=out_specs, out_shape=out_shape,
                      sem=("arbitrary",))(*args))
    d_x = outs.pop(0) if has_next else None
    d_y, d_gp = (outs.pop(0), outs.pop(0)) if has_y else (None, None)
    d_gn = outs.pop(0) if has_next else None
    return d_x, d_y, d_gp, d_gn


def _pool_counts(t0, tt, w):
    t = t0 + lax.broadcasted_iota(jnp.int32, (tt, 1), 0)
    return jnp.minimum(t + 1, w).astype(F32)


def _pool_pooled(ext, a, t0, tt, g, w):
    cols = pl.ds(g * GROUP, GROUP)
    sm = ext[pl.ds(POOL_HALO, tt), cols]
    for j in range(1, w):
        sm = sm + ext[pl.ds(POOL_HALO - j, tt), cols]
    return sm / _pool_counts(t0, tt, w) - a[:, g * GROUP:(g + 1) * GROUP]


def _pool_fwd(z, cb, pool_w, pool_scale, name):
    s = z.shape[0]
    tt = _tile(s, (TOK_NARROW,))
    hb = tt // POOL_HALO

    def body(zc_ref, zp_ref, pw_ref, sc_ref, out_ref, ext):
        i = pl.program_id(0)
        a = zc_ref[...]
        ext[pl.ds(0, POOL_HALO), :] = jnp.where(i > 0, zp_ref[...], 0.0)
        ext[pl.ds(POOL_HALO, tt), :] = a
        for g, w in enumerate(POOL_WINDOWS):
            pooled = _pool_pooled(ext, a, i * tt, tt, g, w).astype(BF16)
            mixed = _dot(pooled, pw_ref[g])
            out_ref[:, g * GROUP:(g + 1) * GROUP] = (mixed * sc_ref[:, g * GROUP:(g + 1) * GROUP]).astype(BF16)

    return _call(body, name=name, grid=(s // tt,),
                 in_specs=[_bs((tt, BR), lambda i: (i, cb)),
                           _bs((POOL_HALO, BR), lambda i: (jnp.maximum(i * hb - 1, 0), cb)),
                           _bs((len(POOL_WINDOWS), GROUP, GROUP), lambda i: (0, 0, 0)),
                           _bs((1, BR), lambda i: (0, 0))],
                 out_specs=_bs((tt, BR), lambda i: (i, 0)), out_shape=_sds((s, BR), BF16),
                 scratch=[pltpu.VMEM((tt + POOL_HALO, BR), F32)], sem=("parallel",))(z, z, pool_w, pool_scale)


def _pool_bwd(z, cb, d_b, pool_w, pool_scale, name):
    s = z.shape[0]
    tt = _tile(s, (TOK_NARROW,))
    nt, hb, ng = s // tt, tt // POOL_HALO, len(POOL_WINDOWS)

    def body(zc_ref, zp_ref, db_ref, pw_ref, sc_ref, dz_ref, dpw_ref, dsc_ref, ext, ext_e, carry):
        step = pl.program_id(0)
        i = nt - 1 - step

        @pl.when(step == 0)
        def _():
            dpw_ref[...] = jnp.zeros_like(dpw_ref)
            dsc_ref[...] = jnp.zeros_like(dsc_ref)
            carry[...] = jnp.zeros_like(carry)

        a = zc_ref[...]
        ext[pl.ds(0, POOL_HALO), :] = jnp.where(i > 0, zp_ref[...], 0.0)
        ext[pl.ds(POOL_HALO, tt), :] = a
        ext_e[pl.ds(tt, POOL_HALO), :] = carry[...]
        db = db_ref[...]
        for g, w in enumerate(POOL_WINDOWS):
            c0, c1 = g * GROUP, (g + 1) * GROUP
            pooled = _pool_pooled(ext, a, i * tt, tt, g, w).astype(BF16)
            mixed = _dot(pooled, pw_ref[g])
            dsc_ref[:, c0:c1] += _colsum(db[:, c0:c1] * mixed)
            dmixed = (db[:, c0:c1] * sc_ref[:, c0:c1]).astype(BF16)
            dpw_ref[g] += _dot_tn(pooled, dmixed)
            dpooled = _dot_nt(dmixed, pw_ref[g])
            ext_e[pl.ds(0, tt), pl.ds(c0, GROUP)] = dpooled / _pool_counts(i * tt, tt, w)
            acc = -dpooled
            for j in range(w):
                acc = acc + ext_e[pl.ds(j, tt), pl.ds(c0, GROUP)]
            dz_ref[:, c0:c1] = acc.astype(BF16)
        carry[...] = ext_e[pl.ds(0, POOL_HALO), :]

    rev = lambda st: nt - 1 - st
    dz, dpw, dsc = _call(
        body, name=name, grid=(nt,),
        in_specs=[_bs((tt, BR), lambda st: (rev(st), cb)),
                  _bs((POOL_HALO, BR), lambda st: (jnp.maximum(rev(st) * hb - 1, 0), cb)),
                  _bs((tt, BR), lambda st: (rev(st), 0)),
                  _bs((ng, GROUP, GROUP), lambda st: (0, 0, 0)),
                  _bs((1, BR), lambda st: (0, 0))],
        out_specs=[_bs((tt, BR), lambda st: (rev(st), 0)), _bs((ng, GROUP, GROUP), lambda st: (0, 0, 0)),
                   _bs((1, BR), lambda st: (0, 0))],
        out_shape=[_sds((s, BR), BF16), _sds((ng, GROUP, GROUP), F32), _sds((1, BR), F32)],
        scratch=[pltpu.VMEM((tt + POOL_HALO, BR), F32), pltpu.VMEM((tt + POOL_HALO, BR), F32),
                 pltpu.VMEM((POOL_HALO, BR), F32)],
        sem=("arbitrary",))(z, z, d_b, pool_w, pool_scale)
    return dz, dpw, dsc


def _conv_fwd(z, cb, conv_w, conv_b, ng, nb, name):
    s = z.shape[0]
    tt = _tile(s, (TOK_NARROW,))
    hb = tt // CONV_HALO

    def body(a_ref, g_ref, ap_ref, gp_ref, w_ref, b_ref, ng_ref, nb_ref, out_ref, ypre_ref, ext):
        i = pl.program_id(0)
        ext[pl.ds(0, CONV_HALO), :] = jnp.where(i > 0, ap_ref[...] * _sigmoid(gp_ref[...]), 0.0)
        ext[pl.ds(CONV_HALO, tt), :] = a_ref[...] * _sigmoid(g_ref[...])
        acc = jnp.zeros((tt, BR), F32)
        for k in range(CONV_WIDTH):
            acc = acc + w_ref[pl.ds(k, 1), :] * ext[pl.ds(CONV_HALO - (CONV_WIDTH - 1) + k, tt), :]
        ypre = acc + b_ref[...]
        ypre_ref[...] = ypre
        xh, _ = _ln_stats(ypre)
        yl = xh * ng_ref[...] + nb_ref[...]
        out_ref[...] = (yl * _sigmoid(yl)).astype(BF16)

    cur = lambda c: _bs((tt, BR), lambda i: (i, c))
    prev = lambda c: _bs((CONV_HALO, BR), lambda i: (jnp.maximum(i * hb - 1, 0), c))
    vec = _bs((1, BR), lambda i: (0, 0))
    row = _bs((tt, BR), lambda i: (i, 0))
    return _call(body, name=name, grid=(s // tt,),
                 in_specs=[cur(cb), cur(cb + 1), prev(cb), prev(cb + 1),
                           _bs((CONV_WIDTH, BR), lambda i: (0, 0)), vec, vec, vec],
                 out_specs=[row, row], out_shape=[_sds((s, BR), BF16), _sds((s, BR), F32)],
                 scratch=[pltpu.VMEM((tt + CONV_HALO, BR), F32)], sem=("parallel",))(
        z, z, z, z, conv_w, conv_b, ng, nb)


def _conv_bwd(z, cb, d_b, ypre, conv_w, ng, nb, name):
    s = z.shape[0]
    tt = _tile(s, (TOK_NARROW,))
    nt, hb = s // tt, tt // CONV_HALO
    lead = CONV_HALO - (CONV_WIDTH - 1)

    def body(a_ref, g_ref, ap_ref, gp_ref, db_ref, yp_ref, w_ref, ng_ref, nb_ref,
             da_ref, dg_ref, dw_ref, dcb_ref, dng_ref, dnb_ref, ext, ext_d, carry):
        step = pl.program_id(0)
        i = nt - 1 - step

        @pl.when(step == 0)
        def _():
            dw_ref[...] = jnp.zeros_like(dw_ref)
            dcb_ref[...] = jnp.zeros_like(dcb_ref)
            dng_ref[...] = jnp.zeros_like(dng_ref)
            dnb_ref[...] = jnp.zeros_like(dnb_ref)
            carry[...] = jnp.zeros_like(carry)

        xh, r = _ln_stats(yp_ref[...])
        yl = xh * ng_ref[...] + nb_ref[...]
        sg = _sigmoid(yl)
        dyl = db_ref[...] * (sg * (1.0 + yl * (1.0 - sg)))
        dng_ref[...] += _colsum(dyl * xh)
        dnb_ref[...] += _colsum(dyl)
        dypre = _ln_bwd(xh, r, dyl * ng_ref[...])
        dcb_ref[...] += _colsum(dypre)

        a, gate = a_ref[...], g_ref[...]
        sgate = _sigmoid(gate)
        ext[pl.ds(0, CONV_HALO), :] = jnp.where(i > 0, ap_ref[...] * _sigmoid(gp_ref[...]), 0.0)
        ext[pl.ds(CONV_HALO, tt), :] = a * sgate
        ext_d[pl.ds(0, tt), :] = dypre
        ext_d[pl.ds(tt, CONV_HALO), :] = carry[...]
        dglu = jnp.zeros((tt, BR), F32)
        for k in range(CONV_WIDTH):
            dw_ref[pl.ds(k, 1), :] += _colsum(dypre * ext[pl.ds(lead + k, tt), :])
            dglu = dglu + w_ref[pl.ds(k, 1), :] * ext_d[pl.ds(CONV_WIDTH - 1 - k, tt), :]
        carry[...] = ext_d[pl.ds(0, CONV_HALO), :]
        da_ref[...] = (dglu * sgate).astype(BF16)
        dg_ref[...] = (dglu * a * sgate * (1.0 - sgate)).astype(BF16)

    rev = lambda st: nt - 1 - st
    cur = lambda c: _bs((tt, BR), lambda st: (rev(st), c))
    prev = lambda c: _bs((CONV_HALO, BR), lambda st: (jnp.maximum(rev(st) * hb - 1, 0), c))
    vec = _bs((1, BR), lambda st: (0, 0))
    row = _bs((tt, BR), lambda st: (rev(st), 0))
    wsp = _bs((CONV_WIDTH, BR), lambda st: (0, 0))
    return _call(body, name=name, grid=(nt,),
                 in_specs=[cur(cb), cur(cb + 1), prev(cb), prev(cb + 1), row, row, wsp, vec, vec],
                 out_specs=[row, row, wsp, vec, vec, vec],
                 out_shape=[_sds((s, BR), BF16), _sds((s, BR), BF16), _sds((CONV_WIDTH, BR), F32),
                            _sds((1, BR), F32), _sds((1, BR), F32), _sds((1, BR), F32)],
                 scratch=[pltpu.VMEM((tt + CONV_HALO, BR), F32), pltpu.VMEM((tt + CONV_HALO, BR), F32),
                          pltpu.VMEM((CONV_HALO, BR), F32)],
                 sem=("arbitrary",))(z, z, z, z, d_b, ypre, conv_w, ng, nb)


def _sgu_fwd(z, cb, ng, nb, w_masked, bias_full, name):
    s = z.shape[0]
    tt = _tile(s, (TOK_NARROW,))
    ngr = BR // GROUP

    def body(u_ref, v_ref, ng_ref, nb_ref, w_ref, bb_ref, out_ref):
        ua = _gelu(u_ref[...])
        xh, _ = _ln_stats(_gelu(v_ref[...]))
        vn = (xh * ng_ref[...] + nb_ref[...]).astype(BF16)
        for n in range(tt // GROUP):
            for g in range(ngr):
                r0, c0 = n * GROUP, g * GROUP
                sp = _dot(w_ref[g], vn[r0:r0 + GROUP, c0:c0 + GROUP]) + bb_ref[g]
                out_ref[r0:r0 + GROUP, c0:c0 + GROUP] = (ua[r0:r0 + GROUP, c0:c0 + GROUP] * sp).astype(BF16)

    vec = _bs((1, BR), lambda i: (0, 0))
    sq = _bs((ngr, GROUP, GROUP), lambda i: (0, 0, 0))
    return _call(body, name=name, grid=(s // tt,),
                 in_specs=[_bs((tt, BR), lambda i: (i, cb)), _bs((tt, BR), lambda i: (i, cb + 1)), vec, vec, sq, sq],
                 out_specs=_bs((tt, BR), lambda i: (i, 0)), out_shape=_sds((s, BR), BF16),
                 sem=("parallel",))(z, z, ng, nb, w_masked, bias_full)


def _sgu_bwd(z, cb, d_b, ng, nb, w_masked, bias_full, name):
    s = z.shape[0]
    tt = _tile(s, (TOK_NARROW,))
    ngr = BR // GROUP

    def body(u_ref, v_ref, db_ref, ng_ref, nb_ref, w_ref, bb_ref,
             du_ref, dv_ref, dw_ref, dbias_ref, dng_ref, dnb_ref, dvn_s):
        @pl.when(pl.program_id(0) == 0)
        def _():
            dw_ref[...] = jnp.zeros_like(dw_ref)
            dbias_ref[...] = jnp.zeros_like(dbias_ref)
            dng_ref[...] = jnp.zeros_like(dng_ref)
            dnb_ref[...] = jnp.zeros_like(dnb_ref)

        u, v, db = u_ref[...], v_ref[...], db_ref[...]
        ua = _gelu(u)
        xh, r = _ln_stats(_gelu(v))
        vn = (xh * ng_ref[...] + nb_ref[...]).astype(BF16)
        for n in range(tt // GROUP):
            for g in range(ngr):
                rows, cols = slice(n * GROUP, (n + 1) * GROUP), slice(g * GROUP, (g + 1) * GROUP)
                vn_c = vn[rows, cols]
                sp = _dot(w_ref[g], vn_c) + bb_ref[g]
                du_ref[rows, cols] = (db[rows, cols] * sp * _gelu_grad(u[rows, cols])).astype(BF16)
                dsp = db[rows, cols] * ua[rows, cols]
                dbias_ref[g] += dsp
                dsp16 = dsp.astype(BF16)
                dw_ref[g] += _dot_nt(dsp16, vn_c)
                dvn_s[rows, cols] = _dot_tn(w_ref[g], dsp16)
        dvn = dvn_s[...]
        dng_ref[...] += _colsum(dvn * xh)
        dnb_ref[...] += _colsum(dvn)
        dv_ref[...] = (_ln_bwd(xh, r, dvn * ng_ref[...]) * _gelu_grad(v)).astype(BF16)

    vec = _bs((1, BR), lambda i: (0, 0))
    sq = _bs((ngr, GROUP, GROUP), lambda i: (0, 0, 0))
    row = _bs((tt, BR), lambda i: (i, 0))
    return _call(body, name=name, grid=(s // tt,),
                 in_specs=[_bs((tt, BR), lambda i: (i, cb)), _bs((tt, BR), lambda i: (i, cb + 1)), row, vec, vec, sq, sq],
                 out_specs=[row, row, sq, sq, vec, vec],
                 out_shape=[_sds((s, BR), BF16), _sds((s, BR), BF16), _sds((ngr, GROUP, GROUP), F32),
                            _sds((ngr, GROUP, GROUP), F32), _sds((1, BR), F32), _sds((1, BR), F32)],
                 scratch=[pltpu.VMEM((tt, BR), F32)], sem=("arbitrary",))(z, z, d_b, ng, nb, w_masked, bias_full)


def _rope(rv, c_t, s1_t, s2_t):
    return rv * c_t + pltpu.roll(rv, 96, 1) * s1_t + pltpu.roll(rv, 32, 1) * s2_t


def _rope_bwd(gv, c_t, s1_t, s2_t):
    return gv * c_t - pltpu.roll(gv, 96, 1) * s1_t - pltpu.roll(gv, 32, 1) * s2_t


def _mla_prep(z, cb_q, cb_kv, cb_kr, qg, kvg, c_t, s1_t, s2_t, name):
    s = z.shape[0]
    tt = _tile(s, (TOK_NARROW,))

    def body(cq_ref, ckv_ref, kr_ref, qg_ref, kvg_ref, c_ref, s1_ref, s2_ref, qn_ref, kvn_ref, krp_ref):
        cq, ckv = cq_ref[...], ckv_ref[...]
        qn_ref[...] = (cq * _rstd(cq) * qg_ref[...]).astype(BF16)
        kvn_ref[...] = (ckv * _rstd(ckv) * kvg_ref[...]).astype(BF16)
        krp_ref[...] = _rope(kr_ref[...], c_ref[...], s1_ref[...], s2_ref[...]).astype(BF16)

    vec = _bs((1, BR), lambda i: (0, 0))
    rp = _bs((tt, 128), lambda i: (i, 0))
    row = _bs((tt, BR), lambda i: (i, 0))
    return _call(body, name=name, grid=(s // tt,),
                 in_specs=[_bs((tt, BR), lambda i: (i, cb_q)), _bs((tt, BR), lambda i: (i, cb_kv)),
                           _bs((tt, 128), lambda i: (i, cb_kr)), vec, vec, rp, rp, rp],
                 out_specs=[row, row, rp], out_shape=[_sds((s, BR), BF16), _sds((s, BR), BF16), _sds((s, 128), BF16)],
                 sem=("parallel",))(z, z, z, qg, kvg, c_t, s1_t, s2_t)


def _mla_prep_bwd(z, cb_q, cb_kv, d_qn, d_kvn, d_krp, qg, kvg, c_t, s1_t, s2_t, name):
    s = z.shape[0]
    tt = _tile(s, (TOK_NARROW,))

    def body(cq_ref, ckv_ref, dqn_ref, dkvn_ref, dkr_ref, qg_ref, kvg_ref, c_ref, s1_ref, s2_ref,
             dcq_ref, dckv_ref, dkro_ref, dqg_ref, dkvg_ref):
        @pl.when(pl.program_id(0) == 0)
        def _():
            dqg_ref[...] = jnp.zeros_like(dqg_ref)
            dkvg_ref[...] = jnp.zeros_like(dkvg_ref)

        cq, ckv, dqn, dkvn = cq_ref[...], ckv_ref[...], dqn_ref[...], dkvn_ref[...]
        rq, rkv = _rstd(cq), _rstd(ckv)
        dcq_ref[...] = _rms_bwd(cq, rq, dqn * qg_ref[...]).astype(BF16)
        dckv_ref[...] = _rms_bwd(ckv, rkv, dkvn * kvg_ref[...]).astype(BF16)
        dqg_ref[...] += _colsum(dqn * cq * rq)
        dkvg_ref[...] += _colsum(dkvn * ckv * rkv)
        dkro_ref[...] = _rope_bwd(dkr_ref[...], c_ref[...], s1_ref[...], s2_ref[...]).astype(BF16)

    vec = _bs((1, BR), lambda i: (0, 0))
    rp = _bs((tt, 128), lambda i: (i, 0))
    row = _bs((tt, BR), lambda i: (i, 0))
    return _call(body, name=name, grid=(s // tt,),
                 in_specs=[_bs((tt, BR), lambda i: (i, cb_q)), _bs((tt, BR), lambda i: (i, cb_kv)),
                           row, row, rp, vec, vec, rp, rp, rp],
                 out_specs=[row, row, rp, vec, vec],
                 out_shape=[_sds((s, BR), BF16), _sds((s, BR), BF16), _sds((s, 128), BF16),
                            _sds((1, BR), F32), _sds((1, BR), F32)],
                 sem=("arbitrary",))(z, z, d_qn, d_kvn, d_krp, qg, kvg, c_t, s1_t, s2_t)


def _q_rope(q, c_t, s1_t, s2_t, name):
    s, n = q.shape
    tt = _tile(s, (TOK_WIDE,))

    def body(q_ref, c_ref, s1_ref, s2_ref, out_ref):
        c_v, s1_v, s2_v = c_ref[...], s1_ref[...], s2_ref[...]
        for h in range(HEADS):
            b0 = h * 256
            out_ref[:, b0:b0 + 128] = q_ref[:, b0:b0 + 128].astype(BF16)
            out_ref[:, b0 + 128:b0 + 256] = _rope(q_ref[:, b0 + 128:b0 + 256], c_v, s1_v, s2_v).astype(BF16)

    rp = _bs((tt, 128), lambda i: (i, 0))
    row = _bs((tt, n), lambda i: (i, 0))
    return _call(body, name=name, grid=(s // tt,), in_specs=[row, rp, rp, rp], out_specs=row,
                 out_shape=_sds((s, n), BF16), sem=("parallel",))(q, c_t, s1_t, s2_t)


def _att_scores(q_ref, kn_ref, kr_ref, diagonal, t):
    q = q_ref[...]
    sc = (_dot_nt(q[:, :128], kn_ref[...]) + _dot_nt(q[:, 128:], kr_ref[...])) * ATT_SCALE
    if not diagonal:
        return sc
    row = lax.broadcasted_iota(jnp.int32, (t, t), 0)
    col = lax.broadcasted_iota(jnp.int32, (t, t), 1)
    return jnp.where(col <= row, sc, -1e30)


def _on_causal_pairs(q_tile, k_tile, step):
    @pl.when(k_tile < q_tile)
    def _():
        step(False)

    @pl.when(k_tile == q_tile)
    def _():
        step(True)


def _flash_fwd(qb, kv, krp, name):
    s = qb.shape[0]
    t = _tile(s, (ATT_TILE,))
    nq = s // t

    def body(q_ref, kn_ref, v_ref, kr_ref, o_ref, lse_ref, m_s, l_s, acc):
        i, j = pl.program_id(1), pl.program_id(2)

        @pl.when(j == 0)
        def _():
            m_s[...] = jnp.full_like(m_s, -1e30)
            l_s[...] = jnp.zeros_like(l_s)
            acc[...] = jnp.zeros_like(acc)

        def step(diagonal):
            sc = _att_scores(q_ref, kn_ref, kr_ref, diagonal, t)
            m_new = jnp.maximum(m_s[...], jnp.max(sc, axis=-1, keepdims=True))
            p = jnp.exp(sc - m_new)
            alpha = jnp.exp(m_s[...] - m_new)
            l_s[...] = alpha * l_s[...] + jnp.sum(p, axis=-1, keepdims=True)
            acc[...] = alpha * acc[...] + _dot(p.astype(BF16), v_ref[...])
            m_s[...] = m_new

        _on_causal_pairs(i, j, step)

        @pl.when(j == i)
        def _():
            o_ref[...] = (acc[...] / l_s[...]).astype(BF16)
            lse_ref[...] = m_s[...] + jnp.log(l_s[...])

    kmap = lambda off: (lambda h, i, j: (jnp.minimum(j, i), 2 * h + off))
    return _call(body, name=name, grid=(HEADS, nq, nq),
                 in_specs=[_bs((t, 256), lambda h, i, j: (i, h)), _bs((t, 128), kmap(0)), _bs((t, 128), kmap(1)),
                           _bs((t, 128), lambda h, i, j: (jnp.minimum(j, i), 0))],
                 out_specs=[_bs((t, 128), lambda h, i, j: (i, h)), _bs((None, t, 1), lambda h, i, j: (h, i, 0))],
                 out_shape=[_sds((s, HEADS * V_DIM), BF16), _sds((HEADS, s, 1), F32)],
                 scratch=[pltpu.VMEM((t, 1), F32), pltpu.VMEM((t, 1), F32), pltpu.VMEM((t, 128), F32)],
                 sem=("parallel", "parallel", "arbitrary"))(qb, kv, kv, krp)


def _flash_bwd_dq(qb, kv, krp, o, d_o, lse, c_t, s1_t, s2_t, name):
    s = qb.shape[0]
    t = _tile(s, (ATT_TILE,))
    nq = s // t

    def body(q_ref, kn_ref, v_ref, kr_ref, o_ref, do_ref, lse_ref, c_ref, s1_ref, s2_ref, dq_ref, dqn_s, dqr_s, dl_s):
        i, j = pl.program_id(1), pl.program_id(2)

        @pl.when(j == 0)
        def _():
            dqn_s[...] = jnp.zeros_like(dqn_s)
            dqr_s[...] = jnp.zeros_like(dqr_s)
            dl_s[...] = jnp.sum(do_ref[...].astype(F32) * o_ref[...].astype(F32), axis=-1, keepdims=True)

        def step(diagonal):
            p = jnp.exp(_att_scores(q_ref, kn_ref, kr_ref, diagonal, t) - lse_ref[...])
            dp = _dot_nt(do_ref[...], v_ref[...])
            ds = (p * (dp - dl_s[...]) * ATT_SCALE).astype(BF16)
            dqn_s[...] += _dot(ds, kn_ref[...])
            dqr_s[...] += _dot(ds, kr_ref[...])

        _on_causal_pairs(i, j, step)

        @pl.when(j == i)
        def _():
            dq_ref[:, :128] = dqn_s[...].astype(BF16)
            dq_ref[:, 128:] = _rope_bwd(dqr_s[...], c_ref[...], s1_ref[...], s2_ref[...]).astype(BF16)

    kmap = lambda off: (lambda h, i, j: (jnp.minimum(j, i), 2 * h + off))
    qrow = _bs((t, 128), lambda h, i, j: (i, h))
    rp = _bs((t, 128), lambda h, i, j: (i, 0))
    return _call(body, name=name, grid=(HEADS, nq, nq),
                 in_specs=[_bs((t, 256), lambda h, i, j: (i, h)), _bs((t, 128), kmap(0)), _bs((t, 128), kmap(1)),
                           _bs((t, 128), lambda h, i, j: (jnp.minimum(j, i), 0)), qrow, qrow,
                           _bs((None, t, 1), lambda h, i, j: (h, i, 0)), rp, rp, rp],
                 out_specs=_bs((t, 256), lambda h, i, j: (i, h)), out_shape=_sds((s, HEADS * 256), BF16),
                 scratch=[pltpu.VMEM((t, 128), F32), pltpu.VMEM((t, 128), F32), pltpu.VMEM((t, 1), F32)],
                 sem=("parallel", "parallel", "arbitrary"))(qb, kv, kv, krp, o, d_o, lse, c_t, s1_t, s2_t)


def _flash_bwd_dkv(qb, kv, krp, o, d_o, lse, name):
    s = qb.shape[0]
    t = _tile(s, (ATT_TILE,))
    nq = s // t

    def body(q_ref, kn_ref, v_ref, kr_ref, o_ref, do_ref, lse_ref, dkv_ref, dkr_ref, dk_s, dv_s, dkr_s):
        j, h, i = pl.program_id(0), pl.program_id(1), pl.program_id(2)

        @pl.when(i == j)
        def _():
            dk_s[...] = jnp.zeros_like(dk_s)
            dv_s[...] = jnp.zeros_like(dv_s)

        @pl.when(jnp.logical_and(i == j, h == 0))
        def _():
            dkr_s[...] = jnp.zeros_like(dkr_s)

        def step(diagonal):
            q, do = q_ref[...], do_ref[...]
            p = jnp.exp(_att_scores(q_ref, kn_ref, kr_ref, diagonal, t) - lse_ref[...])
            delta = jnp.sum(do.astype(F32) * o_ref[...].astype(F32), axis=-1, keepdims=True)
            dv_s[...] += _dot_tn(p.astype(BF16), do)
            ds = (p * (_dot_nt(do, v_ref[...]) - delta) * ATT_SCALE).astype(BF16)
            dk_s[...] += _dot_tn(ds, q[:, :128])
            dkr_s[...] += _dot_tn(ds, q[:, 128:])

        _on_causal_pairs(i, j, step)

        @pl.when(i == nq - 1)
        def _():
            dkv_ref[:, :128] = dk_s[...].astype(BF16)
            dkv_ref[:, 128:] = dv_s[...].astype(BF16)

        @pl.when(jnp.logical_and(i == nq - 1, h == HEADS - 1))
        def _():
            dkr_ref[...] = dkr_s[...]

    qi = lambda j, h, i: jnp.maximum(i, j)
    qrow = _bs((t, 128), lambda j, h, i: (qi(j, h, i), h))
    return _call(body, name=name, grid=(nq, HEADS, nq),
                 in_specs=[_bs((t, 256), lambda j, h, i: (qi(j, h, i), h)),
                           _bs((t, 128), lambda j, h, i: (j, 2 * h)), _bs((t, 128), lambda j, h, i: (j, 2 * h + 1)),
                           _bs((t, 128), lambda j, h, i: (j, 0)), qrow, qrow,
                           _bs((None, t, 1), lambda j, h, i: (h, qi(j, h, i), 0))],
                 out_specs=[_bs((t, 256), lambda j, h, i: (j, h)), _bs((t, 128), lambda j, h, i: (j, 0))],
                 out_shape=[_sds((s, HEADS * 256), BF16), _sds((s, 128), F32)],
                 scratch=[pltpu.VMEM((t, 128), F32), pltpu.VMEM((t, 128), F32), pltpu.VMEM((t, 128), F32)],
                 sem=("parallel", "arbitrary", "arbitrary"))(qb, kv, kv, krp, o, d_o, lse)


def _merge_tiles(s, projs):
    return _tile(s, (TOK_WIDE,)), _div_tile(projs[0].shape[2], 512, 128)


def _merge_specs(s, d, tt, tn):
    nb, npb = d // tn, d // N_CHIPS // tn
    br = lambda w: _bs((tt, w), lambda i, n: (i, 0))
    pw = lambda k: _bs((None, k, tn), lambda i, n: (n // npb, 0, n % npb))
    gate = lambda g: _bs((tt, tn), lambda i, n: (i, g * nb + n))
    return [br(BR), br(BR), br(BR), br(HEADS * V_DIM), pw(BR), pw(BR), pw(BR), pw(HEADS * V_DIM)] + \
           [gate(g) for g in range(4)]


def _merge_fwd(z, branches, projs, name):
    s, d = z.shape[0], N_CHIPS * projs[0].shape[2]
    tt, tn = _merge_tiles(s, projs)

    def body(*refs):
        b_refs, p_refs, g_refs, out_ref = refs[0:4], refs[4:8], refs[8:12], refs[12]
        acc = None
        for b_ref, p_ref, g_ref in zip(b_refs, p_refs, g_refs):
            term = _sigmoid(g_ref[...]) * _dot(b_ref[...], p_ref[...])
            acc = term if acc is None else acc + term
        out_ref[...] = acc.astype(BF16)

    return _call(body, name=name, grid=(s // tt, d // tn), in_specs=_merge_specs(s, d, tt, tn),
                 out_specs=_bs((tt, tn), lambda i, n: (i, n)), out_shape=_sds((s, d), BF16),
                 sem=("parallel", "parallel"))(*branches, *projs, z, z, z, z)


def _merge_bwd(z, branches, projs, d_merged, name):
    s, d = z.shape[0], N_CHIPS * projs[0].shape[2]
    tt, tn = _merge_tiles(s, projs)

    def body(*refs):
        b_refs, p_refs, g_refs, dm_ref = refs[0:4], refs[4:8], refs[8:12], refs[12]
        dy_refs, dg_refs = refs[13:17], refs[17:21]
        dm = dm_ref[...]
        for b_ref, p_ref, g_ref, dy_ref, dg_ref in zip(b_refs, p_refs, g_refs, dy_refs, dg_refs):
            sg = _sigmoid(g_ref[...])
            dy_ref[...] = (dm * sg).astype(BF16)
            dg_ref[...] = (dm * _dot(b_ref[...], p_ref[...]) * sg * (1.0 - sg)).astype(BF16)

    tile = _bs((tt, tn), lambda i, n: (i, n))
    outs = _call(body, name=name, grid=(s // tt, d // tn), in_specs=_merge_specs(s, d, tt, tn) + [tile],
                 out_specs=[tile] * 8, out_shape=[_sds((s, d), BF16)] * 8,
                 sem=("parallel", "parallel"))(*branches, *projs, z, z, z, z, d_merged)
    return outs[:4], outs[4:]


def _ew_rows(rows, cols, align=16):
    lanes = -(-cols // 128) * 128
    return _div_tile(rows, max(EW_BLOCK_BYTES // (lanes * 4), align), align)


def _add_pair(a, b, name):
    n, r, c = a.shape
    tr = _ew_rows(r, c)

    def body(a_ref, b_ref, o_ref):
        o_ref[...] = (a_ref[...].astype(F32) + b_ref[...].astype(F32)).astype(BF16)

    blk = _bs((None, tr, c), lambda k, i: (k, i, 0))
    return _call(body, name=name, grid=(n, r // tr), in_specs=[blk, blk], out_specs=blk,
                 out_shape=_sds((n, r, c), BF16), sem=("parallel", "parallel"))(a, b)


def _sum_chips(land, own, name):
    n, r, c = land.shape
    tr = _ew_rows(r, c)

    def body(l_ref, o_ref, out_ref):
        acc = l_ref[0].astype(F32)
        for k in range(1, n):
            acc = acc + l_ref[k].astype(F32)
        out_ref[...] = acc + o_ref[...].astype(F32)

    blk = _bs((tr, c), lambda i: (i, 0))
    return _call(body, name=name, grid=(r // tr,), in_specs=[_bs((n, tr, c), lambda i: (0, i, 0)), blk],
                 out_specs=blk, out_shape=_sds((r, c), F32), sem=("parallel",))(land, own)


def _sum_slots(buf, name):
    n, r, c = buf.shape
    tr = _div_tile(r, max(EW_BLOCK_BYTES // (c * 4 * n), 8), 8)

    def body(b_ref, o_ref):
        acc = b_ref[0].astype(F32)
        for k in range(1, n):
            acc = acc + b_ref[k].astype(F32)
        o_ref[...] = acc

    return _call(body, name=name, grid=(r // tr,), in_specs=[_bs((n, tr, c), lambda i: (0, i, 0))],
                 out_specs=_bs((tr, c), lambda i: (i, 0)), out_shape=_sds((r, c), F32), sem=("parallel",))(buf)


def _adam_update(w, g, m, v):
    nm = ADAM_B1 * m + (1.0 - ADAM_B1) * g
    nv = ADAM_B2 * v + (1.0 - ADAM_B2) * jnp.square(g)
    m_hat = nm / (1.0 - ADAM_B1 ** ADAM_STEP)
    v_hat = nv / (1.0 - ADAM_B2 ** ADAM_STEP)
    return -ADAM_LR * (m_hat / (jnp.sqrt(v_hat) + ADAM_EPS) + ADAM_WD * w), nm, nv


def _adamw(w, g, m, v, name):
    shape = w.shape
    cols = shape[-1]
    rows = 1
    for dim in shape[:-1]:
        rows *= dim
    w2, g2, m2, v2 = (t.reshape(rows, cols) for t in (w, g, m, v))
    tr = _ew_rows(rows, cols, align=8)

    def body(w_ref, g_ref, m_ref, v_ref, d_ref, nm_ref, nv_ref):
        d_ref[...], nm_ref[...], nv_ref[...] = _adam_update(w_ref[...], g_ref[...], m_ref[...], v_ref[...])

    blk = _bs((tr, cols), lambda i: (i, 0))
    outs = _call(body, name=name, grid=(rows // tr,), in_specs=[blk] * 4, out_specs=[blk] * 3,
                 out_shape=[_sds((rows, cols), F32)] * 3, sem=("parallel",))(w2, g2, m2, v2)
    return [o.reshape(shape) for o in outs]


def _adamw_full(w, gs, m, v, name):
    depth, r, c = w.shape
    tr = _ew_rows(r, c, align=8)
    nb = r // tr

    def body(w_ref, m_ref, v_ref, *rest):
        g_refs, (g_out, d_ref, nm_ref, nv_ref) = rest[:depth], rest[depth:]
        layer = pl.program_id(0)
        gv = g_refs[0][...]
        for k in range(1, depth):
            gv = jnp.where(layer == k, g_refs[k][...], gv)
        g_out[...] = gv
        d_ref[...], nm_ref[...], nv_ref[...] = _adam_update(w_ref[...], gv, m_ref[...], v_ref[...])

    blk = _bs((None, tr, c), lambda l, i: (l, i, 0))
    g_spec = lambda k: _bs((tr, c), lambda l, i: (jnp.where(l == k, i, jnp.where(l < k, 0, nb - 1)), 0))
    return _call(body, name=name, grid=(depth, nb), in_specs=[blk] * 3 + [g_spec(k) for k in range(depth)],
                 out_specs=[blk] * 4, out_shape=[_sds((depth, r, c), F32)] * 4,
                 sem=("arbitrary", "arbitrary"))(w, m, v, *gs)


def _adamw_transposed(w, mine, other, m, v, my_c, name):
    gs = []
    for a, b in zip(mine, other):
        lo, hi = jnp.where(my_c == 0, a, b), jnp.where(my_c == 0, b, a)
        gs.append(jnp.concatenate([lo.T, hi.T], axis=1))
    outs = _adamw_full(jnp.swapaxes(w, 1, 2), gs, jnp.swapaxes(m, 1, 2), jnp.swapaxes(v, 1, 2), name)
    return [jnp.swapaxes(o, 1, 2) for o in outs]


def _adamw_big(w, mine, other, m, v, name):
    depth, r, c = w.shape
    tr = _ew_rows(r // 2, c, align=8)
    hb = (r // 2) // tr

    def body(w_ref, m_ref, v_ref, *rest):
        mine_refs, other_refs = rest[:depth], rest[depth:2 * depth]
        g_out, d_ref, nm_ref, nv_ref = rest[2 * depth:]
        layer, blk_i = pl.program_id(0), pl.program_id(1)
        is_mine = (blk_i // hb) == lax.axis_index("c")
        gv = jnp.where(is_mine, mine_refs[0][...], other_refs[0][...])
        for k in range(1, depth):
            gv = jnp.where(layer == k, jnp.where(is_mine, mine_refs[k][...], other_refs[k][...]), gv)
        g_out[...] = gv
        d_ref[...], nm_ref[...], nv_ref[...] = _adam_update(w_ref[...], gv, m_ref[...], v_ref[...])

    blk = _bs((None, tr, c), lambda l, i: (l, i, 0))
    g_spec = lambda k: _bs((tr, c), lambda l, i: (jnp.where(l == k, i % hb, jnp.where(l < k, 0, hb - 1)), 0))
    return _call(body, name=name, grid=(depth, 2 * hb),
                 in_specs=[blk] * 3 + [g_spec(k) for k in range(depth)] * 2,
                 out_specs=[blk] * 4, out_shape=[_sds((depth, r, c), F32)] * 4,
                 sem=("arbitrary", "arbitrary"))(w, m, v, *mine, *other)


ANY = pl.BlockSpec(memory_space=pl.ANY)
HBM = pl.BlockSpec(memory_space=pltpu.HBM)
SEM = pl.BlockSpec(memory_space=pltpu.SEMAPHORE)
DATAFLOW = pltpu.SideEffectType.DATAFLOW_SIDE_EFFECTING
DMA_SEMS = pltpu.SemaphoreType.DMA


def _place():
    return lax.axis_index("x"), lax.axis_index("y"), lax.axis_index("c")


def _other_chips(x, y):
    return [(1 - x, y), (x, 1 - y), (1 - x, 1 - y)]


def _half_rows(rows, c):
    half = rows // 2
    assert half % 16 == 0
    return pl.ds(pl.multiple_of(c * half, 16), half)


def _in_hbm(a):
    return pltpu.with_memory_space_constraint(a, pltpu.HBM)


def _ici_copies(mode, src, land, send_sems, recv_sems):
    x, y, c = _place()
    my = 2 * x + y
    out = []
    for i in range(len(src)):
        for j, chip in enumerate(_other_chips(x, y)):
            peer = 2 * chip[0] + chip[1]
            if mode == 'gather':
                rows = _half_rows(src[i].shape[0], c)
                s_ref, d_send, d_recv = src[i].at[rows], land[i].at[my, rows], land[i].at[peer, rows]
            else:
                s_ref, d_send, d_recv = src[i].at[peer], land[i].at[j], land[i].at[j]
            pair = [pltpu.make_async_remote_copy(src_ref=s_ref, dst_ref=dst, send_sem=send_sems.at[3 * i + j],
                                                 recv_sem=recv_sems.at[3 * i + j], device_id=(chip[0], chip[1], c),
                                                 device_id_type=MESH) for dst in (d_send, d_recv)]
            out.append(pair)
    return out


def _ici_start(mode, srcs, land_shapes, name):
    n = len(srcs)
    lands = [_in_hbm(lax.empty(shp, s.dtype)) for shp, s in zip(land_shapes, srcs)]

    def body(*refs):
        src, land, send_sems, recv_sems, token = refs[:n], refs[n:2 * n], refs[2 * n], refs[2 * n + 1], refs[-1]
        for send, _ in _ici_copies(mode, src, land, send_sems, recv_sems):
            send.start()
        token[...] = jnp.zeros_like(token)

    outs = pl.pallas_call(
        body, name=name,
        out_shape=(DMA_SEMS((3 * n,)), DMA_SEMS((3 * n,)), *[pltpu.HBM(s.shape, s.dtype) for s in srcs],
                   *[pltpu.HBM(l.shape, l.dtype) for l in lands], _sds((8, 128), F32)),
        in_specs=[HBM] * (2 * n), out_specs=(SEM, SEM, *[HBM] * (2 * n), pl.BlockSpec(memory_space=pltpu.VMEM)),
        input_output_aliases={i: 2 + i for i in range(2 * n)},
        compiler_params=pltpu.CompilerParams(has_side_effects=DATAFLOW))(*[_in_hbm(s) for s in srcs], *lands)
    return outs[0], outs[1], list(outs[2:2 + n]), list(outs[2 + n:2 + 2 * n]), outs[-1]


def _ici_wait(mode, started, after, name):
    send_sems, recv_sems, src_thru, land_thru, _ = started
    n = len(src_thru)

    def body(*refs):
        src, land, send_s, recv_s = refs[:n], refs[n:2 * n], refs[2 * n], refs[2 * n + 1]
        for send, recv in _ici_copies(mode, src, land, send_s, recv_s):
            send.wait_send()
            recv.wait_recv()

    outs = pl.pallas_call(
        body, name=name, out_shape=tuple(pltpu.HBM(t.shape, t.dtype) for t in src_thru + land_thru),
        in_specs=[HBM] * (2 * n) + [SEM, SEM, ANY], out_specs=(HBM,) * (2 * n),
        input_output_aliases={i: i for i in range(2 * n)},
        compiler_params=pltpu.CompilerParams(has_side_effects=DATAFLOW))(*src_thru, *land_thru, send_sems, recv_sems,
                                                                        after)
    return list(outs[n:])


def _sibling_call(body, name, inputs, out_shape, n_copies, n_local, aliases=None):
    return pl.pallas_call(body, name=name, out_shape=out_shape, in_specs=[ANY] * len(inputs),
                          out_specs=[ANY] * len(out_shape), input_output_aliases=aliases or {},
                          scratch_shapes=[DMA_SEMS((n_copies,)), DMA_SEMS((n_copies,)), DMA_SEMS((n_local,))])(*inputs)


def _share_weights(srcs, lands, name):
    n = len(srcs)

    def body(*refs):
        src, land_in, land = refs[:n], refs[n:2 * n], refs[2 * n:3 * n]
        send_sems, recv_sems, _ = refs[3 * n:3 * n + 3]
        x, y, c = _place()
        my, sib = 2 * x + y, (x, y, 1 - c)

        def d2d(k, from_ref, to_ref):
            return pltpu.make_async_remote_copy(src_ref=from_ref, dst_ref=to_ref, send_sem=send_sems.at[k],
                                                recv_sem=recv_sems.at[k], device_id=sib, device_id_type=MESH)

        sends, arrivals = [], []
        for i in range(n):
            mine, theirs = _half_rows(src[i].shape[0], c), _half_rows(src[i].shape[0], 1 - c)
            for j, chip in enumerate(_other_chips(x, y)):
                peer = 2 * chip[0] + chip[1]
                sends.append(d2d(4 * i + j, land_in[i].at[peer, mine], land[i].at[peer, mine]))
                arrivals.append(d2d(4 * i + j, land_in[i].at[peer, theirs], land[i].at[peer, theirs]))
            sends.append(d2d(4 * i + 3, src[i], land[i].at[my]))
            arrivals.append(d2d(4 * i + 3, src[i], land[i].at[my]))
        for cp in sends:
            cp.start()
        for cp in arrivals:
            cp.wait_recv()
        for cp in sends:
            cp.wait_send()

    return list(_sibling_call(body, name, list(srcs) + list(lands), [_sds(l.shape, l.dtype) for l in lands], 4 * n, 1,
                              aliases={n + i: i for i in range(n)}))


def _swap_grad_halves(gs, name):
    n = len(gs)

    def body(*refs):
        g, got = refs[:n], refs[n:2 * n]
        send_sems, recv_sems, _ = refs[2 * n:2 * n + 3]
        x, y, c = _place()
        sends = [pltpu.make_async_remote_copy(src_ref=g[i].at[:, _half_rows(g[i].shape[1], 1 - c)], dst_ref=got[i],
                                              send_sem=send_sems.at[i], recv_sem=recv_sems.at[i],
                                              device_id=(x, y, 1 - c), device_id_type=MESH) for i in range(n)]
        for cp in sends:
            cp.start()
        for cp in sends:
            cp.wait()

    half = [_sds((g.shape[0], g.shape[1] // 2, g.shape[2]), g.dtype) for g in gs]
    return list(_sibling_call(body, name, list(gs), half, n, 1))


def _share_final(fins, name):
    n = len(fins)

    def body(*refs):
        fin, other = refs[:n], refs[n:2 * n]
        send_sems, recv_sems, _ = refs[2 * n:2 * n + 3]
        x, y, c = _place()
        sends = [pltpu.make_async_remote_copy(src_ref=fin[i], dst_ref=other[i], send_sem=send_sems.at[i],
                                              recv_sem=recv_sems.at[i], device_id=(x, y, 1 - c), device_id_type=MESH)
                 for i in range(n)]
        for cp in sends:
            cp.start()
        for cp in sends:
            cp.wait()

    return list(_sibling_call(body, name, list(fins), [_sds(f.shape, f.dtype) for f in fins], n, 1))


def _gather_all(buf, name):
    flips = [(fx, fy, fc) for fx in (0, 1) for fy in (0, 1) for fc in (0, 1) if (fx, fy, fc) != (0, 0, 0)]

    def body(src, dst, send_sems, recv_sems, loc_sem):
        x, y, c = _place()
        me = 4 * x + 2 * y + c
        loc = pltpu.make_async_copy(src, dst.at[me], loc_sem.at[0])
        loc.start()
        peers = [(1 - x if fx else x, 1 - y if fy else y, 1 - c if fc else c) for fx, fy, fc in flips]

        def cp(j, peer, slot):
            return pltpu.make_async_remote_copy(src_ref=src, dst_ref=dst.at[slot], send_sem=send_sems.at[j],
                                                recv_sem=recv_sems.at[j], device_id=peer, device_id_type=MESH)

        sends = [cp(j, peer, me) for j, peer in enumerate(peers)]
        for s_ in sends:
            s_.start()
        for j, peer in enumerate(peers):
            cp(j, peer, 4 * peer[0] + 2 * peer[1] + peer[2]).wait_recv()
        for s_ in sends:
            s_.wait_send()
        loc.wait()

    return _sibling_call(body, name, [buf], [_sds((8,) + buf.shape, buf.dtype)], 7, 1)[0]


def _layout(shapes):
    out, r0 = [], 0
    for name, shape in shapes:
        n = 1
        for dim in shape:
            n *= dim
        nr = -(-n // PACK_COLS)
        nr = -(-nr // PACK_ROW_ALIGN) * PACK_ROW_ALIGN
        out.append((name, tuple(shape), r0, nr))
        r0 += nr
    return out, r0


def _pack(layout, get, dtype):
    parts = []
    for name, shape, _, nr in layout:
        flat = get(name).astype(dtype).reshape(-1)
        pad = nr * PACK_COLS - flat.shape[0]
        if pad:
            flat = jnp.pad(flat, (0, pad))
        parts.append(flat.reshape(nr, PACK_COLS))
    return jnp.concatenate(parts, axis=0)


def _unpack(layout, buf):
    out = {}
    for name, shape, r0, nr in layout:
        n = 1
        for dim in shape:
            n *= dim
        out[name] = buf[r0:r0 + nr].reshape(-1)[:n].reshape(shape)
    return out


def _rope_tables(positions):
    inv_freq = ROPE_THETA ** (-jnp.arange(0, QK_ROPE, 2, dtype=F32) / QK_ROPE)
    ang = positions.astype(F32)[:, None] * inv_freq
    cos, sin, zero = jnp.cos(ang), jnp.sin(ang), jnp.zeros_like(ang)
    c_t = jnp.concatenate([cos, cos, zero, zero], axis=1)
    s1_t = jnp.concatenate([-sin, zero, zero, zero], axis=1)
    s2_t = jnp.concatenate([zero, sin, zero, zero], axis=1)
    return c_t, s1_t, s2_t


def _shard_cols(shards, lo, hi):
    ws = shards.shape[2]
    out = []
    for k in range(shards.shape[0]):
        a, b = max(lo, k * ws), min(hi, (k + 1) * ws)
        if a < b:
            out.append(shards[k][:, a - k * ws:b - k * ws])
    return out


def _w_in_to_kernel_layout(shards, nbc, nz):
    d, n_in = shards.shape[1], N_CHIPS * shards.shape[2]
    return jnp.concatenate(_shard_cols(shards, nbc, n_in) + _shard_cols(shards, 0, nbc)
                           + [jnp.zeros((d, nz - n_in), shards.dtype)], axis=1)


def _w_in_grad_to_shards(gx, nbc, n_in):
    ws, n_gate = n_in // N_CHIPS, n_in - nbc
    out = []
    for k in range(N_CHIPS):
        lo, hi, parts = k * ws, (k + 1) * ws, []
        if lo < nbc:
            parts.append(gx[:, n_gate + lo:n_gate + min(hi, nbc)])
        if hi > nbc:
            parts.append(gx[:, max(lo, nbc) - nbc:hi - nbc])
        out.append(parts[0] if len(parts) == 1 else jnp.concatenate(parts, axis=1))
    return jnp.stack(out)


def _w_uq_to_kernel_layout(shards):
    full = jnp.concatenate([shards[k] for k in range(N_CHIPS)], axis=1).reshape(-1, HEADS, QK_NOPE + QK_ROPE)
    pad = jnp.zeros((full.shape[0], HEADS, 256 - QK_NOPE - QK_ROPE), full.dtype)
    return jnp.concatenate([full, pad], axis=2).reshape(-1, HEADS * 256)


def _w_uq_grad_to_shards(gx):
    full = gx.reshape(-1, HEADS, 256)[:, :, :QK_NOPE + QK_ROPE].reshape(gx.shape[0], -1)
    ws = full.shape[1] // N_CHIPS
    return jnp.stack([full[:, k * ws:(k + 1) * ws] for k in range(N_CHIPS)])


def _layer_forward(l, x_in, h, w_in_x, other_weights, sm, tabs, col):
    c_t, s1_t, s2_t = tabs
    tag = f"l{l}"
    z = _mm(h, w_in_x, name=f"z_{tag}")
    b_pool = _pool_fwd(z, col['pool'], sm['pool_w16'], sm['pool_scale'], f"pool_fwd_{tag}")
    b_conv, ypre = _conv_fwd(z, col['conv'], sm['conv_w'], sm['conv_b'], sm['conv_norm_g'], sm['conv_norm_b'],
                             f"conv_fwd_{tag}")
    b_sgu = _sgu_fwd(z, col['sgu'], sm['sgu_norm_g'], sm['sgu_norm_b'], sm['sgu_w16'], sm['sgu_bias_full'],
                     f"sgu_fwd_{tag}")
    qn, kvn, krp = _mla_prep(z, col['q'], col['kv'], col['kr'], sm['q_norm_g'], sm['kv_norm_g'], c_t, s1_t, s2_t,
                             f"mla_prep_{tag}")
    (z, b_pool, b_conv, ypre, b_sgu, qn, kvn, krp), w = other_weights((z, b_pool, b_conv, ypre, b_sgu, qn, kvn, krp))
    w = dict(w, w_in_x=w_in_x)
    q = _mm(qn, w['w_uq_x'], name=f"q_{tag}")
    qb = _q_rope(q, c_t, s1_t, s2_t, f"q_rope_{tag}")
    kv = _mm(kvn, w['w_ukv'], name=f"kv_{tag}", b_split=True, out_dtypes=(BF16,))
    o, lse = _flash_fwd(qb, kv, krp, f"flash_fwd_{tag}")
    branches = (b_pool, b_conv, b_sgu, o)
    projs = (w['pool_proj'], w['conv_proj'], w['sgu_proj'], w['attn_proj'])
    merged = _merge_fwd(z, branches, projs, f"merge_fwd_{tag}")
    o2 = _mm(merged, w['w_out'], name=f"o2_{tag}")
    x1, h2 = _resid_norm_fwd(x_in, o2, sm['post_mix_g'], sm['pre_mlp_g'], f"mix_out_{tag}")
    u, a = _mm(h2, w['w_up'], name=f"up_{tag}", b_split=True, out_dtypes=(F32, BF16),
               epilogue=lambda acc: (acc, jnp.square(jnp.maximum(acc, 0.0))))
    f = _mm(a, w['w_down'], name=f"down_{tag}")
    return dict(x_in=x_in, h=h, z=z, ypre=ypre, branches=branches, projs=projs, qn=qn, kvn=kvn, krp=krp, qb=qb, kv=kv,
                o=o, lse=lse, merged=merged, o2=o2, x1=x1, h2=h2, u=u, a=a, f=f), w


def _layer_backward(l, sv, d_f, d_x2, w, sm, tabs, col, nz, nbc, early_reduce):
    c_t, s1_t, s2_t = tabs
    tag = f"l{l}"
    s, d = sv['x_in'].shape
    gb, gs = {}, {}
    row_shards = lambda g: g.reshape(N_CHIPS, g.shape[0] // N_CHIPS, g.shape[1])
    d_u = _mm(d_f, w['w_down'], name=f"d_u_{tag}", tb=True, out_dtypes=(BF16,), extras=(sv['u'],),
              epilogue=lambda acc, u: (acc * (2.0 * jnp.maximum(u, 0.0)),))
    gb['w_down'] = row_shards(_mm(sv['a'], d_f, name=f"g_down_{tag}", ta=True, out_dtypes=(BF16,)))
    d_h2 = _mm(d_u, w['w_up'], name=f"d_h2_{tag}", tb=True, b_split=True)
    gb['w_up'] = _mm(sv['h2'], d_u, name=f"g_up_{tag}", ta=True, out_split=True, out_dtypes=(BF16,))
    d_x1, d_o2, gs['post_mix_g'], gs['pre_mlp_g'] = _resid_norm_bwd(
        d_x2, d_h2, sv['x1'], sv['o2'], sm['post_mix_g'], sm['pre_mlp_g'], f"mix_out_bwd_{tag}")
    d_merged = _mm(d_o2, w['w_out'], name=f"d_merged_{tag}", tb=True)
    gb['w_out'] = row_shards(_mm(sv['merged'], d_o2, name=f"g_out_{tag}", ta=True, out_dtypes=(BF16,)))
    d_ys, d_gates = _merge_bwd(sv['z'], sv['branches'], sv['projs'], d_merged, f"merge_bwd_{tag}")
    d_br = []
    for k, pname in enumerate(('pool_proj', 'conv_proj', 'sgu_proj', 'attn_proj')):
        last = pname == 'attn_proj'
        d_br.append(_mm(d_ys[k], w[pname], name=f"d_{pname}_in_{tag}", tb=True, b_split=True,
                        out_dtypes=(BF16 if last else F32,)))
        gb[pname] = _mm(sv['branches'][k], d_ys[k], name=f"g_{pname}_{tag}", ta=True, out_split=True,
                        out_dtypes=(BF16,))
    dz_pool, gs['pool_w'], gs['pool_scale'] = _pool_bwd(sv['z'], col['pool'], d_br[0], sm['pool_w16'],
                                                        sm['pool_scale'], f"pool_bwd_{tag}")
    dz_ca, dz_cg, gs['conv_w'], gs['conv_b'], gs['conv_norm_g'], gs['conv_norm_b'] = _conv_bwd(
        sv['z'], col['conv'], d_br[1], sv['ypre'], sm['conv_w'], sm['conv_norm_g'], sm['conv_norm_b'],
        f"conv_bwd_{tag}")
    dz_su, dz_sv, g_sgu_w, g_sgu_bfull, gs['sgu_norm_g'], gs['sgu_norm_b'] = _sgu_bwd(
        sv['z'], col['sgu'], d_br[2], sm['sgu_norm_g'], sm['sgu_norm_b'], sm['sgu_w16'], sm['sgu_bias_full'],
        f"sgu_bwd_{tag}")
    gs['sgu_w'] = g_sgu_w * sm['tril']
    gs['sgu_b'] = jnp.sum(g_sgu_bfull, axis=-1)
    d_qb = _flash_bwd_dq(sv['qb'], sv['kv'], sv['krp'], sv['o'], d_br[3], sv['lse'], c_t, s1_t, s2_t,
                         f"flash_dq_{tag}")
    d_kv, d_krp = _flash_bwd_dkv(sv['qb'], sv['kv'], sv['krp'], sv['o'], d_br[3], sv['lse'], f"flash_dkv_{tag}")
    d_qn = _mm(d_qb, w['w_uq_x'], name=f"d_qn_{tag}", tb=True)
    gb['w_uq'] = _w_uq_grad_to_shards(_mm(sv['qn'], d_qb, name=f"g_uq_{tag}", ta=True, out_dtypes=(BF16,)))
    d_kvn = _mm(d_kv, w['w_ukv'], name=f"d_kvn_{tag}", tb=True, b_split=True)
    gb['w_ukv'] = _mm(sv['kvn'], d_kv, name=f"g_ukv_{tag}", ta=True, out_split=True, out_dtypes=(BF16,))
    dz_q, dz_kv, dz_kr, gs['q_norm_g'], gs['kv_norm_g'] = _mla_prep_bwd(
        sv['z'], col['q'], col['kv'], d_qn, d_kvn, d_krp, sm['q_norm_g'] + early_reduce(gb), sm['kv_norm_g'],
        c_t, s1_t, s2_t, f"mla_prep_bwd_{tag}")
    used = 4 * d + 7 * BR + 128
    dz = jnp.concatenate(list(d_gates) + [dz_pool, dz_ca, dz_cg, dz_su, dz_sv, dz_q, dz_kv, dz_kr,
                                          jnp.zeros((s, nz - used), BF16)], axis=1)
    d_h = _mm(dz, w['w_in_x'], name=f"d_h_{tag}", tb=True)
    gb['w_in'] = _w_in_grad_to_shards(_mm(sv['h'], dz, name=f"g_in_{tag}", ta=True, out_dtypes=(BF16,)),
                                      nbc, 4 * d + nbc)
    return d_x1, d_h, gb, gs


def _reduce_start(tag, names, gb, my_c):
    gs = [gb[n] for n in names]
    sib4 = _swap_grad_halves(gs, f"swap_grads_{tag}")
    own4 = [lax.dynamic_slice_in_dim(g, my_c * (g.shape[1] // 2), g.shape[1] // 2, axis=1) for g in gs]
    partial = [_add_pair(o, g, f"add_sibling_{n}_{tag}") for n, o, g in zip(names, own4, sib4)]
    return _ici_start('scatter', partial, [(3,) + p.shape[1:] for p in partial], f"scatter_start_{tag}")


def _reduce_finish(tag, names, started, my_chip, after):
    own = [lax.dynamic_index_in_dim(p, my_chip, 0, keepdims=False) for p in started[2]]
    lands = _ici_wait('scatter', started, after, f"scatter_wait_{tag}")
    fins = [_sum_chips(ld, ow, f"sum_chips_{n}_{tag}") for n, ld, ow in zip(names, lands, own)]
    return dict(zip(names, fins)), dict(zip(names, _share_final(fins, f"share_final_{tag}")))


def kernel(x, positions, pre_mix_g, w_in, pool_w, pool_scale, pool_proj, conv_w, conv_b, conv_norm_g, conv_norm_b, conv_proj, sgu_norm_g, sgu_norm_b, sgu_w, sgu_b, sgu_proj, q_norm_g, w_uq, kv_norm_g, w_ukv, attn_proj, w_out, post_mix_g, pre_mlp_g, w_up, w_down, post_mlp_g, loss_target, m_pre_mix_g, m_w_in, m_pool_w, m_pool_scale, m_pool_proj, m_conv_w, m_conv_b, m_conv_norm_g, m_conv_norm_b, m_conv_proj, m_sgu_norm_g, m_sgu_norm_b, m_sgu_w, m_sgu_b, m_sgu_proj, m_q_norm_g, m_w_uq, m_kv_norm_g, m_w_ukv, m_attn_proj, m_w_out, m_post_mix_g, m_pre_mlp_g, m_w_up, m_w_down, m_post_mlp_g, v_pre_mix_g, v_w_in, v_pool_w, v_pool_scale, v_pool_proj, v_conv_w, v_conv_b, v_conv_norm_g, v_conv_norm_b, v_conv_proj, v_sgu_norm_g, v_sgu_norm_b, v_sgu_w, v_sgu_b, v_sgu_proj, v_q_norm_g, v_w_uq, v_kv_norm_g, v_w_ukv, v_attn_proj, v_w_out, v_post_mix_g, v_pre_mlp_g, v_w_up, v_w_down, v_post_mlp_g):
    arg = dict(locals())
    depth = pre_mix_g.shape[0]
    s, d = x.shape[1], x.shape[2]
    nbc = 7 * BR + QK_ROPE
    nz = 4 * d + -(-(7 * BR + 128) // 512) * 512
    g0 = 4 * d // BR
    col = dict(pool=g0, conv=g0 + 1, sgu=g0 + 3, q=g0 + 5, kv=g0 + 6, kr=(4 * d + 7 * BR) // 128)
    my_x, my_y, my_c = _place()
    my_chip = 2 * my_x + my_y
    x2 = x.reshape(s, d)
    target = loss_target.reshape(s, d)
    tabs = _rope_tables(positions.reshape(s))
    how_of = dict(BIG)

    rest_names = [n for n in BIG_NAMES if n != 'w_in']

    def gather_start(tag, names, l, gate):
        srcs = [arg[n][l].astype(BF16) for n in names]
        if gate is not None:
            srcs, _ = lax.optimization_barrier((srcs, gate))
        return names, srcs, _ici_start('gather', srcs, [(N_CHIPS,) + t.shape for t in srcs], f"gather_start_{tag}")

    def gather_finish(tag, group, after):
        names, srcs, started = group
        lands = _ici_wait('gather', started, after, f"gather_wait_{tag}")
        full = dict(zip(names, _share_weights(srcs, lands, f"share_weights_{tag}")))
        w_l = {n: (full[n] if how_of[n] == 'col' else full[n].reshape(-1, full[n].shape[2]))
               for n in names if n not in ('w_in', 'w_uq')}
        if 'w_in' in full:
            w_l['w_in_x'] = _w_in_to_kernel_layout(full['w_in'], nbc, nz)
        if 'w_uq' in full:
            w_l['w_uq_x'] = _w_uq_to_kernel_layout(full['w_uq'])
        return w_l

    def behind(value, group):
        return lax.optimization_barrier((value, group[2][4]))[0]

    first_in = gather_start("l0_in", ['w_in'], 0, None)

    conv_layout, _ = _layout([('conv_w', conv_w.shape)])
    conv_all = _gather_all(_pack(conv_layout, lambda n: arg[n], F32), "gather_conv_w")
    conv_w_full = jnp.concatenate([_unpack(conv_layout, conv_all[4 * (k // 2) + 2 * (k % 2)])['conv_w']
                                   for k in range(4)], axis=-1)
    tril = jnp.tril(jnp.ones((GROUP, GROUP), F32))
    smalls = []
    for l in range(depth):
        sm = {n: arg[n][l][None, :] for n in ('pre_mix_g', 'pool_scale', 'conv_b', 'conv_norm_g', 'conv_norm_b',
                                               'sgu_norm_g', 'sgu_norm_b', 'q_norm_g', 'kv_norm_g', 'post_mix_g',
                                               'pre_mlp_g', 'post_mlp_g')}
        sm['pool_w16'] = pool_w[l].astype(BF16)
        sm['sgu_w16'] = (sgu_w[l] * tril).astype(BF16)
        sm['sgu_bias_full'] = jnp.broadcast_to(sgu_b[l][:, :, None], (BR // GROUP, GROUP, GROUP))
        sm['conv_w'] = conv_w_full[l]
        sm['tril'] = tril
        smalls.append(sm)

    weights, saved, groups = [], [], {}
    x_cur = x2
    h = _rms_fwd(x2, smalls[0]['pre_mix_g'] + first_in[2][4][0, 0], "rms_in")
    w_in_x = gather_finish("l0_in", first_in, h)['w_in_x']
    groups[0] = gather_start("l0_rest", rest_names, 0, w_in_x)
    h = behind(h, groups[0])

    def start_next(l, z, gate):
        if l + 1 < depth:
            groups[l + 1] = gather_start(f"l{l + 1}", BIG_NAMES, l + 1, gate)
            z = behind(z, groups[l + 1])
        return z

    for l in range(depth):
        if l == 0:
            def other_weights(vals):
                vals = lax.optimization_barrier(vals)
                w_l = gather_finish("l0_rest", groups[0], vals[-1])
                return start_next(0, vals, w_l['w_down']), w_l
        else:
            w_l = gather_finish(f"l{l}", groups[l], saved[l - 1]['f'])
            w_in_x = w_l.pop('w_in_x')

            def other_weights(z, w_l=w_l, l=l):
                return start_next(l, z, w_l['w_down']), w_l
        sv, w_full = _layer_forward(l, x_cur, h, w_in_x, other_weights, smalls[l], tabs, col)
        weights.append(w_full)
        saved.append(sv)
        if l + 1 < depth:
            x_cur, h = _resid_norm_fwd(sv['x1'], sv['f'], smalls[l]['post_mlp_g'], smalls[l + 1]['pre_mix_g'],
                                       f"mlp_out_l{l}")
    d_y, loss_local = _resid_norm_loss(saved[-1]['x1'], saved[-1]['f'], smalls[-1]['post_mlp_g'], target, "loss_head")

    reduce_groups = (("a", rest_names), ("b", ['w_in']))
    reduce_started = [None] * depth
    fin_mine, fin_other = [dict() for _ in range(depth)], [dict() for _ in range(depth)]
    grads_small = [dict() for _ in range(depth)]

    def finish_group(l, key, names, after):
        mine, other = _reduce_finish(f"{key}_l{l}", names, reduce_started[l][key], my_chip, after)
        fin_mine[l].update(mine)
        fin_other[l].update(other)

    def finish_layer(l, after):
        for key, names in reduce_groups:
            finish_group(l, key, names, after)

    d_x2 = d_y
    _, d_f, g_post, _ = _resid_norm_bwd(d_y, None, None, saved[-1]['f'], smalls[-1]['post_mlp_g'], None,
                                        f"mlp_out_bwd_l{depth - 1}")
    grads_small[-1]['post_mlp_g'] = g_post
    grad_x = None
    for l in reversed(range(depth)):
        started_l = {}

        def early_reduce(gb, l=l, started_l=started_l):
            started_l["a"] = _reduce_start(f"a_l{l}", reduce_groups[0][1], gb, my_c)
            return started_l["a"][4][0, 0]

        d_x1, d_h, gb, gs = _layer_backward(l, saved[l], d_f, d_x2, weights[l], smalls[l], tabs, col, nz, nbc,
                                            early_reduce)
        grads_small[l].update(gs)
        started_l["b"] = _reduce_start(f"b_l{l}", reduce_groups[1][1], gb, my_c)
        reduce_started[l] = started_l
        pin = started_l["b"][4][0, 0]
        if l > 0:
            d_x2, d_f, g_post, g_pre = _resid_norm_bwd(d_x1, d_h, saved[l]['x_in'], saved[l - 1]['f'],
                                                      smalls[l - 1]['post_mlp_g'], smalls[l]['pre_mix_g'] + pin,
                                                      f"mlp_out_bwd_l{l - 1}")
            grads_small[l - 1]['post_mlp_g'] = g_post
            grads_small[l]['pre_mix_g'] = g_pre
        else:
            grad_x, _, _, g_pre = _resid_norm_bwd(d_x1, d_h, saved[0]['x_in'], None, None,
                                                  smalls[0]['pre_mix_g'] + pin, "rms_in_bwd")
            grads_small[0]['pre_mix_g'] = g_pre
        if l + 1 < depth:
            finish_layer(l + 1, d_h)

    small_shapes = []
    for n in SMALL_NAMES:
        shp = arg[n].shape[1:]
        if n == 'conv_w':
            shp = (shp[0], shp[1] * 4)
        small_shapes.append((n, shp))
    small_layout, _ = _layout([(f"{n}.{l}", shp) for l in range(depth) for n, shp in small_shapes])

    def small_get(key):
        n, l = key.rsplit('.', 1)
        return grads_small[int(l)][n]

    small_all = _gather_all(_pack(small_layout, small_get, F32), "gather_small_grads")
    g_small = _unpack(small_layout, _sum_slots(small_all, "sum_small_grads"))

    grad, delta, new_m, new_v = {}, {}, {}, {}
    finish_group(0, *reduce_groups[0], small_all)
    for n in rest_names:
        grad[n], delta[n], new_m[n], new_v[n] = _adamw_big(
            arg[n], [fin_mine[l][n] for l in range(depth)], [fin_other[l][n] for l in range(depth)],
            arg['m_' + n], arg['v_' + n], f"adamw_{n}")
    finish_group(0, *reduce_groups[1], new_v[rest_names[-1]])
    grad['w_in'], delta['w_in'], new_m['w_in'], new_v['w_in'] = _adamw_transposed(
        w_in, [fin_mine[l]['w_in'] for l in range(depth)], [fin_other[l]['w_in'] for l in range(depth)],
        m_w_in, v_w_in, my_c, "adamw_w_in")
    for n in WEIGHT_NAMES:
        if n in BIG_NAMES:
            continue
        g = jnp.stack([g_small[f"{n}.{l}"] for l in range(depth)])
        if n == 'conv_w':
            wd = conv_w.shape[2]
            g = lax.dynamic_slice_in_dim(g, my_chip * wd, wd, axis=2)
        grad[n] = g
        delta[n], new_m[n], new_v[n] = _adamw(arg[n], g, arg['m_' + n], arg['v_' + n], f"adamw_{n}")

    loss = lax.psum(loss_local, ("x", "y", "c"))
    return (loss, grad_x.reshape(x.shape), *[grad[n] for n in WEIGHT_NAMES], *[delta[n] for n in WEIGHT_NAMES],
            *[new_m[n] for n in WEIGHT_NAMES], *[new_v[n] for n in WEIGHT_NAMES])
```
